```python
import jax
import jax.numpy as jnp
from jax import lax
import numpy as np

D_MODEL = 2048
BATCH = 4
SEQ = 4096
DEPTH = 4

GRID_W = 64
CTX_LEN = 256
N_MIXERS = 4
N_MOD = 6
CHUNK = 64
SC_WIDTH = 3
GLA_HEADS = 4
GLA_DK = D_MODEL // (2 * GLA_HEADS)
GLA_DV = D_MODEL // GLA_HEADS
GLA_RANK = 16
GLA_NORMALIZER = 16.0
GLA_IN = 2 * GLA_HEADS * GLA_DK + 2 * GLA_HEADS * GLA_DV + 2 * GLA_RANK
LRU_WIDTH = D_MODEL
LRU_BLOCKS = 8
LRU_BW = LRU_WIDTH // LRU_BLOCKS
LRU_CONV = 4
RG_C = 8.0
HG_DIM = 128
HG_HEADS = D_MODEL // HG_DIM
N_EXPERTS = 32
TOP_K = 4
D_FF = 768
SWIGLU_LIMIT = 7.0
SWIGLU_ALPHA = 1.702
MOE_BLOCK = 128
LN_EPS = 1e-5
RMS_EPS = 1e-6
DN_ALPHA = (2 * DEPTH) ** 0.25
DN_BETA = (8 * DEPTH) ** -0.25
N_SC = len(range(0, DEPTH, N_MIXERS))
N_GLA = len(range(1, DEPTH, N_MIXERS))
N_LRU = len(range(2, DEPTH, N_MIXERS))
N_HG = len(range(3, DEPTH, N_MIXERS))

kernel_name = 'hybrid_interleaved_flow_backbone'

F32 = jnp.float32


def layer_norm(x, g, b):
    xf = x.astype(F32)
    mu = xf.mean(-1, keepdims=True)
    var = jnp.square(xf - mu).mean(-1, keepdims=True)
    return ((xf - mu) * lax.rsqrt(var + LN_EPS) * g + b).astype(x.dtype)


def rms_norm_gate(o, g, gain):
    B, L, H, dv = o.shape
    o = o * lax.rsqrt(jnp.mean(o * o, -1, keepdims=True) + RMS_EPS) * gain.astype(F32)
    return o.reshape(B, L, H * dv) * jax.nn.silu(g.astype(F32))


def dwconv(u, w, b=None):
    K = w.shape[0]
    left = (K - 1) // 2
    L = u.shape[-2]
    up = jnp.pad(u, [(0, 0)] * (u.ndim - 2) + [(left, K - 1 - left), (0, 0)])
    y = sum(up[..., j:j + L, :] * w[j] for j in range(K))
    return y if b is None else y + b


def grid_row_conv(u, w, b=None):
    B, L, C = u.shape
    rows = L // GRID_W
    return dwconv(u.reshape(B, rows, GRID_W, C), w, b).reshape(B, L, C)


def flip_seq(*ts):
    return tuple(jnp.flip(t, axis=1) for t in ts)


def chunked_gla(q, k, v, log_g, s0):
    B, L, H, _ = q.shape
    dv = v.shape[-1]
    n = L // CHUNK
    blocks = lambda t: jnp.moveaxis(t.reshape(B, n, CHUNK, H, t.shape[-1]), 1, 0)
    mask = jnp.tril(jnp.ones((CHUNK, CHUNK), bool))

    def step(S, inp):
        qc, kc, vc, gc = inp
        b = jnp.cumsum(gc, axis=1)
        qd = qc * jnp.exp(b)
        kd = kc * jnp.exp(-b)
        att = jnp.where(mask, jnp.einsum('bthk,bshk->bhts', qd, kd), 0.0)
        o = jnp.einsum('bhts,bshv->bthv', att, vc) + jnp.einsum('bthk,bhkv->bthv', qd, S)
        b_end = b[:, -1]
        S = S * jnp.exp(b_end)[..., None] + jnp.einsum(
            'bshk,bshv->bhkv', kc * jnp.exp(b_end[:, None] - b), vc)
        return S, o

    S, o = lax.scan(step, s0, (blocks(q), blocks(k), blocks(v), blocks(log_g)))
    return jnp.moveaxis(o, 0, 1).reshape(B, L, H, dv), S


def bidir_gla(ctx_f, ctx_b, lat_f, lat_b):
    B, _, H, dk = ctx_f[0].shape
    dv = ctx_f[2].shape[-1]
    s0 = jnp.zeros((B, H, dk, dv), F32)
    oc_f, s_f = chunked_gla(*ctx_f, s0)
    oc_b, s_b = chunked_gla(*flip_seq(*ctx_b), s0)
    ol_f, _ = chunked_gla(*lat_f, s_f)
    ol_b, _ = chunked_gla(*flip_seq(*lat_b), s_b)
    return oc_f + jnp.flip(oc_b, 1), ol_f + jnp.flip(ol_b, 1)


def linear_scan(a, u, h0):
    u = u.at[:, 0].add(a[:, 0] * h0)
    combine = lambda l, r: (l[0] * r[0], r[0] * l[1] + r[1])
    return lax.associative_scan(combine, (a, u), axis=1)[1]


def short_conv_mixer(hc, hl, w_in, w_conv, w_out, with_ctx):
    def mix(h, conv):
        bg, cg, v = jnp.split(h @ w_in, 3, axis=-1)
        return (bg * conv(cg * v, w_conv)) @ w_out
    yc = mix(hc, dwconv) if with_ctx else None
    return yc, mix(hl, grid_row_conv)


def gla_mixer(hc, hl, w_in, w_gate2, b_gate, norm_g, w_out, with_ctx):
    KD, VD = GLA_HEADS * GLA_DK, GLA_HEADS * GLA_DV
    splits = [KD, 2 * KD, 2 * KD + VD, 2 * KD + 2 * VD, 2 * KD + 2 * VD + GLA_RANK]

    def project(h):
        B, L, _ = h.shape
        q, k, v, g, rf, rb = jnp.split(h @ w_in, splits, axis=-1)
        heads = lambda t, d: t.reshape(B, L, GLA_HEADS, d).astype(F32)
        q = heads(q, GLA_DK) * GLA_DK ** -0.5
        k = heads(k, GLA_DK)
        v = heads(v, GLA_DV)
        log_gate = lambda r, d: jax.nn.log_sigmoid(
            (r @ w_gate2[d] + b_gate[d]).astype(F32)).reshape(B, L, GLA_HEADS, GLA_DK) / GLA_NORMALIZER
        return (q, k, v, log_gate(rf, 0)), (q, k, v, log_gate(rb, 1)), g

    cf, cb, gc = project(hc)
    lf, lb, gl = project(hl)
    oc, ol = bidir_gla(cf, cb, lf, lb)
    readout = lambda o, g, h: rms_norm_gate(o, g, norm_g).astype(h.dtype) @ w_out
    yc = readout(oc, gc, hc) if with_ctx else None
    return yc, readout(ol, gl, hl)


def rglru_mixer(hc, hl, w_in, w_conv, b_conv, w_gate, b_gate, lam, w_out, with_ctx):
    def branches(h, conv):
        B, L, _ = h.shape
        y, xb = jnp.split(h @ w_in, 2, axis=-1)
        xc = conv(xb, w_conv, b_conv)
        pre = jnp.einsum('blnc,dgncm->dgblnm', xc.reshape(B, L, LRU_BLOCKS, LRU_BW), w_gate)
        gates = jax.nn.sigmoid((pre.reshape(2, 2, B, L, LRU_WIDTH)
                                + b_gate[:, :, None, None, :]).astype(F32))
        log_a = RG_C * gates[:, 0] * jax.nn.log_sigmoid(lam.astype(F32))[:, None, None, :]
        u = jnp.sqrt(-jnp.expm1(2.0 * log_a)) * gates[:, 1] * xc.astype(F32)
        return jax.nn.gelu(y), jnp.exp(log_a), u

    yc, ac, uc = branches(hc, dwconv)
    yl, al, ul = branches(hl, grid_row_conv)
    zero = jnp.zeros((hc.shape[0], LRU_WIDTH), F32)
    hcf = linear_scan(ac[0], uc[0], zero)
    hcb = linear_scan(*flip_seq(ac[1], uc[1]), zero)
    hlf = linear_scan(al[0], ul[0], hcf[:, -1])
    hlb = linear_scan(*flip_seq(al[1], ul[1]), hcb[:, -1])
    readout = lambda r, y, h: (r.astype(h.dtype) * y) @ w_out
    out_c = readout(hcf + jnp.flip(hcb, 1), yc, hc) if with_ctx else None
    return out_c, readout(hlf + jnp.flip(hlb, 1), yl, hl)


def hgrn2_lower_bound(raw, layer):
    p = jax.nn.softmax(raw.astype(F32), axis=0)
    return jnp.cumsum(p, axis=0)[layer] - p[0]


def hgrn2_mixer(hc, hl, w_in, lb, norm_g, w_out, with_ctx):
    log_lb = jnp.log(lb).reshape(HG_HEADS, HG_DIM)
    log_1m_lb = jnp.log1p(-lb).reshape(HG_HEADS, HG_DIM)

    def project(h):
        B, L, _ = h.shape
        q, ff, fb, i, g = jnp.split(h @ w_in, 5, axis=-1)
        heads = lambda t: t.reshape(B, L, HG_HEADS, HG_DIM).astype(F32)
        q = jax.nn.silu(heads(q)) * HG_DIM ** -0.5
        v = heads(i)

        def direction(f):
            log_f = jnp.logaddexp(log_lb, log_1m_lb + jax.nn.log_sigmoid(heads(f)))
            return (q, -jnp.expm1(log_f), v, log_f)
        return direction(ff), direction(fb), g

    cf, cb, gc = project(hc)
    lf, lbk, gl = project(hl)
    oc, ol = bidir_gla(cf, cb, lf, lbk)
    readout = lambda o, g, h: rms_norm_gate(o, g, norm_g).astype(h.dtype) @ w_out
    yc = readout(oc, gc, hc) if with_ctx else None
    return yc, readout(ol, gl, hl)


def moe_ffn(h, w_router, b_router, w_gu, b_gu, w_down, b_down):
    N, D = h.shape
    logits = (h @ w_router + b_router).astype(F32)
    top_v, top_e = lax.top_k(logits, TOP_K)
    gate = jax.nn.softmax(top_v, axis=-1).reshape(-1)
    e_flat = top_e.reshape(-1)
    n_assign = N * TOP_K
    order = jnp.argsort(e_flat)
    e_sorted = e_flat[order]
    counts = jnp.bincount(e_flat, length=N_EXPERTS)
    padded = (counts + MOE_BLOCK - 1) // MOE_BLOCK * MOE_BLOCK
    pad_end = jnp.cumsum(padded)
    dest = ((pad_end - padded)[e_sorted] + jnp.arange(n_assign)
            - (jnp.cumsum(counts) - counts)[e_sorted])
    n_blocks = -(-(n_assign + N_EXPERTS * (MOE_BLOCK - 1)) // MOE_BLOCK)
    n_rows = n_blocks * MOE_BLOCK
    row_tok = jnp.full((n_rows,), N, jnp.int32).at[dest].set((order // TOP_K).astype(jnp.int32))
    row_gate = jnp.zeros((n_rows,), F32).at[dest].set(gate[order])
    block_e = jnp.minimum(
        jnp.searchsorted(pad_end, jnp.arange(n_blocks) * MOE_BLOCK, side='right'), N_EXPERTS - 1)
    h_pad = jnp.concatenate([h, jnp.zeros((1, D), h.dtype)], axis=0)

    def expert_block(args):
        rows, e = args
        g, u = jnp.split(h_pad[rows] @ w_gu[e] + b_gu[e], 2, axis=-1)
        g = jnp.minimum(g, SWIGLU_LIMIT)
        u = jnp.clip(u, -SWIGLU_LIMIT, SWIGLU_LIMIT)
        return (g * jax.nn.sigmoid(SWIGLU_ALPHA * g) * (u + 1.0)) @ w_down[e] + b_down[e]

    y = lax.map(expert_block, (row_tok.reshape(n_blocks, MOE_BLOCK), block_e))
    y = y.reshape(n_rows, D).astype(F32) * row_gate[:, None]
    return jax.ops.segment_sum(y, row_tok, num_segments=N + 1)[:N].astype(h.dtype)


def setup_inputs(seed: int = 0) -> dict:
    key = jax.random.key(seed)
    ks = iter(jax.random.split(key, 40))
    nrm = lambda shape, scale: jax.random.normal(next(ks), shape, F32) * scale
    D = D_MODEL
    KD, VD = GLA_HEADS * GLA_DK, GLA_HEADS * GLA_DV
    HW = HG_HEADS * HG_DIM
    u = jax.random.uniform(next(ks), (N_LRU, 2, LRU_WIDTH), F32, 0.9, 0.999)
    a = u ** (1.0 / RG_C)
    lru_lambda = jnp.log(a) - jnp.log1p(-a)
    return {
        'x': nrm((BATCH, SEQ, D), 1.0),
        'c': nrm((BATCH, D), 1.0),
        'ctx': nrm((BATCH, CTX_LEN, D), 1.0),
        'c_ctx': nrm((D,), 1.0),
        'ada_w': nrm((DEPTH, D, N_MOD * D), D ** -0.5),
        'ada_b': nrm((DEPTH, N_MOD * D), 0.02),
        'ln_g': 1.0 + nrm((DEPTH, 2, D), 0.02),
        'ln_b': nrm((DEPTH, 2, D), 0.02),
        'sc_w_in': nrm((N_SC, D, 3 * D), D ** -0.5),
        'sc_conv': nrm((N_SC, SC_WIDTH, D), SC_WIDTH ** -0.5),
        'sc_w_out': nrm((N_SC, D, D), D ** -0.5 * DN_BETA),
        'gla_w_in': nrm((N_GLA, D, GLA_IN), D ** -0.5),
        'gla_w_gate2': nrm((N_GLA, 2, GLA_RANK, KD), GLA_RANK ** -0.5),
        'gla_b_gate': nrm((N_GLA, 2, KD), 0.5),
        'gla_norm': 1.0 + nrm((N_GLA, GLA_DV), 0.02),
        'gla_w_out': nrm((N_GLA, VD, D), VD ** -0.5 * DN_BETA),
        'lru_w_in': nrm((N_LRU, D, 2 * LRU_WIDTH), D ** -0.5),
        'lru_conv': nrm((N_LRU, LRU_CONV, LRU_WIDTH), LRU_CONV ** -0.5),
        'lru_conv_b': nrm((N_LRU, LRU_WIDTH), 0.02),
        'lru_w_gate': nrm((N_LRU, 2, 2, LRU_BLOCKS, LRU_BW, LRU_BW), LRU_BW ** -0.5),
        'lru_b_gate': nrm((N_LRU, 2, 2, LRU_WIDTH), 0.02),
        'lru_lambda': lru_lambda,
        'lru_w_out': nrm((N_LRU, LRU_WIDTH, D), LRU_WIDTH ** -0.5 * DN_BETA),
        'hg_w_in': nrm((N_HG, D, 5 * HW), D ** -0.5),
        'hg_lb_raw': nrm((DEPTH, HW), 0.1),
        'hg_norm': 1.0 + nrm((N_HG, HG_DIM), 0.02),
        'hg_w_out': nrm((N_HG, HW, D), HW ** -0.5 * DN_BETA),
        'moe_w_router': nrm((DEPTH, D, N_EXPERTS), D ** -0.5),
        'moe_b_router': nrm((DEPTH, N_EXPERTS), 0.01),
        'moe_w_gu': nrm((DEPTH, N_EXPERTS, D, 2 * D_FF), D ** -0.5),
        'moe_b_gu': nrm((DEPTH, N_EXPERTS, 2 * D_FF), 0.02),
        'moe_w_down': nrm((DEPTH, N_EXPERTS, D_FF, D), D_FF ** -0.5 * DN_BETA),
        'moe_b_down': nrm((DEPTH, N_EXPERTS, D), 0.02),
    }


def reference(x, c, ctx, c_ctx, ada_w, ada_b, ln_g, ln_b, sc_w_in, sc_conv, sc_w_out,
              gla_w_in, gla_w_gate2, gla_b_gate, gla_norm, gla_w_out,
              lru_w_in, lru_conv, lru_conv_b, lru_w_gate, lru_b_gate, lru_lambda, lru_w_out,
              hg_w_in, hg_lb_raw, hg_norm, hg_w_out,
              moe_w_router, moe_b_router, moe_w_gu, moe_b_gu, moe_w_down, moe_b_down):
    xl, xc = x, ctx
    B, L, D = x.shape
    for i in range(DEPTH):
        kind, j = i % N_MIXERS, i // N_MIXERS
        with_ctx = i < DEPTH - 1
        mod_l = jnp.split((jax.nn.silu(c) @ ada_w[i] + ada_b[i])[:, None, :], N_MOD, axis=-1)
        mod_c = jnp.split(jax.nn.silu(c_ctx) @ ada_w[i] + ada_b[i], N_MOD, axis=-1)
        hl = xl * (1.0 + mod_l[1]) + mod_l[0]
        hc = xc * (1.0 + mod_c[1]) + mod_c[0]
        if kind == 0:
            yc, yl = short_conv_mixer(hc, hl, sc_w_in[j], sc_conv[j], sc_w_out[j], with_ctx)
        elif kind == 1:
            yc, yl = gla_mixer(hc, hl, gla_w_in[j], gla_w_gate2[j], gla_b_gate[j],
                               gla_norm[j], gla_w_out[j], with_ctx)
        elif kind == 2:
            yc, yl = rglru_mixer(hc, hl, lru_w_in[j], lru_conv[j], lru_conv_b[j], lru_w_gate[j],
                                 lru_b_gate[j], lru_lambda[j], lru_w_out[j], with_ctx)
        else:
            yc, yl = hgrn2_mixer(hc, hl, hg_w_in[j], hgrn2_lower_bound(hg_lb_raw, i),
                                 hg_norm[j], hg_w_out[j], with_ctx)
        xl = layer_norm(DN_ALPHA * xl + mod_l[2] * yl, ln_g[i, 0], ln_b[i, 0])
        hl = xl * (1.0 + mod_l[4]) + mod_l[3]
        moe_args = (moe_w_router[i], moe_b_router[i], moe_w_gu[i], moe_b_gu[i],
                    moe_w_down[i], moe_b_down[i])
        if with_ctx:
            xc = layer_norm(DN_ALPHA * xc + mod_c[2] * yc, ln_g[i, 0], ln_b[i, 0])
            hc = xc * (1.0 + mod_c[4]) + mod_c[3]
            n_ctx = hc.shape[0] * hc.shape[1]
            y = moe_ffn(jnp.concatenate([hc.reshape(-1, D), hl.reshape(-1, D)], axis=0), *moe_args)
            xc = layer_norm(DN_ALPHA * xc + mod_c[5] * y[:n_ctx].reshape(hc.shape),
                            ln_g[i, 1], ln_b[i, 1])
            yl = y[n_ctx:].reshape(B, L, D)
        else:
            yl = moe_ffn(hl.reshape(-1, D), *moe_args).reshape(B, L, D)
        xl = layer_norm(DN_ALPHA * xl + mod_l[5] * yl, ln_g[i, 1], ln_b[i, 1])
    return xl
```

```python
import functools

import jax
import jax.numpy as jnp
from jax import lax
from jax.experimental import pallas as pl
from jax.experimental.pallas import tpu as pltpu

F32 = jnp.float32
BF16 = jnp.bfloat16

GRID_W = 64
CHUNK = 64
TOP_K = 4
N_MOD = 6
MOD_ROWS = 8
GLA_NORMALIZER = 16.0
RG_C = 8.0
SWIGLU_LIMIT = 7.0
SWIGLU_ALPHA = 1.702
LN_EPS = 1e-5
RMS_EPS = 1e-6
MIB = 1024 * 1024


def _params(sem, vmem_mib):
    return pltpu.CompilerParams(dimension_semantics=sem, vmem_limit_bytes=vmem_mib * MIB)


def _dot(a, b):
    return jnp.dot(a, b, preferred_element_type=F32)


def _dot_nt(a, b):
    return lax.dot_general(a, b, (((1,), (1,)), ((), ())), preferred_element_type=F32)


def _dot_tn(a, b):
    return lax.dot_general(a, b, (((0,), (0,)), ((), ())), preferred_element_type=F32)


def _log_sigmoid(x):
    return jnp.minimum(x, 0.0) - jnp.log1p(jnp.exp(-jnp.abs(x)))


def _pick(n, pref):
    t = min(n, pref)
    while n % t:
        t //= 2
    return t


class _Rows:
    def __init__(self, batch, n_ctx, n_lat):
        self.batch, self.n_ctx, self.n_lat = batch, n_ctx, n_lat
        self.ctx_rows = batch * n_ctx
        self.rows = self.ctx_rows + batch * n_lat

    def tile(self, pref):
        t = min(pref, self.n_lat)
        while self.n_lat % t or self.ctx_rows % t:
            t //= 2
        return t

    def mod_map(self, layer, which, rt):
        def index(i, *_):
            r0 = i * rt
            row = jnp.where(r0 < self.ctx_rows, self.batch, (r0 - self.ctx_rows) // self.n_lat)
            return ((layer * MOD_ROWS + row) * N_MOD + which, 0, 0)
        return index


def _adaln_kernel(c_ref, w_ref, b_ref, o_ref):
    c = c_ref[...]
    a = (c * jax.nn.sigmoid(c)).astype(BF16)
    o_ref[0] = _dot(a, w_ref[0].astype(BF16)) + b_ref[0]


def _adaln(cc, ada_w, ada_b):
    depth, d, n = ada_w.shape
    tn = _pick(n, 1024)
    return pl.pallas_call(
        _adaln_kernel,
        grid=(depth, n // tn),
        in_specs=[pl.BlockSpec((MOD_ROWS, d), lambda l, j: (0, 0)),
                  pl.BlockSpec((1, d, tn), lambda l, j: (l, 0, j)),
                  pl.BlockSpec((1, 1, tn), lambda l, j: (l, 0, j))],
        out_specs=pl.BlockSpec((1, MOD_ROWS, tn), lambda l, j: (l, 0, j)),
        out_shape=jax.ShapeDtypeStruct((depth, MOD_ROWS, n), F32),
        compiler_params=_params(("arbitrary", "arbitrary"), 40),
        name="adaln",
    )(cc, ada_w, ada_b.reshape(depth, 1, n))


def _modulate(x_ref, sh_ref, sc_ref, h_ref):
    @pl.when(pl.program_id(1) == 0)
    def _():
        h_ref[...] = (x_ref[...] * (1.0 + sc_ref[0]) + sh_ref[0]).astype(BF16)


def _segment_pos(shape, ctx_tiles, seg_ctx, seg_lat):
    mask = jnp.where(pl.program_id(0) < ctx_tiles, seg_ctx - 1, seg_lat - 1)
    return lax.broadcasted_iota(jnp.int32, shape, 0) & mask, mask


def _shift_rows(p, pos, seg_last, offset):
    rt = p.shape[0]
    rolled = pltpu.roll(p, (-offset) % rt, 0)
    ok = (pos + offset >= 0) & (pos + offset <= seg_last)
    return jnp.where(ok, rolled, 0.0)


def _proj_kernel(x_ref, sh_ref, sc_ref, w_ref, o_ref, h_ref):
    _modulate(x_ref, sh_ref, sc_ref, h_ref)
    o_ref[...] = _dot(h_ref[...], w_ref[...]).astype(o_ref.dtype)


def _sconv_in_kernel(x_ref, sh_ref, sc_ref, wb_ref, wc_ref, wv_ref, cw_ref, o_ref, h_ref,
                     *, ctx_tiles, seg_ctx, seg_lat):
    _modulate(x_ref, sh_ref, sc_ref, h_ref)
    h = h_ref[...]
    bg = _dot(h, wb_ref[...])
    p = _dot(h, wc_ref[...]) * _dot(h, wv_ref[...])
    pos, last = _segment_pos(p.shape, ctx_tiles, seg_ctx, seg_lat)
    cw = cw_ref[...]
    conv = (cw[0:1] * _shift_rows(p, pos, last, -1) + cw[1:2] * p
            + cw[2:3] * _shift_rows(p, pos, last, 1))
    o_ref[...] = (bg * conv).astype(o_ref.dtype)


def _lru_in_kernel(x_ref, sh_ref, sc_ref, wy_ref, wx_ref, cw_ref, cb_ref, y_ref, xc_ref, h_ref,
                   *, ctx_tiles, seg_ctx, seg_lat):
    _modulate(x_ref, sh_ref, sc_ref, h_ref)
    h = h_ref[...]
    y_ref[...] = jax.nn.gelu(_dot(h, wy_ref[...])).astype(y_ref.dtype)
    xb = _dot(h, wx_ref[...])
    pos, last = _segment_pos(xb.shape, ctx_tiles, seg_ctx, seg_lat)
    cw = cw_ref[...]
    conv = (cw[0:1] * _shift_rows(xb, pos, last, -1) + cw[1:2] * xb
            + cw[2:3] * _shift_rows(xb, pos, last, 1) + cw[3:4] * _shift_rows(xb, pos, last, 2))
    xc_ref[...] = (conv + cb_ref[...]).astype(xc_ref.dtype)


def _in_proj(kind, rs, x, mods, layer, weights, n_out, out_dtypes, extra=(), tn_pref=256, rt_pref=1024):
    rows, d = x.shape
    rt = rs.tile(rt_pref)
    tn = _pick(n_out, tn_pref)
    nj = n_out // tn
    in_specs = [pl.BlockSpec((rt, d), lambda i, j: (i, 0)),
                pl.BlockSpec((1, 1, d), rs.mod_map(layer, 0, rt)),
                pl.BlockSpec((1, 1, d), rs.mod_map(layer, 1, rt))]
    args = [x, mods, mods]
    for w, off in weights:
        in_specs.append(pl.BlockSpec((d, tn), functools.partial(lambda i, j, o: (0, o * nj + j), o=off)))
        args.append(w)
    for e in extra:
        in_specs.append(pl.BlockSpec((e.shape[0], tn), lambda i, j: (0, j)))
        args.append(e)
    out_specs = [pl.BlockSpec((rt, tn), lambda i, j: (i, j)) for _ in out_dtypes]
    out_shape = [jax.ShapeDtypeStruct((rows, n_out), dt) for dt in out_dtypes]
    ctx_tiles = rs.ctx_rows // rt
    if kind == "plain":
        body = _proj_kernel
    elif kind == "sconv":
        body = functools.partial(_sconv_in_kernel, ctx_tiles=ctx_tiles, seg_ctx=rs.n_ctx, seg_lat=GRID_W)
    else:
        body = functools.partial(_lru_in_kernel, ctx_tiles=ctx_tiles, seg_ctx=rs.n_ctx, seg_lat=GRID_W)
    single = len(out_dtypes) == 1
    return pl.pallas_call(
        body,
        grid=(rows // rt, nj),
        in_specs=in_specs,
        out_specs=out_specs[0] if single else out_specs,
        out_shape=out_shape[0] if single else out_shape,
        scratch_shapes=[pltpu.VMEM((rt, d), BF16)],
        compiler_params=_params(("arbitrary", "arbitrary"), 52),
        name=kind + "_in_proj",
    )(*args)


def _gla_gate_kernel(x_ref, sh_ref, sc_ref, wr_ref, w2_ref, b2_ref, o_ref):
    h = (x_ref[...] * (1.0 + sc_ref[0]) + sh_ref[0]).astype(BF16)
    r = _dot(h, wr_ref[...]).astype(BF16)
    pre = _dot(r, w2_ref[...]) + b2_ref[...]
    o_ref[...] = _log_sigmoid(pre) * (1.0 / GLA_NORMALIZER)


def _gla_gates(rs, x, mods, layer, w_r, w2, b2):
    rows, d = x.shape
    rt = rs.tile(256)
    n = w2.shape[1]
    return pl.pallas_call(
        _gla_gate_kernel,
        grid=(rows // rt,),
        in_specs=[pl.BlockSpec((rt, d), lambda i: (i, 0)),
                  pl.BlockSpec((1, 1, d), rs.mod_map(layer, 0, rt)),
                  pl.BlockSpec((1, 1, d), rs.mod_map(layer, 1, rt)),
                  pl.BlockSpec(w_r.shape, lambda i: (0, 0)),
                  pl.BlockSpec(w2.shape, lambda i: (0, 0)),
                  pl.BlockSpec((1, n), lambda i: (0, 0))],
        out_specs=pl.BlockSpec((rt, n), lambda i: (i, 0)),
        out_shape=jax.ShapeDtypeStruct((rows, n), F32),
        compiler_params=_params(("arbitrary",), 40),
        name="gla_gates",
    )(x, mods, mods, w_r, w2, b2)


def _gla_scan_kernel(*refs, hgrn, reverse, final, heads, dk, dv, n_chunks, qscale):
    if hgrn:
        q_ref, f_ref, v_ref, lb_ref = refs[:4]
        rest = refs[4:]
    else:
        q_ref, k_ref, v_ref, g_ref = refs[:4]
        rest = refs[4:]
    if final:
        og_ref, of_ref, gain_ref, o_ref, st_ref = rest
    else:
        o_ref, st_ref = rest

    @pl.when(pl.program_id(1) == 0)
    def _():
        st_ref[...] = jnp.zeros_like(st_ref)

    row = lax.broadcasted_iota(jnp.int32, (CHUNK, CHUNK), 0)
    col = lax.broadcasted_iota(jnp.int32, (CHUNK, CHUNK), 1)
    tri = (row <= col) if reverse else (row >= col)
    tri_b = jnp.where(tri, 1.0, 0.0).astype(BF16)

    def head(h, carry):
        kcols = pl.ds(pl.multiple_of(h * dk, dk), dk)
        vcols = pl.ds(pl.multiple_of(h * dv, dv), dv)
        order = range(n_chunks - 1, -1, -1) if reverse else range(n_chunks)
        for c in order:
            rows = pl.ds(c * CHUNK, CHUNK)
            qf = q_ref[rows, kcols].astype(F32)
            if hgrn:
                lb = lb_ref[:, kcols]
                fx = f_ref[rows, kcols].astype(F32)
                qf = qf * jax.nn.sigmoid(qf)
                kf = (1.0 - lb) * jax.nn.sigmoid(-fx)
                g = jnp.log(lb + (1.0 - lb) * jax.nn.sigmoid(fx))
            else:
                kf = k_ref[rows, kcols].astype(F32)
                g = g_ref[rows, kcols]
            v = v_ref[rows, vcols]
            g_hi = g.astype(BF16)
            g_lo = (g - g_hi.astype(F32)).astype(BF16)
            b = _dot(tri_b, g_hi) + _dot(tri_b, g_lo)
            qd = (qf * (qscale * jnp.exp(b))).astype(BF16)
            kd = (kf * jnp.exp(-b)).astype(BF16)
            att = jnp.where(tri, _dot_nt(qd, kd), 0.0).astype(BF16)
            st = st_ref[h]
            o = _dot(att, v) + _dot_nt(qd, st.astype(BF16))
            b_end = b[0:1] if reverse else b[CHUNK - 1:CHUNK]
            kdec = (kf * jnp.exp(b_end - b)).astype(BF16)
            st_ref[h] = st * jnp.exp(b_end) + _dot_tn(v, kdec)
            if final:
                o = o + of_ref[rows, vcols]
                o = o * lax.rsqrt(jnp.mean(o * o, axis=-1, keepdims=True) + RMS_EPS) * gain_ref[...]
                og = og_ref[rows, vcols].astype(F32)
                o = o * (og * jax.nn.sigmoid(og))
            o_ref[rows, vcols] = o.astype(o_ref.dtype)
        return carry

    lax.fori_loop(0, heads, head, 0)


def _gla_scan(rs, hgrn, reverse, qkv, col_blocks, extra, heads, dk, dv, qscale, final_args=None):
    tc = min(256, rs.n_ctx)
    ctx_blocks, lat_blocks = rs.n_ctx // tc, rs.n_lat // tc
    steps = ctx_blocks + lat_blocks
    kd_w, vd_w = heads * dk, heads * dv

    def row_block(b, s):
        if reverse:
            ctx = b * ctx_blocks + (ctx_blocks - 1 - s)
            lat = rs.ctx_rows // tc + b * lat_blocks + (steps - 1 - s)
        else:
            ctx = b * ctx_blocks + s
            lat = rs.ctx_rows // tc + b * lat_blocks + (s - ctx_blocks)
        return jnp.where(s < ctx_blocks, ctx, lat)

    def spec(width, cb):
        return pl.BlockSpec((tc, width), lambda b, s: (row_block(b, s), cb))

    in_specs = [spec(kd_w, qkv[0][1]), spec(kd_w, qkv[1][1]), spec(vd_w, qkv[2][1])]
    args = [qkv[0][0], qkv[1][0], qkv[2][0]]
    if hgrn:
        in_specs.append(pl.BlockSpec((1, kd_w), lambda b, s: (0, 0)))
    else:
        in_specs.append(spec(kd_w, col_blocks))
    args.append(extra)
    final = final_args is not None
    if final:
        z, gcb, o_fwd, gain = final_args
        in_specs += [spec(vd_w, gcb), spec(vd_w, 0), pl.BlockSpec((1, dv), lambda b, s: (0, 0))]
        args += [z, o_fwd, gain]
    return pl.pallas_call(
        functools.partial(_gla_scan_kernel, hgrn=hgrn, reverse=reverse, final=final, heads=heads,
                          dk=dk, dv=dv, n_chunks=tc // CHUNK, qscale=qscale),
        grid=(rs.batch, steps),
        in_specs=in_specs,
        out_specs=spec(vd_w, 0),
        out_shape=jax.ShapeDtypeStruct((rs.rows, vd_w), BF16 if final else F32),
        scratch_shapes=[pltpu.VMEM((heads, dv, dk), F32)],
        compiler_params=_params(("arbitrary", "arbitrary"), 40),
        name=("hgrn" if hgrn else "gla") + ("_bwd" if reverse else "_fwd"),
    )(*args)


def _lru_scan_kernel(*refs, reverse, final, n_blk, bw):
    if final:
        xc_ref, wg_ref, bg_ref, ls_ref, hf_ref, y_ref, o_ref, a_ref, u_ref, h_ref = refs
    else:
        xc_ref, wg_ref, bg_ref, ls_ref, o_ref, a_ref, u_ref, h_ref = refs
    tc, w = xc_ref.shape
    groups = tc // 8

    @pl.when(pl.program_id(1) == 0)
    def _():
        h_ref[...] = jnp.zeros_like(h_ref)

    for n in range(n_blk):
        cols = slice(n * bw, (n + 1) * bw)
        xb = xc_ref[:, cols]
        gr = jax.nn.sigmoid(_dot(xb, wg_ref[0, n]) + bg_ref[0:1, cols])
        gi = jax.nn.sigmoid(_dot(xb, wg_ref[1, n]) + bg_ref[1:2, cols])
        log_a = RG_C * gr * ls_ref[:, cols]
        a = jnp.exp(log_a)
        a_ref[:, cols] = a
        u_ref[:, cols] = jnp.sqrt(-jnp.tanh(log_a) * (a * a + 1.0)) * gi * xb.astype(F32)

    sub = lax.broadcasted_iota(jnp.int32, (8, w), 0)

    def group(i, h):
        g = (groups - 1 - i) if reverse else i
        rows = pl.ds(pl.multiple_of(g * 8, 8), 8)
        a, u = a_ref[rows, :], u_ref[rows, :]
        for k in (1, 2, 4):
            if reverse:
                ok = sub < 8 - k
                a_s, u_s = pltpu.roll(a, 8 - k, 0), pltpu.roll(u, 8 - k, 0)
            else:
                ok = sub >= k
                a_s, u_s = pltpu.roll(a, k, 0), pltpu.roll(u, k, 0)
            u = jnp.where(ok, a * u_s + u, u)
            a = jnp.where(ok, a * a_s, a)
        out = a * h + u
        last = out[0:1] if reverse else out[7:8]
        res = out
        if final:
            res = (out + hf_ref[rows, :]) * y_ref[rows, :].astype(F32)
        o_ref[rows, :] = res.astype(o_ref.dtype)
        return jnp.broadcast_to(last, (8, w))

    h_ref[...] = lax.fori_loop(0, groups, group, h_ref[...])


def _lru_scan(rs, reverse, xc, w_gate, b_gate, log_sig_lam, final_args=None):
    tc = min(256, rs.n_ctx)
    ctx_blocks, lat_blocks = rs.n_ctx // tc, rs.n_lat // tc
    steps = ctx_blocks + lat_blocks
    w = xc.shape[1]
    n_blk, bw = w_gate.shape[1], w_gate.shape[2]

    def row_block(b, s):
        if reverse:
            ctx = b * ctx_blocks + (ctx_blocks - 1 - s)
            lat = rs.ctx_rows // tc + b * lat_blocks + (steps - 1 - s)
        else:
            ctx = b * ctx_blocks + s
            lat = rs.ctx_rows // tc + b * lat_blocks + (s - ctx_blocks)
        return jnp.where(s < ctx_blocks, ctx, lat)

    stream = pl.BlockSpec((tc, w), lambda b, s: (row_block(b, s), 0))
    in_specs = [stream,
                pl.BlockSpec(w_gate.shape, lambda b, s: (0, 0, 0, 0)),
                pl.BlockSpec((2, w), lambda b, s: (0, 0)),
                pl.BlockSpec((1, w), lambda b, s: (0, 0))]
    args = [xc, w_gate, b_gate, log_sig_lam]
    final = final_args is not None
    if final:
        in_specs += [stream, stream]
        args += list(final_args)
    return pl.pallas_call(
        functools.partial(_lru_scan_kernel, reverse=reverse, final=final, n_blk=n_blk, bw=bw),
        grid=(rs.batch, steps),
        in_specs=in_specs,
        out_specs=stream,
        out_shape=jax.ShapeDtypeStruct((rs.rows, w), BF16 if final else F32),
        scratch_shapes=[pltpu.VMEM((tc, w), F32), pltpu.VMEM((tc, w), F32), pltpu.VMEM((8, w), F32)],
        compiler_params=_params(("arbitrary", "arbitrary"), 40),
        name="lru_bwd" if reverse else "lru_fwd",
    )(*args)


def _layer_norm(z, g, b):
    mu = jnp.mean(z, axis=-1, keepdims=True)
    zc = z - mu
    var = jnp.mean(zc * zc, axis=-1, keepdims=True)
    return zc * lax.rsqrt(var + LN_EPS) * g + b


def _out_proj_kernel(a_ref, w_ref, x_ref, gate_ref, sh_ref, sc_ref, lg_ref, lb_ref, wr_ref, br_ref,
                     xo_ref, h_ref, lt_ref, *, alpha):
    y = _dot(a_ref[...], w_ref[...])
    x_new = _layer_norm(alpha * x_ref[...] + gate_ref[0] * y, lg_ref[...], lb_ref[...])
    xo_ref[...] = x_new
    h = (x_new * (1.0 + sc_ref[0]) + sh_ref[0]).astype(BF16)
    h_ref[...] = h
    lt_ref[...] = _dot_nt(wr_ref[...], h) + br_ref[...]


def _out_proj(rs, a, w_out, x, mods, layer, ln_g, ln_b, w_router_t, b_router, alpha):
    rows, d = x.shape
    k = a.shape[1]
    n_e = w_router_t.shape[0]
    rt = rs.tile(256)
    row = lambda i: (i, 0)
    const = lambda i: (0, 0)
    return pl.pallas_call(
        functools.partial(_out_proj_kernel, alpha=alpha),
        grid=(rows // rt,),
        in_specs=[pl.BlockSpec((rt, k), row),
                  pl.BlockSpec((k, d), const),
                  pl.BlockSpec((rt, d), row),
                  pl.BlockSpec((1, 1, d), rs.mod_map(layer, 2, rt)),
                  pl.BlockSpec((1, 1, d), rs.mod_map(layer, 3, rt)),
                  pl.BlockSpec((1, 1, d), rs.mod_map(layer, 4, rt)),
                  pl.BlockSpec((1, d), const),
                  pl.BlockSpec((1, d), const),
                  pl.BlockSpec((n_e, d), const),
                  pl.BlockSpec((n_e, 1), const)],
        out_specs=[pl.BlockSpec((rt, d), row),
                   pl.BlockSpec((rt, d), row),
                   pl.BlockSpec((n_e, rt), lambda i: (0, i))],
        out_shape=[jax.ShapeDtypeStruct((rows, d), F32),
                   jax.ShapeDtypeStruct((rows, d), BF16),
                   jax.ShapeDtypeStruct((n_e, rows), F32)],
        compiler_params=_params(("arbitrary",), 48),
        name="out_proj_ln",
    )(a, w_out, x, mods, mods, mods, ln_g, ln_b, w_router_t, b_router)


def _route_kernel(lt_ref, e_ref, g_ref, r_ref, cnt_ref, carry_ref):
    n_e, tr = lt_ref.shape

    @pl.when(pl.program_id(0) == 0)
    def _():
        carry_ref[...] = jnp.zeros_like(carry_ref)

    lg = lt_ref[...]
    eid = lax.broadcasted_iota(jnp.int32, (n_e, tr), 0)
    vals, idxs, sels = [], [], []
    for _ in range(TOP_K):
        m = jnp.max(lg, axis=0, keepdims=True)
        idx = jnp.min(jnp.where(lg == m, eid, n_e), axis=0, keepdims=True)
        sel = eid == idx
        vals.append(m)
        idxs.append(idx)
        sels.append(sel)
        lg = jnp.where(sel, -jnp.inf, lg)
    ex = [jnp.exp(v - vals[0]) for v in vals]
    denom = ex[0] + ex[1] + ex[2] + ex[3]
    chosen = sels[0] | sels[1] | sels[2] | sels[3]
    member = jnp.where(chosen, 1.0, 0.0)
    srow = lax.broadcasted_iota(jnp.int32, (tr, tr), 0)
    scol = lax.broadcasted_iota(jnp.int32, (tr, tr), 1)
    before = jnp.where(srow < scol, 1.0, 0.0).astype(BF16)
    prefix = _dot(member.astype(BF16), before) + carry_ref[...]
    for k in range(TOP_K):
        e_ref[k:k + 1, :] = idxs[k]
        g_ref[k:k + 1, :] = ex[k] / denom
        r_ref[k:k + 1, :] = jnp.sum(jnp.where(sels[k], prefix, 0.0), axis=0, keepdims=True).astype(jnp.int32)
    carry_ref[...] = carry_ref[...] + jnp.sum(member, axis=1, keepdims=True)
    cnt_ref[...] = carry_ref[...]


def _route(logits_t):
    n_e, rows = logits_t.shape
    tr = _pick(rows, 512)
    tok = pl.BlockSpec((TOP_K, tr), lambda i: (0, i))
    return pl.pallas_call(
        _route_kernel,
        grid=(rows // tr,),
        in_specs=[pl.BlockSpec((n_e, tr), lambda i: (0, i))],
        out_specs=[tok, tok, tok, pl.BlockSpec((n_e, 1), lambda i: (0, 0))],
        out_shape=[jax.ShapeDtypeStruct((TOP_K, rows), jnp.int32),
                   jax.ShapeDtypeStruct((TOP_K, rows), F32),
                   jax.ShapeDtypeStruct((TOP_K, rows), jnp.int32),
                   jax.ShapeDtypeStruct((n_e, 1), F32)],
        scratch_shapes=[pltpu.VMEM((n_e, 1), F32)],
        compiler_params=_params(("arbitrary",), 32),
        name="route",
    )(logits_t)


def _expert_kernel(be_ref, nb_ref, x_ref, wgu_ref, bgu_ref, wd_ref, bd_ref, o_ref, *, d_ff):
    @pl.when(pl.program_id(0) < nb_ref[0])
    def _():
        gu = _dot(x_ref[...], wgu_ref[0]) + bgu_ref[0]
        g = jnp.minimum(gu[:, :d_ff], SWIGLU_LIMIT)
        u = jnp.clip(gu[:, d_ff:], -SWIGLU_LIMIT, SWIGLU_LIMIT)
        act = (g * jax.nn.sigmoid(SWIGLU_ALPHA * g) * (u + 1.0)).astype(BF16)
        o_ref[...] = (_dot(act, wd_ref[0]) + bd_ref[0]).astype(o_ref.dtype)

    @pl.when(pl.program_id(0) >= nb_ref[0])
    def _():
        o_ref[...] = jnp.zeros_like(o_ref)


def _experts(xg, block_e, n_used, w_gu, b_gu, w_down, b_down, bm):
    n_rows, d = xg.shape
    n_e, _, ff2 = w_gu.shape
    d_ff = ff2 // 2
    grid_spec = pltpu.PrefetchScalarGridSpec(
        num_scalar_prefetch=2,
        grid=(n_rows // bm,),
        in_specs=[pl.BlockSpec((bm, d), lambda i, be, nb: (i, 0)),
                  pl.BlockSpec((1, d, ff2), lambda i, be, nb: (be[i], 0, 0)),
                  pl.BlockSpec((1, 1, ff2), lambda i, be, nb: (be[i], 0, 0)),
                  pl.BlockSpec((1, d_ff, d), lambda i, be, nb: (be[i], 0, 0)),
                  pl.BlockSpec((1, 1, d), lambda i, be, nb: (be[i], 0, 0))],
        out_specs=pl.BlockSpec((bm, d), lambda i, be, nb: (i, 0)),
    )
    return pl.pallas_call(
        functools.partial(_expert_kernel, d_ff=d_ff),
        grid_spec=grid_spec,
        out_shape=jax.ShapeDtypeStruct((n_rows, d), BF16),
        compiler_params=_params(("arbitrary",), 52),
        name="experts",
    )(block_e, n_used, xg, w_gu, b_gu.reshape(n_e, 1, ff2), w_down, b_down.reshape(n_e, 1, d))


def _combine_kernel(y_ref, g_ref, x_ref, gate_ref, lg_ref, lb_ref, o_ref, *, alpha):
    g = g_ref[...]
    acc = g[:, 0:1] * y_ref[0].astype(F32)
    for k in range(1, TOP_K):
        acc = acc + g[:, k:k + 1] * y_ref[k].astype(F32)
    o_ref[...] = _layer_norm(alpha * x_ref[...] + gate_ref[0] * acc, lg_ref[...], lb_ref[...])


def _combine(rs, yg, gates, x, mods, layer, ln_g, ln_b, alpha):
    rows, d = x.shape
    rt = rs.tile(256)
    row = lambda i: (i, 0)
    const = lambda i: (0, 0)
    return pl.pallas_call(
        functools.partial(_combine_kernel, alpha=alpha),
        grid=(rows // rt,),
        in_specs=[pl.BlockSpec((TOP_K, rt, d), lambda i: (0, i, 0)),
                  pl.BlockSpec((rt, TOP_K), row),
                  pl.BlockSpec((rt, d), row),
                  pl.BlockSpec((1, 1, d), rs.mod_map(layer, 5, rt)),
                  pl.BlockSpec((1, d), const),
                  pl.BlockSpec((1, d), const)],
        out_specs=pl.BlockSpec((rt, d), row),
        out_shape=jax.ShapeDtypeStruct((rows, d), F32),
        compiler_params=_params(("arbitrary",), 40),
        name="combine_ln",
    )(yg, gates, x, mods, ln_g, ln_b)


def _moe(rs, h, logits_t, x, mods, layer, ln_g, ln_b, w_gu, b_gu, w_down, b_down, alpha):
    rows, d = h.shape
    n_e = w_gu.shape[0]
    bm = 512 if rows * TOP_K >= 512 * n_e * 4 else 128
    e_t, g_t, r_t, cnt = _route(logits_t)
    counts = cnt[:, 0].astype(jnp.int32)
    padded = (counts + bm - 1) // bm * bm
    pad_end = jnp.cumsum(padded)
    dest = (pad_end - padded)[e_t] + r_t
    n_blocks = -(-(rows * TOP_K + n_e * (bm - 1)) // bm)
    tok = jnp.broadcast_to(jnp.arange(rows, dtype=jnp.int32)[None, :], dest.shape)
    row_tok = jnp.zeros((n_blocks * bm,), jnp.int32).at[dest.reshape(-1)].set(tok.reshape(-1))
    block_e = jnp.minimum(
        jnp.searchsorted(pad_end, jnp.arange(n_blocks, dtype=jnp.int32) * bm, side="right"),
        n_e - 1).astype(jnp.int32)
    n_used = (pad_end[-1:] // bm).astype(jnp.int32)
    xg = jnp.take(h, row_tok, axis=0)
    y = _experts(xg, block_e, n_used, w_gu, b_gu, w_down, b_down, bm)
    yg = jnp.take(y, dest, axis=0)
    return _combine(rs, yg, g_t.T, x, mods, layer, ln_g, ln_b, alpha)


def kernel(x, c, ctx, c_ctx, ada_w, ada_b, ln_g, ln_b, sc_w_in, sc_conv, sc_w_out, gla_w_in, gla_w_gate2, gla_b_gate, gla_norm, gla_w_out, lru_w_in, lru_conv, lru_conv_b, lru_w_gate, lru_b_gate, lru_lambda, lru_w_out, hg_w_in, hg_lb_raw, hg_norm, hg_w_out, moe_w_router, moe_b_router, moe_w_gu, moe_b_gu, moe_w_down, moe_b_down):
    batch, n_lat, d = x.shape
    n_ctx = ctx.shape[1]
    depth = ada_w.shape[0]
    alpha = (2 * depth) ** 0.25
    assert GRID_W & (GRID_W - 1) == 0 and n_ctx & (n_ctx - 1) == 0 and batch < MOD_ROWS
    rs = _Rows(batch, n_ctx, n_lat)

    cc = jnp.zeros((MOD_ROWS, d), F32).at[:batch].set(c).at[batch].set(c_ctx)
    mods = _adaln(cc, ada_w, ada_b).reshape(depth * MOD_ROWS * N_MOD, 1, d)
    xs = jnp.concatenate([ctx.reshape(-1, d), x.reshape(-1, d)], axis=0)

    for i in range(depth):
        kind, j = i % 4, i // 4
        with_ctx = i < depth - 1
        if kind == 0:
            w_in = sc_w_in[j].astype(BF16)
            a = _in_proj("sconv", rs, xs, mods, i, [(w_in, 0), (w_in, 1), (w_in, 2)], d, [BF16],
                         extra=[sc_conv[j]])
            w_out = sc_w_out[j]
        elif kind == 1:
            dv = gla_norm.shape[-1]
            dk = dv // 2
            heads = gla_w_gate2.shape[-1] // dk
            kd_w, vd_w = heads * dk, heads * dv
            n_main = 2 * kd_w + 2 * vd_w
            rank = gla_w_gate2.shape[2]
            w_in = gla_w_in[j]
            z = _in_proj("plain", rs, xs, mods, i, [(w_in[:, :n_main].astype(BF16), 0)], n_main, [BF16],
                         tn_pref=512)
            w2 = jnp.zeros((2 * rank, 2 * kd_w), F32)
            w2 = w2.at[:rank, :kd_w].set(gla_w_gate2[j, 0]).at[rank:, kd_w:].set(gla_w_gate2[j, 1])
            gates = _gla_gates(rs, xs, mods, i, w_in[:, n_main:].astype(BF16), w2.astype(BF16),
                               gla_b_gate[j].reshape(1, 2 * kd_w))
            qkv = [(z, 0), (z, 1), (z, kd_w * 2 // vd_w)]
            scale = dk ** -0.5
            o_f = _gla_scan(rs, False, False, qkv, 0, gates, heads, dk, dv, scale)
            a = _gla_scan(rs, False, True, qkv, 1, gates, heads, dk, dv, scale,
                          final_args=(z, (2 * kd_w + vd_w) // vd_w, o_f, gla_norm[j].reshape(1, dv)))
            w_out = gla_w_out[j]
        elif kind == 2:
            w = lru_w_in.shape[-1] // 2
            w_in = lru_w_in[j].astype(BF16)
            y, xc = _in_proj("lru", rs, xs, mods, i, [(w_in, 0), (w_in, 1)], w, [BF16, BF16],
                             extra=[lru_conv[j], lru_conv_b[j].reshape(1, w)])
            log_sig = _log_sigmoid(lru_lambda[j])
            wg, bgate = lru_w_gate[j].astype(BF16), lru_b_gate[j]
            h_f = _lru_scan(rs, False, xc, wg[0], bgate[0], log_sig[0:1])
            a = _lru_scan(rs, True, xc, wg[1], bgate[1], log_sig[1:2], final_args=(h_f, y))
            w_out = lru_w_out[j]
        else:
            dk = hg_norm.shape[-1]
            heads = d // dk
            z = _in_proj("plain", rs, xs, mods, i, [(hg_w_in[j].astype(BF16), 0)], 5 * d, [BF16], tn_pref=512)
            p = jax.nn.softmax(hg_lb_raw, axis=0)
            lower = (jnp.cumsum(p, axis=0)[i] - p[0]).reshape(1, d)
            scale = dk ** -0.5
            o_f = _gla_scan(rs, True, False, [(z, 0), (z, 1), (z, 3)], 0, lower, heads, dk, dk, scale)
            a = _gla_scan(rs, True, True, [(z, 0), (z, 2), (z, 3)], 0, lower, heads, dk, dk, scale,
                          final_args=(z, 4, o_f, hg_norm[j].reshape(1, dk)))
            w_out = hg_w_out[j]

        if not with_ctx:
            a, xs = a[rs.ctx_rows:], xs[rs.ctx_rows:]
            rs = _Rows(batch, 0, n_lat)
        xs, h, logits_t = _out_proj(rs, a, w_out.astype(BF16), xs, mods, i,
                                    ln_g[i, 0].reshape(1, d), ln_b[i, 0].reshape(1, d),
                                    moe_w_router[i].T.astype(BF16), moe_b_router[i].reshape(-1, 1), alpha)
        xs = _moe(rs, h, logits_t, xs, mods, i, ln_g[i, 1].reshape(1, d), ln_b[i, 1].reshape(1, d),
                  moe_w_gu[i].astype(BF16), moe_b_gu[i], moe_w_down[i].astype(BF16), moe_b_down[i], alpha)
    return xs[rs.ctx_rows:].reshape(batch, n_lat, d)
```

```python
import functools

import jax
import jax.numpy as jnp
from jax import lax
from jax.experimental import pallas as pl
from jax.experimental.pallas import tpu as pltpu

F32 = jnp.float32
BF16 = jnp.bfloat16

GRID_W = 64
CHUNK = 64
TOP_K = 4
N_MOD = 6
MOD_ROWS = 8
GLA_NORMALIZER = 16.0
RG_C = 8.0
SWIGLU_LIMIT = 7.0
SWIGLU_ALPHA = 1.702
LN_EPS = 1e-5
RMS_EPS = 1e-6
MIB = 1024 * 1024


def _params(sem, vmem_mib):
    return pltpu.CompilerParams(dimension_semantics=sem, vmem_limit_bytes=vmem_mib * MIB)


def _dot(a, b):
    return jnp.dot(a, b, preferred_element_type=F32)


def _dot_nt(a, b):
    return lax.dot_general(a, b, (((1,), (1,)), ((), ())), preferred_element_type=F32)


def _dot_tn(a, b):
    return lax.dot_general(a, b, (((0,), (0,)), ((), ())), preferred_element_type=F32)


def _log_sigmoid(x):
    return jnp.minimum(x, 0.0) - jnp.log1p(jnp.exp(-jnp.abs(x)))


def _pick(n, pref):
    t = min(n, pref)
    while n % t:
        t //= 2
    return t


class _Rows:
    def __init__(self, batch, n_ctx, n_lat):
        self.batch, self.n_ctx, self.n_lat = batch, n_ctx, n_lat
        self.ctx_rows = batch * n_ctx
        self.rows = self.ctx_rows + batch * n_lat

    def tile(self, pref):
        t = min(pref, self.n_lat)
        while self.n_lat % t or self.ctx_rows % t:
            t //= 2
        return t

    def mod_map(self, layer, which, rt):
        def index(i, *_):
            r0 = i * rt
            row = jnp.where(r0 < self.ctx_rows, self.batch, (r0 - self.ctx_rows) // self.n_lat)
            return ((layer * MOD_ROWS + row) * N_MOD + which, 0, 0)
        return index


def _adaln_kernel(c_ref, w_ref, b_ref, o_ref):
    c = c_ref[...]
    a = (c * jax.nn.sigmoid(c)).astype(BF16)
    o_ref[0] = _dot(a, w_ref[0].astype(BF16)) + b_ref[0]


def _adaln(cc, ada_w, ada_b):
    depth, d, n = ada_w.shape
    tn = _pick(n, 1024)
    return pl.pallas_call(
        _adaln_kernel,
        grid=(depth, n // tn),
        in_specs=[pl.BlockSpec((MOD_ROWS, d), lambda l, j: (0, 0)),
                  pl.BlockSpec((1, d, tn), lambda l, j: (l, 0, j)),
                  pl.BlockSpec((1, 1, tn), lambda l, j: (l, 0, j))],
        out_specs=pl.BlockSpec((1, MOD_ROWS, tn), lambda l, j: (l, 0, j)),
        out_shape=jax.ShapeDtypeStruct((depth, MOD_ROWS, n), F32),
        compiler_params=_params(("arbitrary", "arbitrary"), 40),
        name="adaln",
    )(cc, ada_w, ada_b.reshape(depth, 1, n))


def _modulate(x_ref, sh_ref, sc_ref, h_ref):
    @pl.when(pl.program_id(1) == 0)
    def _():
        h_ref[...] = (x_ref[...] * (1.0 + sc_ref[0]) + sh_ref[0]).astype(BF16)


def _segment_pos(shape, ctx_tiles, seg_ctx, seg_lat):
    mask = jnp.where(pl.program_id(0) < ctx_tiles, seg_ctx - 1, seg_lat - 1)
    return lax.broadcasted_iota(jnp.int32, shape, 0) & mask, mask


def _shift_rows(p, pos, seg_last, offset):
    rt = p.shape[0]
    rolled = pltpu.roll(p, (-offset) % rt, 0)
    ok = (pos + offset >= 0) & (pos + offset <= seg_last)
    return jnp.where(ok, rolled, 0.0)


def _proj_kernel(x_ref, sh_ref, sc_ref, w_ref, o_ref, h_ref):
    _modulate(x_ref, sh_ref, sc_ref, h_ref)
    o_ref[...] = _dot(h_ref[...], w_ref[...]).astype(o_ref.dtype)


def _sconv_in_kernel(x_ref, sh_ref, sc_ref, wb_ref, wc_ref, wv_ref, cw_ref, o_ref, h_ref,
                     *, ctx_tiles, seg_ctx, seg_lat):
    _modulate(x_ref, sh_ref, sc_ref, h_ref)
    h = h_ref[...]
    bg = _dot(h, wb_ref[...])
    p = _dot(h, wc_ref[...]) * _dot(h, wv_ref[...])
    pos, last = _segment_pos(p.shape, ctx_tiles, seg_ctx, seg_lat)
    cw = cw_ref[...]
    conv = (cw[0:1] * _shift_rows(p, pos, last, -1) + cw[1:2] * p
            + cw[2:3] * _shift_rows(p, pos, last, 1))
    o_ref[...] = (bg * conv).astype(o_ref.dtype)


def _lru_in_kernel(x_ref, sh_ref, sc_ref, wy_ref, wx_ref, cw_ref, cb_ref, y_ref, xc_ref, h_ref,
                   *, ctx_tiles, seg_ctx, seg_lat):
    _modulate(x_ref, sh_ref, sc_ref, h_ref)
    h = h_ref[...]
    y_ref[...] = jax.nn.gelu(_dot(h, wy_ref[...])).astype(y_ref.dtype)
    xb = _dot(h, wx_ref[...])
    pos, last = _segment_pos(xb.shape, ctx_tiles, seg_ctx, seg_lat)
    cw = cw_ref[...]
    conv = (cw[0:1] * _shift_rows(xb, pos, last, -1) + cw[1:2] * xb
            + cw[2:3] * _shift_rows(xb, pos, last, 1) + cw[3:4] * _shift_rows(xb, pos, last, 2))
    xc_ref[...] = (conv + cb_ref[...]).astype(xc_ref.dtype)


def _in_proj(kind, rs, x, mods, layer, weights, n_out, out_dtypes, extra=(), tn_pref=256, rt_pref=1024):
    rows, d = x.shape
    rt = rs.tile(rt_pref)
    tn = _pick(n_out, tn_pref)
    nj = n_out // tn
    in_specs = [pl.BlockSpec((rt, d), lambda i, j: (i, 0)),
                pl.BlockSpec((1, 1, d), rs.mod_map(layer, 0, rt)),
                pl.BlockSpec((1, 1, d), rs.mod_map(layer, 1, rt))]
    args = [x, mods, mods]
    for w, off in weights:
        in_specs.append(pl.BlockSpec((d, tn), functools.partial(lambda i, j, o: (0, o * nj + j), o=off)))
        args.append(w)
    for e in extra:
        in_specs.append(pl.BlockSpec((e.shape[0], tn), lambda i, j: (0, j)))
        args.append(e)
    out_specs = [pl.BlockSpec((rt, tn), lambda i, j: (i, j)) for _ in out_dtypes]
    out_shape = [jax.ShapeDtypeStruct((rows, n_out), dt) for dt in out_dtypes]
    ctx_tiles = rs.ctx_rows // rt
    if kind == "plain":
        body = _proj_kernel
    elif kind == "sconv":
        body = functools.partial(_sconv_in_kernel, ctx_tiles=ctx_tiles, seg_ctx=rs.n_ctx, seg_lat=GRID_W)
    else:
        body = functools.partial(_lru_in_kernel, ctx_tiles=ctx_tiles, seg_ctx=rs.n_ctx, seg_lat=GRID_W)
    single = len(out_dtypes) == 1
    return pl.pallas_call(
        body,
        grid=(rows // rt, nj),
        in_specs=in_specs,
        out_specs=out_specs[0] if single else out_specs,
        out_shape=out_shape[0] if single else out_shape,
        scratch_shapes=[pltpu.VMEM((rt, d), BF16)],
        compiler_params=_params(("arbitrary", "arbitrary"), 52),
        name=kind + "_in_proj",
    )(*args)


def _gla_gate_kernel(x_ref, sh_ref, sc_ref, wr_ref, w2_ref, b2_ref, o_ref):
    h = (x_ref[...] * (1.0 + sc_ref[0]) + sh_ref[0]).astype(BF16)
    r = _dot(h, wr_ref[...]).astype(BF16)
    pre = _dot(r, w2_ref[...]) + b2_ref[...]
    o_ref[...] = _log_sigmoid(pre) * (1.0 / GLA_NORMALIZER)


def _gla_gates(rs, x, mods, layer, w_r, w2, b2):
    rows, d = x.shape
    rt = rs.tile(256)
    n = w2.shape[1]
    return pl.pallas_call(
        _gla_gate_kernel,
        grid=(rows // rt,),
        in_specs=[pl.BlockSpec((rt, d), lambda i: (i, 0)),
                  pl.BlockSpec((1, 1, d), rs.mod_map(layer, 0, rt)),
                  pl.BlockSpec((1, 1, d), rs.mod_map(layer, 1, rt)),
                  pl.BlockSpec(w_r.shape, lambda i: (0, 0)),
                  pl.BlockSpec(w2.shape, lambda i: (0, 0)),
                  pl.BlockSpec((1, n), lambda i: (0, 0))],
        out_specs=pl.BlockSpec((rt, n), lambda i: (i, 0)),
        out_shape=jax.ShapeDtypeStruct((rows, n), F32),
        compiler_params=_params(("arbitrary",), 40),
        name="gla_gates",
    )(x, mods, mods, w_r, w2, b2)


def _gla_scan_kernel(*refs, hgrn, reverse, final, heads, group, dk, dv, n_chunks, qscale):
    if hgrn:
        q_ref, f_ref, v_ref, lb_ref = refs[:4]
        rest = refs[4:]
    else:
        q_ref, k_ref, v_ref, g_ref = refs[:4]
        rest = refs[4:]
    if final:
        og_ref, of_ref, gain_ref, o_ref, st_ref = rest
    else:
        o_ref, st_ref = rest

    @pl.when(pl.program_id(1) == 0)
    def _():
        st_ref[...] = jnp.zeros_like(st_ref)

    tc = n_chunks * CHUNK
    row = lax.broadcasted_iota(jnp.int32, (tc, tc), 0)
    col = lax.broadcasted_iota(jnp.int32, (tc, tc), 1)
    shift = CHUNK.bit_length() - 1
    same_chunk = (row >> shift) == (col >> shift)
    tri = same_chunk & ((row <= col) if reverse else (row >= col))
    tri_b = jnp.where(tri, 1.0, 0.0).astype(BF16)
    order = range(n_chunks - 1, -1, -1) if reverse else range(n_chunks)

    def intra(h):
        kcols = pl.ds(pl.multiple_of(h * dk, dk), dk)
        vcols = pl.ds(pl.multiple_of(h * dv, dv), dv)
        qf = q_ref[:, kcols].astype(F32)
        if hgrn:
            lb = lb_ref[:, kcols]
            s = jax.nn.sigmoid(f_ref[:, kcols].astype(F32))
            qf = qf * jax.nn.sigmoid(qf)
            kf = (1.0 - lb) * (1.0 - s)
            g = jnp.log(lb + (1.0 - lb) * s)
        else:
            kf = k_ref[:, kcols].astype(F32)
            g = g_ref[:, kcols]
        v = v_ref[:, vcols]
        g_hi = g.astype(BF16)
        g_lo = (g - g_hi.astype(F32)).astype(BF16)
        b = _dot(tri_b, g_hi) + _dot(tri_b, g_lo)
        qd = (qf * (qscale * jnp.exp(b))).astype(BF16)
        kdf = kf * jnp.exp(-b)
        att = jnp.where(tri, _dot_nt(qd, kdf.astype(BF16)), 0.0).astype(BF16)
        return vcols, b, qd, kdf, v, _dot(att, v)

    def finish(vcols, outs):
        o = jnp.concatenate(outs, axis=0)
        if final:
            o = o + of_ref[:, vcols]
            o = o * lax.rsqrt(jnp.mean(o * o, axis=-1, keepdims=True) + RMS_EPS) * gain_ref[...]
            og = og_ref[:, vcols].astype(F32)
            o = o * (og * jax.nn.sigmoid(og))
        o_ref[:, vcols] = o.astype(o_ref.dtype)

    def head_group(i, carry):
        hs = [i * group + j for j in range(group)]
        parts = [intra(h) for h in hs]
        states = [st_ref[h] for h in hs]
        outs = [[None] * n_chunks for _ in hs]
        for c in order:
            rows = slice(c * CHUNK, (c + 1) * CHUNK)
            end = c * CHUNK if reverse else (c + 1) * CHUNK - 1
            for j, (_, b, qd, kdf, v, o_intra) in enumerate(parts):
                st = states[j]
                outs[j][c] = o_intra[rows] + _dot_nt(qd[rows], st.astype(BF16))
                dec = jnp.exp(b[end:end + 1])
                states[j] = st * dec + _dot_tn(v[rows], (kdf[rows] * dec).astype(BF16))
        for j, h in enumerate(hs):
            st_ref[h] = states[j]
            finish(parts[j][0], outs[j])
        return carry

    lax.fori_loop(0, heads // group, head_group, 0)


def _gla_scan(rs, hgrn, reverse, qkv, col_blocks, extra, heads, dk, dv, qscale, final_args=None):
    tc = min(256, rs.n_ctx)
    ctx_blocks, lat_blocks = rs.n_ctx // tc, rs.n_lat // tc
    steps = ctx_blocks + lat_blocks
    kd_w, vd_w = heads * dk, heads * dv

    def row_block(b, s):
        if reverse:
            ctx = b * ctx_blocks + (ctx_blocks - 1 - s)
            lat = rs.ctx_rows // tc + b * lat_blocks + (steps - 1 - s)
        else:
            ctx = b * ctx_blocks + s
            lat = rs.ctx_rows // tc + b * lat_blocks + (s - ctx_blocks)
        return jnp.where(s < ctx_blocks, ctx, lat)

    def spec(width, cb):
        return pl.BlockSpec((tc, width), lambda b, s: (row_block(b, s), cb))

    in_specs = [spec(kd_w, qkv[0][1]), spec(kd_w, qkv[1][1]), spec(vd_w, qkv[2][1])]
    args = [qkv[0][0], qkv[1][0], qkv[2][0]]
    if hgrn:
        in_specs.append(pl.BlockSpec((1, kd_w), lambda b, s: (0, 0)))
    else:
        in_specs.append(spec(kd_w, col_blocks))
    args.append(extra)
    final = final_args is not None
    if final:
        z, gcb, o_fwd, gain = final_args
        in_specs += [spec(vd_w, gcb), spec(vd_w, 0), pl.BlockSpec((1, dv), lambda b, s: (0, 0))]
        args += [z, o_fwd, gain]
    return pl.pallas_call(
        functools.partial(_gla_scan_kernel, hgrn=hgrn, reverse=reverse, final=final, heads=heads,
                          group=4 if heads % 4 == 0 and heads > 4 else 2 - heads % 2,
                          dk=dk, dv=dv, n_chunks=tc // CHUNK, qscale=qscale),
        grid=(rs.batch, steps),
        in_specs=in_specs,
        out_specs=spec(vd_w, 0),
        out_shape=jax.ShapeDtypeStruct((rs.rows, vd_w), BF16 if final else F32),
        scratch_shapes=[pltpu.VMEM((heads, dv, dk), F32)],
        compiler_params=_params(("arbitrary", "arbitrary"), 40),
        name=("hgrn" if hgrn else "gla") + ("_bwd" if reverse else "_fwd"),
    )(*args)


def _lru_scan_kernel(*refs, reverse, final, n_blk, bw):
    if final:
        xc_ref, wg_ref, bg_ref, ls_ref, hf_ref, y_ref, o_ref, a_ref, u_ref, h_ref = refs
    else:
        xc_ref, wg_ref, bg_ref, ls_ref, o_ref, a_ref, u_ref, h_ref = refs
    tc, w = xc_ref.shape
    groups = tc // 8

    @pl.when(pl.program_id(1) == 0)
    def _():
        h_ref[...] = jnp.zeros_like(h_ref)

    for n in range(n_blk):
        cols = slice(n * bw, (n + 1) * bw)
        xb = xc_ref[:, cols]
        gr = jax.nn.sigmoid(_dot(xb, wg_ref[0, n]) + bg_ref[0:1, cols])
        gi = jax.nn.sigmoid(_dot(xb, wg_ref[1, n]) + bg_ref[1:2, cols])
        log_a = RG_C * gr * ls_ref[:, cols]
        a = jnp.exp(log_a)
        a_ref[:, cols] = a
        u_ref[:, cols] = jnp.sqrt(-jnp.tanh(log_a) * (a * a + 1.0)) * gi * xb.astype(F32)

    sub = lax.broadcasted_iota(jnp.int32, (8, w), 0)

    def group(i, h):
        g = (groups - 1 - i) if reverse else i
        rows = pl.ds(pl.multiple_of(g * 8, 8), 8)
        a, u = a_ref[rows, :], u_ref[rows, :]
        for k in (1, 2, 4):
            if reverse:
                ok = sub < 8 - k
                a_s, u_s = pltpu.roll(a, 8 - k, 0), pltpu.roll(u, 8 - k, 0)
            else:
                ok = sub >= k
                a_s, u_s = pltpu.roll(a, k, 0), pltpu.roll(u, k, 0)
            u = jnp.where(ok, a * u_s + u, u)
            a = jnp.where(ok, a * a_s, a)
        out = a * h + u
        last = out[0:1] if reverse else out[7:8]
        res = out
        if final:
            res = (out + hf_ref[rows, :]) * y_ref[rows, :].astype(F32)
        o_ref[rows, :] = res.astype(o_ref.dtype)
        return jnp.broadcast_to(last, (8, w))

    h_ref[...] = lax.fori_loop(0, groups, group, h_ref[...])


def _lru_scan(rs, reverse, xc, w_gate, b_gate, log_sig_lam, final_args=None):
    tc = min(256, rs.n_ctx)
    ctx_blocks, lat_blocks = rs.n_ctx // tc, rs.n_lat // tc
    steps = ctx_blocks + lat_blocks
    w = xc.shape[1]
    n_blk, bw = w_gate.shape[1], w_gate.shape[2]

    def row_block(b, s):
        if reverse:
            ctx = b * ctx_blocks + (ctx_blocks - 1 - s)
            lat = rs.ctx_rows // tc + b * lat_blocks + (steps - 1 - s)
        else:
            ctx = b * ctx_blocks + s
            lat = rs.ctx_rows // tc + b * lat_blocks + (s - ctx_blocks)
        return jnp.where(s < ctx_blocks, ctx, lat)

    stream = pl.BlockSpec((tc, w), lambda b, s: (row_block(b, s), 0))
    in_specs = [stream,
                pl.BlockSpec(w_gate.shape, lambda b, s: (0, 0, 0, 0)),
                pl.BlockSpec((2, w), lambda b, s: (0, 0)),
                pl.BlockSpec((1, w), lambda b, s: (0, 0))]
    args = [xc, w_gate, b_gate, log_sig_lam]
    final = final_args is not None
    if final:
        in_specs += [stream, stream]
        args += list(final_args)
    return pl.pallas_call(
        functools.partial(_lru_scan_kernel, reverse=reverse, final=final, n_blk=n_blk, bw=bw),
        grid=(rs.batch, steps),
        in_specs=in_specs,
        out_specs=stream,
        out_shape=jax.ShapeDtypeStruct((rs.rows, w), BF16 if final else F32),
        scratch_shapes=[pltpu.VMEM((tc, w), F32), pltpu.VMEM((tc, w), F32), pltpu.VMEM((8, w), F32)],
        compiler_params=_params(("arbitrary", "arbitrary"), 40),
        name="lru_bwd" if reverse else "lru_fwd",
    )(*args)


def _layer_norm(z, g, b):
    mu = jnp.mean(z, axis=-1, keepdims=True)
    zc = z - mu
    var = jnp.mean(zc * zc, axis=-1, keepdims=True)
    return zc * lax.rsqrt(var + LN_EPS) * g + b


def _pack_words(v):
    half = v.shape[1] // 2
    vb = v.astype(BF16).astype(F32)
    lo = lax.bitcast_convert_type(vb[:, :half], jnp.uint32) >> 16
    hi = lax.bitcast_convert_type(vb[:, half:], jnp.uint32) & jnp.uint32(0xFFFF0000)
    return hi | lo


def _store_packed(ref, row0, n, words):
    s_rows = words.shape[1] // 128
    for s in range(s_rows):
        ref[pl.ds(row0 * s_rows + s, n, stride=s_rows), :] = words[:, s * 128:(s + 1) * 128]


def _load_packed(ref, row0, n, s_rows):
    lo, hi = [], []
    for s in range(s_rows):
        w = ref[pl.ds(row0 * s_rows + s, n, stride=s_rows), :]
        lo.append(lax.bitcast_convert_type(w << 16, F32))
        hi.append(lax.bitcast_convert_type(w & jnp.uint32(0xFFFF0000), F32))
    return lo, hi


def _out_proj_kernel(a_ref, w_ref, x_ref, gate_ref, sh_ref, sc_ref, lg_ref, lb_ref, wr_ref, br_ref,
                     xo_ref, hp_ref, lt_ref, *, alpha):
    y = _dot(a_ref[...], w_ref[...])
    x_new = _layer_norm(alpha * x_ref[...] + gate_ref[0] * y, lg_ref[...], lb_ref[...])
    xo_ref[...] = x_new
    h = x_new * (1.0 + sc_ref[0]) + sh_ref[0]
    _store_packed(hp_ref, 0, h.shape[0], _pack_words(h))
    lt_ref[...] = _dot_nt(wr_ref[...], h.astype(BF16)) + br_ref[...]


def _out_proj(rs, a, w_out, x, mods, layer, ln_g, ln_b, w_router_t, b_router, alpha):
    rows, d = x.shape
    k = a.shape[1]
    n_e = w_router_t.shape[0]
    rt = rs.tile(256)
    row = lambda i: (i, 0)
    const = lambda i: (0, 0)
    return pl.pallas_call(
        functools.partial(_out_proj_kernel, alpha=alpha),
        grid=(rows // rt,),
        in_specs=[pl.BlockSpec((rt, k), row),
                  pl.BlockSpec((k, d), const),
                  pl.BlockSpec((rt, d), row),
                  pl.BlockSpec((1, 1, d), rs.mod_map(layer, 2, rt)),
                  pl.BlockSpec((1, 1, d), rs.mod_map(layer, 3, rt)),
                  pl.BlockSpec((1, 1, d), rs.mod_map(layer, 4, rt)),
                  pl.BlockSpec((1, d), const),
                  pl.BlockSpec((1, d), const),
                  pl.BlockSpec((n_e, d), const),
                  pl.BlockSpec((n_e, 1), const)],
        out_specs=[pl.BlockSpec((rt, d), row),
                   pl.BlockSpec((rt * (d // 256), 128), row),
                   pl.BlockSpec((n_e, rt), lambda i: (0, i))],
        out_shape=[jax.ShapeDtypeStruct((rows, d), F32),
                   jax.ShapeDtypeStruct((rows * (d // 256), 128), jnp.uint32),
                   jax.ShapeDtypeStruct((n_e, rows), F32)],
        compiler_params=_params(("arbitrary",), 48),
        name="out_proj_ln",
    )(a, w_out, x, mods, mods, mods, ln_g, ln_b, w_router_t, b_router)


def _route_kernel(lt_ref, e_ref, g_ref, r_ref, cnt_ref, carry_ref):
    n_e, tr = lt_ref.shape

    @pl.when(pl.program_id(0) == 0)
    def _():
        carry_ref[...] = jnp.zeros_like(carry_ref)

    lg = lt_ref[...]
    eid = lax.broadcasted_iota(jnp.int32, (n_e, tr), 0)
    vals, idxs, sels = [], [], []
    for _ in range(TOP_K):
        m = jnp.max(lg, axis=0, keepdims=True)
        idx = jnp.min(jnp.where(lg == m, eid, n_e), axis=0, keepdims=True)
        sel = eid == idx
        vals.append(m)
        idxs.append(idx)
        sels.append(sel)
        lg = jnp.where(sel, -jnp.inf, lg)
    ex = [jnp.exp(v - vals[0]) for v in vals]
    denom = ex[0] + ex[1] + ex[2] + ex[3]
    chosen = sels[0] | sels[1] | sels[2] | sels[3]
    member = jnp.where(chosen, 1.0, 0.0)
    srow = lax.broadcasted_iota(jnp.int32, (tr, tr), 0)
    scol = lax.broadcasted_iota(jnp.int32, (tr, tr), 1)
    before = jnp.where(srow < scol, 1.0, 0.0).astype(BF16)
    prefix = _dot(member.astype(BF16), before) + carry_ref[...]
    for k in range(TOP_K):
        e_ref[k:k + 1, :] = idxs[k]
        g_ref[k:k + 1, :] = ex[k] / denom
        r_ref[k:k + 1, :] = jnp.sum(jnp.where(sels[k], prefix, 0.0), axis=0, keepdims=True).astype(jnp.int32)
    carry_ref[...] = carry_ref[...] + jnp.sum(member, axis=1, keepdims=True)
    cnt_ref[...] = carry_ref[...]


def _route(logits_t):
    n_e, rows = logits_t.shape
    tr = _pick(rows, 512)
    tok = pl.BlockSpec((TOP_K, tr), lambda i: (0, i))
    return pl.pallas_call(
        _route_kernel,
        grid=(rows // tr,),
        in_specs=[pl.BlockSpec((n_e, tr), lambda i: (0, i))],
        out_specs=[tok, tok, tok, pl.BlockSpec((n_e, 1), lambda i: (0, 0))],
        out_shape=[jax.ShapeDtypeStruct((TOP_K, rows), jnp.int32),
                   jax.ShapeDtypeStruct((TOP_K, rows), F32),
                   jax.ShapeDtypeStruct((TOP_K, rows), jnp.int32),
                   jax.ShapeDtypeStruct((n_e, 1), F32)],
        scratch_shapes=[pltpu.VMEM((n_e, 1), F32)],
        compiler_params=_params(("arbitrary",), 32),
        name="route",
    )(logits_t)


def _row_slab(ref, row, s_rows):
    return ref.at[pl.ds(pl.multiple_of(row * s_rows, s_rows), s_rows), :]


def _dispatch_kernel(pe_ref, pd_ref, dest_ref, hp_ref, xg_ref, idx_ref, zero_ref, sem, idx_sem,
                     *, rt, bm, s_rows, n_e, n_blocks):
    @pl.when(pl.program_id(0) == 0)
    def _():
        zero_ref[...] = jnp.zeros_like(zero_ref)
        def fill(last_row):
            first = pl.multiple_of((last_row - bm) * s_rows, s_rows)
            return pltpu.make_async_copy(zero_ref, xg_ref.at[pl.ds(first, bm * s_rows), :], sem)

        def start(e, c):
            @pl.when(pd_ref[e] > 0)
            def _():
                fill(pe_ref[e]).start()
            return c

        def wait(e, c):
            @pl.when(pd_ref[e] > 0)
            def _():
                fill(pe_ref[e]).wait()
            return c

        lax.fori_loop(0, n_e, start, 0)
        lax.fori_loop(0, n_e, wait, 0)
        n_used = pe_ref[n_e - 1] // bm
        lax.fori_loop(n_used, n_blocks, lambda j, c: (fill((j + 1) * bm).start(), c)[1], 0)
        lax.fori_loop(n_used, n_blocks, lambda j, c: (fill((j + 1) * bm).wait(), c)[1], 0)

    load = pltpu.make_async_copy(dest_ref, idx_ref, idx_sem)
    load.start()
    load.wait()

    def token(t, c):
        src = _row_slab(hp_ref, t, s_rows)
        for k in range(TOP_K):
            pltpu.make_async_copy(src, _row_slab(xg_ref, idx_ref[k, t], s_rows), sem).start()
        return c

    lax.fori_loop(0, rt, token, 0)
    for _ in range(TOP_K):
        pltpu.make_async_copy(hp_ref, xg_ref.at[pl.ds(0, rt * s_rows), :], sem).wait()


def _dispatch(hp, dest, pad_end, padded, n_rows, bm, rt):
    s_rows = hp.shape[0] // dest.shape[1]
    rows = dest.shape[1]
    n_e = pad_end.shape[0]
    grid_spec = pltpu.PrefetchScalarGridSpec(
        num_scalar_prefetch=2,
        grid=(rows // rt,),
        in_specs=[pl.BlockSpec((TOP_K, rt), lambda i, pe, pd: (0, i)),
                  pl.BlockSpec((rt * s_rows, 128), lambda i, pe, pd: (i, 0))],
        out_specs=pl.BlockSpec(memory_space=pl.ANY),
        scratch_shapes=[pltpu.SMEM((TOP_K, rt), jnp.int32),
                        pltpu.VMEM((bm * s_rows, 128), jnp.uint32),
                        pltpu.SemaphoreType.DMA,
                        pltpu.SemaphoreType.DMA],
    )
    return pl.pallas_call(
        functools.partial(_dispatch_kernel, rt=rt, bm=bm, s_rows=s_rows, n_e=n_e, n_blocks=n_rows // bm),
        grid_spec=grid_spec,
        out_shape=jax.ShapeDtypeStruct((n_rows * s_rows, 128), jnp.uint32),
        compiler_params=_params(("arbitrary",), 32),
        name="dispatch",
    )(pad_end, padded, dest, hp)


def _expert_kernel(be_ref, nb_ref, x_ref, wgu_ref, bgu_ref, wd_ref, bd_ref, o_ref, wgu_s, wd_s, xb_s,
                   *, d_ff, bm, s_rows):
    i = pl.program_id(0)

    @pl.when(i < nb_ref[0])
    def _():
        @pl.when((i == 0) | (be_ref[i] != be_ref[jnp.maximum(i - 1, 0)]))
        def _():
            def cast(dst, src, step):
                def body(c, carry):
                    r = pl.ds(pl.multiple_of(c * step, step), step)
                    dst[r, :] = src[0, r, :].astype(BF16)
                    return carry
                lax.fori_loop(0, dst.shape[0] // step, body, 0)
            cast(wgu_s, wgu_ref, _pick(wgu_s.shape[0], 128))
            cast(wd_s, wd_ref, _pick(wd_s.shape[0], 128))

        half = s_rows * 128
        lo, hi = _load_packed(x_ref, 0, bm, s_rows)
        for s in range(s_rows):
            xb_s[:, s * 128:(s + 1) * 128] = lo[s].astype(BF16)
            xb_s[:, half + s * 128:half + (s + 1) * 128] = hi[s].astype(BF16)
        gu = _dot(xb_s[...], wgu_s[...]) + bgu_ref[0]
        g = jnp.minimum(gu[:, :d_ff], SWIGLU_LIMIT)
        u = jnp.clip(gu[:, d_ff:], -SWIGLU_LIMIT, SWIGLU_LIMIT)
        act = (g * jax.nn.sigmoid(SWIGLU_ALPHA * g) * (u + 1.0)).astype(BF16)
        y = _dot(act, wd_s[...]) + bd_ref[0]
        _store_packed(o_ref, 0, bm, _pack_words(y))

    @pl.when(i >= nb_ref[0])
    def _():
        o_ref[...] = jnp.zeros_like(o_ref)


def _experts(xg, block_e, n_used, layer, w_gu, b_gu, w_down, b_down, bm):
    depth, n_e, d, ff2 = w_gu.shape
    d_ff = ff2 // 2
    s_rows = d // 256
    n_blocks = xg.shape[0] // (bm * s_rows)
    used = lambda i, nb: jnp.minimum(i, nb[0] - 1)
    rows = pl.BlockSpec((bm * s_rows, 128), lambda i, be, nb: (used(i, nb), 0))
    expert = lambda i, be, nb: (layer, be[used(i, nb)], 0, 0)
    once = pl.Buffered(1)
    grid_spec = pltpu.PrefetchScalarGridSpec(
        num_scalar_prefetch=2,
        grid=(n_blocks,),
        in_specs=[rows,
                  pl.BlockSpec((None, 1, d, ff2), expert, pipeline_mode=once),
                  pl.BlockSpec((None, 1, 1, ff2), expert),
                  pl.BlockSpec((None, 1, d_ff, d), expert, pipeline_mode=once),
                  pl.BlockSpec((None, 1, 1, d), expert)],
        out_specs=pl.BlockSpec((bm * s_rows, 128), lambda i, be, nb: (i, 0)),
        scratch_shapes=[pltpu.VMEM((d, ff2), BF16), pltpu.VMEM((d_ff, d), BF16), pltpu.VMEM((bm, d), BF16)],
    )
    return pl.pallas_call(
        functools.partial(_expert_kernel, d_ff=d_ff, bm=bm, s_rows=s_rows),
        grid_spec=grid_spec,
        out_shape=jax.ShapeDtypeStruct(xg.shape, jnp.uint32),
        compiler_params=_params(("arbitrary",), 58),
        name="experts",
    )(block_e, n_used, xg, w_gu, b_gu.reshape(depth, n_e, 1, ff2), w_down, b_down.reshape(depth, n_e, 1, d))


def _combine_kernel(dest_ref, g_ref, x_ref, gate_ref, lg_ref, lb_ref, y_ref, o_ref, idx_ref, buf_ref, sem, idx_sem,
                    *, alpha, rt, s_rows):
    load = pltpu.make_async_copy(dest_ref, idx_ref, idx_sem)
    load.start()
    load.wait()

    def token(t, c):
        for k in range(TOP_K):
            pltpu.make_async_copy(_row_slab(y_ref, idx_ref[k, t], s_rows),
                                  _row_slab(buf_ref, k * rt + t, s_rows), sem).start()
        return c

    lax.fori_loop(0, rt, token, 0)
    for k in range(TOP_K):
        part = pl.ds(k * rt * s_rows, rt * s_rows)
        pltpu.make_async_copy(y_ref.at[pl.ds(0, rt * s_rows), :], buf_ref.at[part, :], sem).wait()

    g = g_ref[...]
    acc = None
    for k in range(TOP_K):
        lo, hi = _load_packed(buf_ref, k * rt, rt, s_rows)
        term = g[:, k:k + 1] * jnp.concatenate(lo + hi, axis=1)
        acc = term if acc is None else acc + term
    o_ref[...] = _layer_norm(alpha * x_ref[...] + gate_ref[0] * acc, lg_ref[...], lb_ref[...])


def _combine(rs, y, dest, gates, x, mods, layer, ln_g, ln_b, alpha):
    rows, d = x.shape
    s_rows = d // 256
    rt = rs.tile(256)
    row = lambda i: (i, 0)
    const = lambda i: (0, 0)
    return pl.pallas_call(
        functools.partial(_combine_kernel, alpha=alpha, rt=rt, s_rows=s_rows),
        grid=(rows // rt,),
        in_specs=[pl.BlockSpec((TOP_K, rt), lambda i: (0, i)),
                  pl.BlockSpec((rt, TOP_K), row),
                  pl.BlockSpec((rt, d), row),
                  pl.BlockSpec((1, 1, d), rs.mod_map(layer, 5, rt)),
                  pl.BlockSpec((1, d), const),
                  pl.BlockSpec((1, d), const),
                  pl.BlockSpec(memory_space=pl.ANY)],
        out_specs=pl.BlockSpec((rt, d), row),
        out_shape=jax.ShapeDtypeStruct((rows, d), F32),
        scratch_shapes=[pltpu.SMEM((TOP_K, rt), jnp.int32),
                        pltpu.VMEM((TOP_K * rt * s_rows, 128), jnp.uint32),
                        pltpu.SemaphoreType.DMA,
                        pltpu.SemaphoreType.DMA],
        compiler_params=_params(("arbitrary",), 40),
        name="combine_ln",
    )(dest, gates, x, mods, ln_g, ln_b, y)


def _moe(rs, hp, logits_t, x, mods, layer, ln_g, ln_b, w_gu, b_gu, w_down, b_down, alpha):
    rows = x.shape[0]
    n_e = w_gu.shape[1]
    bm = 512 if rows * TOP_K >= 512 * n_e * 4 else 128
    e_t, g_t, r_t, cnt = _route(logits_t)
    counts = cnt[:, 0].astype(jnp.int32)
    padded = (counts + bm - 1) // bm * bm
    pad_end = jnp.cumsum(padded)
    first = pad_end - padded
    experts = jnp.arange(n_e, dtype=jnp.int32)
    first_of = jnp.sum(jnp.where(e_t[None] == experts[:, None, None], first[:, None, None], 0), axis=0)
    dest = first_of + r_t
    n_blocks = -(-(rows * TOP_K + n_e * (bm - 1)) // bm)
    block_start = jnp.arange(n_blocks, dtype=jnp.int32) * bm
    block_e = jnp.minimum(jnp.sum(pad_end[None, :] <= block_start[:, None], axis=1), n_e - 1).astype(jnp.int32)
    n_used = (pad_end[-1:] // bm).astype(jnp.int32)
    xg = _dispatch(hp, dest, pad_end, padded, n_blocks * bm, bm, rs.tile(256))
    y = _experts(xg, block_e, n_used, layer, w_gu, b_gu, w_down, b_down, bm)
    return _combine(rs, y, dest, g_t.T, x, mods, layer, ln_g, ln_b, alpha)


def kernel(x, c, ctx, c_ctx, ada_w, ada_b, ln_g, ln_b, sc_w_in, sc_conv, sc_w_out, gla_w_in, gla_w_gate2, gla_b_gate, gla_norm, gla_w_out, lru_w_in, lru_conv, lru_conv_b, lru_w_gate, lru_b_gate, lru_lambda, lru_w_out, hg_w_in, hg_lb_raw, hg_norm, hg_w_out, moe_w_router, moe_b_router, moe_w_gu, moe_b_gu, moe_w_down, moe_b_down):
    batch, n_lat, d = x.shape
    n_ctx = ctx.shape[1]
    depth = ada_w.shape[0]
    alpha = (2 * depth) ** 0.25
    assert GRID_W & (GRID_W - 1) == 0 and n_ctx & (n_ctx - 1) == 0 and batch < MOD_ROWS
    rs = _Rows(batch, n_ctx, n_lat)

    cc = jnp.zeros((MOD_ROWS, d), F32).at[:batch].set(c).at[batch].set(c_ctx)
    mods = _adaln(cc, ada_w, ada_b).reshape(depth * MOD_ROWS * N_MOD, 1, d)
    xs = jnp.concatenate([ctx.reshape(-1, d), x.reshape(-1, d)], axis=0)

    for i in range(depth):
        kind, j = i % 4, i // 4
        with_ctx = i < depth - 1
        if kind == 0:
            w_in = sc_w_in[j].astype(BF16)
            a = _in_proj("sconv", rs, xs, mods, i, [(w_in, 0), (w_in, 1), (w_in, 2)], d, [BF16],
                         extra=[sc_conv[j]])
            w_out = sc_w_out[j]
        elif kind == 1:
            dv = gla_norm.shape[-1]
            dk = dv // 2
            heads = gla_w_gate2.shape[-1] // dk
            kd_w, vd_w = heads * dk, heads * dv
            n_main = 2 * kd_w + 2 * vd_w
            rank = gla_w_gate2.shape[2]
            w_in = gla_w_in[j]
            z = _in_proj("plain", rs, xs, mods, i, [(w_in[:, :n_main].astype(BF16), 0)], n_main, [BF16],
                         tn_pref=512)
            w2 = jnp.zeros((2 * rank, 2 * kd_w), F32)
            w2 = w2.at[:rank, :kd_w].set(gla_w_gate2[j, 0]).at[rank:, kd_w:].set(gla_w_gate2[j, 1])
            gates = _gla_gates(rs, xs, mods, i, w_in[:, n_main:].astype(BF16), w2.astype(BF16),
                               gla_b_gate[j].reshape(1, 2 * kd_w))
            qkv = [(z, 0), (z, 1), (z, kd_w * 2 // vd_w)]
            scale = dk ** -0.5
            o_f = _gla_scan(rs, False, False, qkv, 0, gates, heads, dk, dv, scale)
            a = _gla_scan(rs, False, True, qkv, 1, gates, heads, dk, dv, scale,
                          final_args=(z, (2 * kd_w + vd_w) // vd_w, o_f, gla_norm[j].reshape(1, dv)))
            w_out = gla_w_out[j]
        elif kind == 2:
            w = lru_w_in.shape[-1] // 2
            w_in = lru_w_in[j].astype(BF16)
            y, xc = _in_proj("lru", rs, xs, mods, i, [(w_in, 0), (w_in, 1)], w, [BF16, BF16],
                             extra=[lru_conv[j], lru_conv_b[j].reshape(1, w)])
            log_sig = _log_sigmoid(lru_lambda[j])
            wg, bgate = lru_w_gate[j].astype(BF16), lru_b_gate[j]
            h_f = _lru_scan(rs, False, xc, wg[0], bgate[0], log_sig[0:1])
            a = _lru_scan(rs, True, xc, wg[1], bgate[1], log_sig[1:2], final_args=(h_f, y))
            w_out = lru_w_out[j]
        else:
            dk = hg_norm.shape[-1]
            heads = d // dk
            z = _in_proj("plain", rs, xs, mods, i, [(hg_w_in[j].astype(BF16), 0)], 5 * d, [BF16], tn_pref=512)
            p = jax.nn.softmax(hg_lb_raw, axis=0)
            lower = (jnp.cumsum(p, axis=0)[i] - p[0]).reshape(1, d)
            scale = dk ** -0.5
            o_f = _gla_scan(rs, True, False, [(z, 0), (z, 1), (z, 3)], 0, lower, heads, dk, dk, scale)
            a = _gla_scan(rs, True, True, [(z, 0), (z, 2), (z, 3)], 0, lower, heads, dk, dk, scale,
                          final_args=(z, 4, o_f, hg_norm[j].reshape(1, dk)))
            w_out = hg_w_out[j]

        if not with_ctx:
            a, xs = a[rs.ctx_rows:], xs[rs.ctx_rows:]
            rs = _Rows(batch, 0, n_lat)
        xs, hp, logits_t = _out_proj(rs, a, w_out.astype(BF16), xs, mods, i,
                                     ln_g[i, 0].reshape(1, d), ln_b[i, 0].reshape(1, d),
                                     moe_w_router[i].T.astype(BF16), moe_b_router[i].reshape(-1, 1), alpha)
        xs = _moe(rs, hp, logits_t, xs, mods, i, ln_g[i, 1].reshape(1, d), ln_b[i, 1].reshape(1, d),
                  moe_w_gu, moe_b_gu, moe_w_down, moe_b_down, alpha)
    return xs[rs.ctx_rows:].reshape(batch, n_lat, d)
```

```python
import functools

import jax
import jax.numpy as jnp
from jax import lax
from jax.experimental import pallas as pl
from jax.experimental.pallas import tpu as pltpu

F32 = jnp.float32
BF16 = jnp.bfloat16

GRID_W = 64
CHUNK = 64
TOP_K = 4
N_MOD = 6
MOD_ROWS = 8
GLA_NORMALIZER = 16.0
RG_C = 8.0
SWIGLU_LIMIT = 7.0
SWIGLU_ALPHA = 1.702
LN_EPS = 1e-5
RMS_EPS = 1e-6
MIB = 1024 * 1024


def _params(sem, vmem_mib):
    return pltpu.CompilerParams(dimension_semantics=sem, vmem_limit_bytes=vmem_mib * MIB)


def _dot(a, b):
    return jnp.dot(a, b, preferred_element_type=F32)


def _dot_nt(a, b):
    return lax.dot_general(a, b, (((1,), (1,)), ((), ())), preferred_element_type=F32)


def _dot_tn(a, b):
    return lax.dot_general(a, b, (((0,), (0,)), ((), ())), preferred_element_type=F32)


def _log_sigmoid(x):
    return jnp.minimum(x, 0.0) - jnp.log1p(jnp.exp(-jnp.abs(x)))


def _pick(n, pref):
    t = min(n, pref)
    while n % t:
        t //= 2
    return t


class _Rows:
    def __init__(self, batch, n_ctx, n_lat):
        self.batch, self.n_ctx, self.n_lat = batch, n_ctx, n_lat
        self.ctx_rows = batch * n_ctx
        self.rows = self.ctx_rows + batch * n_lat

    def tile(self, pref):
        t = min(pref, self.n_lat)
        while self.n_lat % t or self.ctx_rows % t:
            t //= 2
        return t

    def mod_map(self, layer, which, rt):
        def index(i, *_):
            r0 = i * rt
            row = jnp.where(r0 < self.ctx_rows, self.batch, (r0 - self.ctx_rows) // self.n_lat)
            return ((layer * MOD_ROWS + row) * N_MOD + which, 0, 0)
        return index


def _adaln_kernel(c_ref, w_ref, b_ref, o_ref):
    c = c_ref[...]
    a = (c * jax.nn.sigmoid(c)).astype(BF16)
    o_ref[0] = _dot(a, w_ref[0].astype(BF16)) + b_ref[0]


def _adaln(cc, ada_w, ada_b):
    depth, d, n = ada_w.shape
    tn = _pick(n, 1024)
    return pl.pallas_call(
        _adaln_kernel,
        grid=(depth, n // tn),
        in_specs=[pl.BlockSpec((MOD_ROWS, d), lambda l, j: (0, 0)),
                  pl.BlockSpec((1, d, tn), lambda l, j: (l, 0, j)),
                  pl.BlockSpec((1, 1, tn), lambda l, j: (l, 0, j))],
        out_specs=pl.BlockSpec((1, MOD_ROWS, tn), lambda l, j: (l, 0, j)),
        out_shape=jax.ShapeDtypeStruct((depth, MOD_ROWS, n), F32),
        compiler_params=_params(("arbitrary", "arbitrary"), 40),
        name="adaln",
    )(cc, ada_w, ada_b.reshape(depth, 1, n))


def _modulate(x_ref, sh_ref, sc_ref, h_ref):
    @pl.when(pl.program_id(1) == 0)
    def _():
        h_ref[...] = (x_ref[...] * (1.0 + sc_ref[0]) + sh_ref[0]).astype(BF16)


def _segment_pos(shape, ctx_tiles, seg_ctx, seg_lat):
    mask = jnp.where(pl.program_id(0) < ctx_tiles, seg_ctx - 1, seg_lat - 1)
    return lax.broadcasted_iota(jnp.int32, shape, 0) & mask, mask


def _shift_rows(p, pos, seg_last, offset):
    rt = p.shape[0]
    rolled = pltpu.roll(p, (-offset) % rt, 0)
    ok = (pos + offset >= 0) & (pos + offset <= seg_last)
    return jnp.where(ok, rolled, 0.0)


def _proj_kernel(x_ref, sh_ref, sc_ref, w_ref, o_ref, h_ref):
    _modulate(x_ref, sh_ref, sc_ref, h_ref)
    o_ref[...] = _dot(h_ref[...], w_ref[...]).astype(o_ref.dtype)


def _sconv_in_kernel(x_ref, sh_ref, sc_ref, wb_ref, wc_ref, wv_ref, cw_ref, o_ref, h_ref,
                     *, ctx_tiles, seg_ctx, seg_lat):
    _modulate(x_ref, sh_ref, sc_ref, h_ref)
    h = h_ref[...]
    bg = _dot(h, wb_ref[...])
    p = _dot(h, wc_ref[...]) * _dot(h, wv_ref[...])
    pos, last = _segment_pos(p.shape, ctx_tiles, seg_ctx, seg_lat)
    cw = cw_ref[...]
    conv = (cw[0:1] * _shift_rows(p, pos, last, -1) + cw[1:2] * p
            + cw[2:3] * _shift_rows(p, pos, last, 1))
    o_ref[...] = (bg * conv).astype(o_ref.dtype)


def _lru_in_kernel(x_ref, sh_ref, sc_ref, wy_ref, wx_ref, cw_ref, cb_ref, y_ref, xc_ref, h_ref,
                   *, ctx_tiles, seg_ctx, seg_lat):
    _modulate(x_ref, sh_ref, sc_ref, h_ref)
    h = h_ref[...]
    y_ref[...] = jax.nn.gelu(_dot(h, wy_ref[...])).astype(y_ref.dtype)
    xb = _dot(h, wx_ref[...])
    pos, last = _segment_pos(xb.shape, ctx_tiles, seg_ctx, seg_lat)
    cw = cw_ref[...]
    conv = (cw[0:1] * _shift_rows(xb, pos, last, -1) + cw[1:2] * xb
            + cw[2:3] * _shift_rows(xb, pos, last, 1) + cw[3:4] * _shift_rows(xb, pos, last, 2))
    xc_ref[...] = (conv + cb_ref[...]).astype(xc_ref.dtype)


def _in_proj(kind, rs, x, mods, layer, weights, n_out, out_dtypes, extra=(), tn_pref=512, rt_pref=1024):
    rows, d = x.shape
    rt = rs.tile(rt_pref)
    tn = _pick(n_out, tn_pref)
    nj = n_out // tn
    in_specs = [pl.BlockSpec((rt, d), lambda i, j: (i, 0)),
                pl.BlockSpec((1, 1, d), rs.mod_map(layer, 0, rt)),
                pl.BlockSpec((1, 1, d), rs.mod_map(layer, 1, rt))]
    args = [x, mods, mods]
    for w, off in weights:
        in_specs.append(pl.BlockSpec((d, tn), functools.partial(lambda i, j, o: (0, o * nj + j), o=off)))
        args.append(w)
    for e in extra:
        in_specs.append(pl.BlockSpec((e.shape[0], tn), lambda i, j: (0, j)))
        args.append(e)
    out_specs = [pl.BlockSpec((rt, tn), lambda i, j: (i, j)) for _ in out_dtypes]
    out_shape = [jax.ShapeDtypeStruct((rows, n_out), dt) for dt in out_dtypes]
    ctx_tiles = rs.ctx_rows // rt
    if kind == "plain":
        body = _proj_kernel
    elif kind == "sconv":
        body = functools.partial(_sconv_in_kernel, ctx_tiles=ctx_tiles, seg_ctx=rs.n_ctx, seg_lat=GRID_W)
    else:
        body = functools.partial(_lru_in_kernel, ctx_tiles=ctx_tiles, seg_ctx=rs.n_ctx, seg_lat=GRID_W)
    single = len(out_dtypes) == 1
    return pl.pallas_call(
        body,
        grid=(rows // rt, nj),
        in_specs=in_specs,
        out_specs=out_specs[0] if single else out_specs,
        out_shape=out_shape[0] if single else out_shape,
        scratch_shapes=[pltpu.VMEM((rt, d), BF16)],
        compiler_params=_params(("arbitrary", "arbitrary"), 52),
        name=kind + "_in_proj",
    )(*args)


def _gla_gate_kernel(x_ref, sh_ref, sc_ref, wr_ref, w2_ref, b2_ref, o_ref):
    h = (x_ref[...] * (1.0 + sc_ref[0]) + sh_ref[0]).astype(BF16)
    r = _dot(h, wr_ref[...]).astype(BF16)
    pre = _dot(r, w2_ref[...]) + b2_ref[...]
    o_ref[...] = _log_sigmoid(pre) * (1.0 / GLA_NORMALIZER)


def _gla_gates(rs, x, mods, layer, w_r, w2, b2):
    rows, d = x.shape
    rt = rs.tile(256)
    n = w2.shape[1]
    return pl.pallas_call(
        _gla_gate_kernel,
        grid=(rows // rt,),
        in_specs=[pl.BlockSpec((rt, d), lambda i: (i, 0)),
                  pl.BlockSpec((1, 1, d), rs.mod_map(layer, 0, rt)),
                  pl.BlockSpec((1, 1, d), rs.mod_map(layer, 1, rt)),
                  pl.BlockSpec(w_r.shape, lambda i: (0, 0)),
                  pl.BlockSpec(w2.shape, lambda i: (0, 0)),
                  pl.BlockSpec((1, n), lambda i: (0, 0))],
        out_specs=pl.BlockSpec((rt, n), lambda i: (i, 0)),
        out_shape=jax.ShapeDtypeStruct((rows, n), F32),
        compiler_params=_params(("arbitrary",), 40),
        name="gla_gates",
    )(x, mods, mods, w_r, w2, b2)


def _gla_scan_kernel(*refs, hgrn, reverse, final, heads, group, dk, dv, n_chunks, qscale):
    if hgrn:
        q_ref, f_ref, v_ref, lb_ref = refs[:4]
        rest = refs[4:]
    else:
        q_ref, k_ref, v_ref, g_ref = refs[:4]
        rest = refs[4:]
    if final:
        og_ref, of_ref, gain_ref, o_ref, st_ref = rest
    else:
        o_ref, st_ref = rest

    @pl.when(pl.program_id(1) == 0)
    def _():
        st_ref[...] = jnp.zeros_like(st_ref)

    tc = n_chunks * CHUNK
    row = lax.broadcasted_iota(jnp.int32, (tc, tc), 0)
    col = lax.broadcasted_iota(jnp.int32, (tc, tc), 1)
    shift = CHUNK.bit_length() - 1
    same_chunk = (row >> shift) == (col >> shift)
    tri = same_chunk & ((row <= col) if reverse else (row >= col))
    tri_b = jnp.where(tri, 1.0, 0.0).astype(BF16)
    order = range(n_chunks - 1, -1, -1) if reverse else range(n_chunks)

    def intra(h):
        kcols = pl.ds(pl.multiple_of(h * dk, dk), dk)
        vcols = pl.ds(pl.multiple_of(h * dv, dv), dv)
        qf = q_ref[:, kcols].astype(F32)
        if hgrn:
            lb = lb_ref[:, kcols]
            s = jax.nn.sigmoid(f_ref[:, kcols].astype(F32))
            qf = qf * jax.nn.sigmoid(qf)
            kf = (1.0 - lb) * (1.0 - s)
            g = jnp.log(lb + (1.0 - lb) * s)
        else:
            kf = k_ref[:, kcols].astype(F32)
            g = g_ref[:, kcols]
        v = v_ref[:, vcols]
        g_hi = g.astype(BF16)
        g_lo = (g - g_hi.astype(F32)).astype(BF16)
        b = _dot(tri_b, g_hi) + _dot(tri_b, g_lo)
        qd = (qf * (qscale * jnp.exp(b))).astype(BF16)
        kdf = kf * jnp.exp(-b)
        att = jnp.where(tri, _dot_nt(qd, kdf.astype(BF16)), 0.0).astype(BF16)
        return vcols, b, qd, kdf, v, _dot(att, v)

    def finish(vcols, outs):
        o = jnp.concatenate(outs, axis=0)
        if final:
            o = o + of_ref[:, vcols]
            o = o * lax.rsqrt(jnp.mean(o * o, axis=-1, keepdims=True) + RMS_EPS) * gain_ref[...]
            og = og_ref[:, vcols].astype(F32)
            o = o * (og * jax.nn.sigmoid(og))
        o_ref[:, vcols] = o.astype(o_ref.dtype)

    def head_group(i, carry):
        hs = [i * group + j for j in range(group)]
        parts = [intra(h) for h in hs]
        states = [st_ref[h] for h in hs]
        outs = [[None] * n_chunks for _ in hs]
        for c in order:
            rows = slice(c * CHUNK, (c + 1) * CHUNK)
            end = c * CHUNK if reverse else (c + 1) * CHUNK - 1
            for j, (_, b, qd, kdf, v, o_intra) in enumerate(parts):
                st = states[j]
                outs[j][c] = o_intra[rows] + _dot_nt(qd[rows], st.astype(BF16))
                dec = jnp.exp(b[end:end + 1])
                states[j] = st * dec + _dot_tn(v[rows], (kdf[rows] * dec).astype(BF16))
        for j, h in enumerate(hs):
            st_ref[h] = states[j]
            finish(parts[j][0], outs[j])
        return carry

    lax.fori_loop(0, heads // group, head_group, 0)


def _gla_scan(rs, hgrn, reverse, qkv, col_blocks, extra, heads, dk, dv, qscale, final_args=None):
    tc = min(256, rs.n_ctx)
    ctx_blocks, lat_blocks = rs.n_ctx // tc, rs.n_lat // tc
    steps = ctx_blocks + lat_blocks
    kd_w, vd_w = heads * dk, heads * dv

    def row_block(b, s):
        if reverse:
            ctx = b * ctx_blocks + (ctx_blocks - 1 - s)
            lat = rs.ctx_rows // tc + b * lat_blocks + (steps - 1 - s)
        else:
            ctx = b * ctx_blocks + s
            lat = rs.ctx_rows // tc + b * lat_blocks + (s - ctx_blocks)
        return jnp.where(s < ctx_blocks, ctx, lat)

    def spec(width, cb):
        return pl.BlockSpec((tc, width), lambda b, s: (row_block(b, s), cb))

    in_specs = [spec(kd_w, qkv[0][1]), spec(kd_w, qkv[1][1]), spec(vd_w, qkv[2][1])]
    args = [qkv[0][0], qkv[1][0], qkv[2][0]]
    if hgrn:
        in_specs.append(pl.BlockSpec((1, kd_w), lambda b, s: (0, 0)))
    else:
        in_specs.append(spec(kd_w, col_blocks))
    args.append(extra)
    final = final_args is not None
    if final:
        z, gcb, o_fwd, gain = final_args
        in_specs += [spec(vd_w, gcb), spec(vd_w, 0), pl.BlockSpec((1, dv), lambda b, s: (0, 0))]
        args += [z, o_fwd, gain]
    return pl.pallas_call(
        functools.partial(_gla_scan_kernel, hgrn=hgrn, reverse=reverse, final=final, heads=heads,
                          group=8 if heads % 8 == 0 else 4 if heads % 4 == 0 else 2 - heads % 2,
                          dk=dk, dv=dv, n_chunks=tc // CHUNK, qscale=qscale),
        grid=(rs.batch, steps),
        in_specs=in_specs,
        out_specs=spec(vd_w, 0),
        out_shape=jax.ShapeDtypeStruct((rs.rows, vd_w), BF16 if final else F32),
        scratch_shapes=[pltpu.VMEM((heads, dv, dk), F32)],
        compiler_params=_params(("arbitrary", "arbitrary"), 40),
        name=("hgrn" if hgrn else "gla") + ("_bwd" if reverse else "_fwd"),
    )(*args)


def _lru_scan_kernel(*refs, reverse, final, n_blk, bw):
    if final:
        xc_ref, wg_ref, bg_ref, ls_ref, hf_ref, y_ref, o_ref, a_ref, u_ref, h_ref = refs
    else:
        xc_ref, wg_ref, bg_ref, ls_ref, o_ref, a_ref, u_ref, h_ref = refs
    tc, w = xc_ref.shape
    groups = tc // 8

    @pl.when(pl.program_id(1) == 0)
    def _():
        h_ref[...] = jnp.zeros_like(h_ref)

    for n in range(n_blk):
        cols = slice(n * bw, (n + 1) * bw)
        xb = xc_ref[:, cols]
        gr = jax.nn.sigmoid(_dot(xb, wg_ref[0, n]) + bg_ref[0:1, cols])
        gi = jax.nn.sigmoid(_dot(xb, wg_ref[1, n]) + bg_ref[1:2, cols])
        log_a = RG_C * gr * ls_ref[:, cols]
        a = jnp.exp(log_a)
        a_ref[:, cols] = a
        u_ref[:, cols] = jnp.sqrt(-jnp.tanh(log_a) * (a * a + 1.0)) * gi * xb.astype(F32)

    sub = lax.broadcasted_iota(jnp.int32, (8, w), 0)

    def group(i, h):
        g = (groups - 1 - i) if reverse else i
        rows = pl.ds(pl.multiple_of(g * 8, 8), 8)
        a, u = a_ref[rows, :], u_ref[rows, :]
        for k in (1, 2, 4):
            if reverse:
                ok = sub < 8 - k
                a_s, u_s = pltpu.roll(a, 8 - k, 0), pltpu.roll(u, 8 - k, 0)
            else:
                ok = sub >= k
                a_s, u_s = pltpu.roll(a, k, 0), pltpu.roll(u, k, 0)
            u = jnp.where(ok, a * u_s + u, u)
            a = jnp.where(ok, a * a_s, a)
        out = a * h + u
        last = out[0:1] if reverse else out[7:8]
        res = out
        if final:
            res = (out + hf_ref[rows, :]) * y_ref[rows, :].astype(F32)
        o_ref[rows, :] = res.astype(o_ref.dtype)
        return jnp.broadcast_to(last, (8, w))

    h_ref[...] = lax.fori_loop(0, groups, group, h_ref[...])


def _lru_scan(rs, reverse, xc, w_gate, b_gate, log_sig_lam, final_args=None):
    tc = min(256, rs.n_ctx)
    ctx_blocks, lat_blocks = rs.n_ctx // tc, rs.n_lat // tc
    steps = ctx_blocks + lat_blocks
    w = xc.shape[1]
    n_blk, bw = w_gate.shape[1], w_gate.shape[2]

    def row_block(b, s):
        if reverse:
            ctx = b * ctx_blocks + (ctx_blocks - 1 - s)
            lat = rs.ctx_rows // tc + b * lat_blocks + (steps - 1 - s)
        else:
            ctx = b * ctx_blocks + s
            lat = rs.ctx_rows // tc + b * lat_blocks + (s - ctx_blocks)
        return jnp.where(s < ctx_blocks, ctx, lat)

    stream = pl.BlockSpec((tc, w), lambda b, s: (row_block(b, s), 0))
    in_specs = [stream,
                pl.BlockSpec(w_gate.shape, lambda b, s: (0, 0, 0, 0)),
                pl.BlockSpec((2, w), lambda b, s: (0, 0)),
                pl.BlockSpec((1, w), lambda b, s: (0, 0))]
    args = [xc, w_gate, b_gate, log_sig_lam]
    final = final_args is not None
    if final:
        in_specs += [stream, stream]
        args += list(final_args)
    return pl.pallas_call(
        functools.partial(_lru_scan_kernel, reverse=reverse, final=final, n_blk=n_blk, bw=bw),
        grid=(rs.batch, steps),
        in_specs=in_specs,
        out_specs=stream,
        out_shape=jax.ShapeDtypeStruct((rs.rows, w), BF16 if final else F32),
        scratch_shapes=[pltpu.VMEM((tc, w), F32), pltpu.VMEM((tc, w), F32), pltpu.VMEM((8, w), F32)],
        compiler_params=_params(("arbitrary", "arbitrary"), 40),
        name="lru_bwd" if reverse else "lru_fwd",
    )(*args)


def _layer_norm(z, g, b):
    mu = jnp.mean(z, axis=-1, keepdims=True)
    zc = z - mu
    var = jnp.mean(zc * zc, axis=-1, keepdims=True)
    return zc * lax.rsqrt(var + LN_EPS) * g + b


def _pack_words(v):
    half = v.shape[1] // 2
    vb = v.astype(BF16).astype(F32)
    lo = lax.bitcast_convert_type(vb[:, :half], jnp.uint32) >> 16
    hi = lax.bitcast_convert_type(vb[:, half:], jnp.uint32) & jnp.uint32(0xFFFF0000)
    return hi | lo


def _store_packed(ref, row0, n, words):
    s_rows = words.shape[1] // 128
    for s in range(s_rows):
        ref[pl.ds(row0 * s_rows + s, n, stride=s_rows), :] = words[:, s * 128:(s + 1) * 128]


def _load_packed(ref, row0, n, s_rows):
    lo, hi = [], []
    for s in range(s_rows):
        w = ref[pl.ds(row0 * s_rows + s, n, stride=s_rows), :]
        lo.append(lax.bitcast_convert_type(w << 16, F32))
        hi.append(lax.bitcast_convert_type(w & jnp.uint32(0xFFFF0000), F32))
    return lo, hi


def _out_proj_kernel(a_ref, w_ref, x_ref, gate_ref, sh_ref, sc_ref, lg_ref, lb_ref, wr_ref, br_ref,
                     xo_ref, hp_ref, lt_ref, *, alpha):
    y = _dot(a_ref[...], w_ref[...])
    x_new = _layer_norm(alpha * x_ref[...] + gate_ref[0] * y, lg_ref[...], lb_ref[...])
    xo_ref[...] = x_new
    h = x_new * (1.0 + sc_ref[0]) + sh_ref[0]
    _store_packed(hp_ref, 0, h.shape[0], _pack_words(h))
    lt_ref[...] = _dot_nt(wr_ref[...], h.astype(BF16)) + br_ref[...]


def _out_proj(rs, a, w_out, x, mods, layer, ln_g, ln_b, w_router_t, b_router, alpha, skip_rows=0):
    rows, d = rs.rows, x.shape[1]
    k = a.shape[1]
    n_e = w_router_t.shape[0]
    rt = rs.tile(256)
    skip = skip_rows // rt
    row = lambda i: (i, 0)
    row_in = lambda i: (i + skip, 0)
    const = lambda i: (0, 0)
    return pl.pallas_call(
        functools.partial(_out_proj_kernel, alpha=alpha),
        grid=(rows // rt,),
        in_specs=[pl.BlockSpec((rt, k), row_in),
                  pl.BlockSpec((k, d), const),
                  pl.BlockSpec((rt, d), row_in),
                  pl.BlockSpec((1, 1, d), rs.mod_map(layer, 2, rt)),
                  pl.BlockSpec((1, 1, d), rs.mod_map(layer, 3, rt)),
                  pl.BlockSpec((1, 1, d), rs.mod_map(layer, 4, rt)),
                  pl.BlockSpec((1, d), const),
                  pl.BlockSpec((1, d), const),
                  pl.BlockSpec((n_e, d), const),
                  pl.BlockSpec((n_e, 1), const)],
        out_specs=[pl.BlockSpec((rt, d), row),
                   pl.BlockSpec((rt * (d // 256), 128), row),
                   pl.BlockSpec((n_e, rt), lambda i: (0, i))],
        out_shape=[jax.ShapeDtypeStruct((rows, d), F32),
                   jax.ShapeDtypeStruct((rows * (d // 256), 128), jnp.uint32),
                   jax.ShapeDtypeStruct((n_e, rows), F32)],
        compiler_params=_params(("arbitrary",), 48),
        name="out_proj_ln",
    )(a, w_out, x, mods, mods, mods, ln_g, ln_b, w_router_t, b_router)


def _route_kernel(lt_ref, e_ref, g_ref, r_ref, cnt_ref, carry_ref):
    n_e, tr = lt_ref.shape

    @pl.when(pl.program_id(0) == 0)
    def _():
        carry_ref[...] = jnp.zeros_like(carry_ref)

    lg = lt_ref[...]
    eid = lax.broadcasted_iota(jnp.int32, (n_e, tr), 0)
    vals, idxs, sels = [], [], []
    for _ in range(TOP_K):
        m = jnp.max(lg, axis=0, keepdims=True)
        idx = jnp.min(jnp.where(lg == m, eid, n_e), axis=0, keepdims=True)
        sel = eid == idx
        vals.append(m)
        idxs.append(idx)
        sels.append(sel)
        lg = jnp.where(sel, -jnp.inf, lg)
    ex = [jnp.exp(v - vals[0]) for v in vals]
    denom = ex[0] + ex[1] + ex[2] + ex[3]
    chosen = sels[0] | sels[1] | sels[2] | sels[3]
    member = jnp.where(chosen, 1.0, 0.0)
    srow = lax.broadcasted_iota(jnp.int32, (tr, tr), 0)
    scol = lax.broadcasted_iota(jnp.int32, (tr, tr), 1)
    before = jnp.where(srow < scol, 1.0, 0.0).astype(BF16)
    prefix = _dot(member.astype(BF16), before) + carry_ref[...]
    for k in range(TOP_K):
        e_ref[k:k + 1, :] = idxs[k]
        g_ref[k:k + 1, :] = ex[k] / denom
        r_ref[k:k + 1, :] = jnp.sum(jnp.where(sels[k], prefix, 0.0), axis=0, keepdims=True).astype(jnp.int32)
    carry_ref[...] = carry_ref[...] + jnp.sum(member, axis=1, keepdims=True)
    cnt_ref[...] = carry_ref[...]


def _route(logits_t):
    n_e, rows = logits_t.shape
    tr = _pick(rows, 512)
    tok = pl.BlockSpec((TOP_K, tr), lambda i: (0, i))
    return pl.pallas_call(
        _route_kernel,
        grid=(rows // tr,),
        in_specs=[pl.BlockSpec((n_e, tr), lambda i: (0, i))],
        out_specs=[tok, tok, tok, pl.BlockSpec((n_e, 1), lambda i: (0, 0))],
        out_shape=[jax.ShapeDtypeStruct((TOP_K, rows), jnp.int32),
                   jax.ShapeDtypeStruct((TOP_K, rows), F32),
                   jax.ShapeDtypeStruct((TOP_K, rows), jnp.int32),
                   jax.ShapeDtypeStruct((n_e, 1), F32)],
        scratch_shapes=[pltpu.VMEM((n_e, 1), F32)],
        compiler_params=_params(("arbitrary",), 32),
        name="route",
    )(logits_t)


def _row_slab(ref, row, s_rows):
    return ref.at[pl.ds(pl.multiple_of(row * s_rows, s_rows), s_rows), :]


def _dispatch_kernel(pe_ref, pd_ref, dest_ref, hp_ref, xg_ref, idx_ref, zero_ref, sem, idx_sem,
                     *, rt, bm, s_rows, n_e, n_blocks):
    @pl.when(pl.program_id(0) == 0)
    def _():
        zero_ref[...] = jnp.zeros_like(zero_ref)
        def fill(last_row):
            first = pl.multiple_of((last_row - bm) * s_rows, s_rows)
            return pltpu.make_async_copy(zero_ref, xg_ref.at[pl.ds(first, bm * s_rows), :], sem)

        def start(e, c):
            @pl.when(pd_ref[e] > 0)
            def _():
                fill(pe_ref[e]).start()
            return c

        def wait(e, c):
            @pl.when(pd_ref[e] > 0)
            def _():
                fill(pe_ref[e]).wait()
            return c

        lax.fori_loop(0, n_e, start, 0)
        lax.fori_loop(0, n_e, wait, 0)
        n_used = pe_ref[n_e - 1] // bm
        lax.fori_loop(n_used, n_blocks, lambda j, c: (fill((j + 1) * bm).start(), c)[1], 0)
        lax.fori_loop(n_used, n_blocks, lambda j, c: (fill((j + 1) * bm).wait(), c)[1], 0)

    load = pltpu.make_async_copy(dest_ref, idx_ref, idx_sem)
    load.start()
    load.wait()

    def token(t, c):
        src = _row_slab(hp_ref, t, s_rows)
        for k in range(TOP_K):
            pltpu.make_async_copy(src, _row_slab(xg_ref, idx_ref[k, t], s_rows), sem).start(priority=k % 2)
        return c

    lax.fori_loop(0, rt, token, 0)
    for _ in range(TOP_K):
        pltpu.make_async_copy(hp_ref, xg_ref.at[pl.ds(0, rt * s_rows), :], sem).wait()


def _dispatch(hp, dest, pad_end, padded, n_rows, bm, rt):
    s_rows = hp.shape[0] // dest.shape[1]
    rows = dest.shape[1]
    n_e = pad_end.shape[0]
    grid_spec = pltpu.PrefetchScalarGridSpec(
        num_scalar_prefetch=2,
        grid=(rows // rt,),
        in_specs=[pl.BlockSpec((TOP_K, rt), lambda i, pe, pd: (0, i)),
                  pl.BlockSpec((rt * s_rows, 128), lambda i, pe, pd: (i, 0))],
        out_specs=pl.BlockSpec(memory_space=pl.ANY),
        scratch_shapes=[pltpu.SMEM((TOP_K, rt), jnp.int32),
                        pltpu.VMEM((bm * s_rows, 128), jnp.uint32),
                        pltpu.SemaphoreType.DMA,
                        pltpu.SemaphoreType.DMA],
    )
    return pl.pallas_call(
        functools.partial(_dispatch_kernel, rt=rt, bm=bm, s_rows=s_rows, n_e=n_e, n_blocks=n_rows // bm),
        grid_spec=grid_spec,
        out_shape=jax.ShapeDtypeStruct((n_rows * s_rows, 128), jnp.uint32),
        compiler_params=_params(("arbitrary",), 32),
        name="dispatch",
    )(pad_end, padded, dest, hp)


N_STAGE = 4


def _expert_kernel(be_ref, nb_ref, nx_ref, c0_ref, c1_ref, par_ref,
                   x_ref, wgu_hbm, bgu_ref, wd_hbm, bd_ref, o_ref,
                   wgu_s, wd_s, xb_s, sg_ref, sd_ref, sem_g, sem_d,
                   *, layer, d_ff, bm, s_rows, wrows, n_gu, n_d):
    i = pl.program_id(0)
    n_chunks = n_gu + n_d

    def gu_copy(e, c):
        rows = pl.ds(pl.multiple_of(c * wrows, wrows), wrows)
        return pltpu.make_async_copy(wgu_hbm.at[layer, e, rows, :], sg_ref.at[c % N_STAGE], sem_g.at[c % N_STAGE])

    def down_copy(e, c):
        rows = pl.ds(pl.multiple_of(c * wrows, wrows), wrows)
        return pltpu.make_async_copy(wd_hbm.at[layer, e, rows, :], sd_ref.at[c % N_STAGE], sem_d.at[c % N_STAGE])

    def start(e, c):
        @pl.when(c < n_gu)
        def _():
            gu_copy(e, c).start()

        @pl.when((c >= n_gu) & (c < n_chunks))
        def _():
            down_copy(e, c - n_gu).start()

    def land(e, c, buf):
        @pl.when(c < n_gu)
        def _():
            gu_copy(e, c).wait()
            rows = pl.ds(pl.multiple_of(c * wrows, wrows), wrows)
            wgu_s[buf, rows, :] = sg_ref[c % N_STAGE].astype(BF16)

        @pl.when(c >= n_gu)
        def _():
            cd = c - n_gu
            down_copy(e, cd).wait()
            rows = pl.ds(pl.multiple_of(cd * wrows, wrows), wrows)
            wd_s[buf, rows, :] = sd_ref[cd % N_STAGE].astype(BF16)

    def prime(e):
        for c in range(N_STAGE - 1):
            start(e, jnp.int32(c))

    def stream(e, lo, hi, buf):
        def body(c, carry):
            start(e, c + (N_STAGE - 1))
            land(e, c, buf)
            return carry
        lax.fori_loop(lo, hi, body, 0)

    @pl.when(i < nb_ref[0])
    def _():
        e, nxt, cur = be_ref[i], nx_ref[i], par_ref[i]
        first = (i == 0) | (e != be_ref[jnp.maximum(i - 1, 0)])

        @pl.when(i == 0)
        def _():
            prime(e)
            stream(e, 0, n_chunks, cur)

        @pl.when(first & (nxt >= 0))
        def _():
            prime(nxt)

        @pl.when(nxt >= 0)
        def _():
            stream(nxt, c0_ref[i], c1_ref[i], 1 - cur)

        half = s_rows * 128
        lo, hi = _load_packed(x_ref, 0, bm, s_rows)
        for s in range(s_rows):
            xb_s[:, s * 128:(s + 1) * 128] = lo[s].astype(BF16)
            xb_s[:, half + s * 128:half + (s + 1) * 128] = hi[s].astype(BF16)
        gu = _dot(xb_s[...], wgu_s[cur]) + bgu_ref[0]
        g = jnp.minimum(gu[:, :d_ff], SWIGLU_LIMIT)
        u = jnp.clip(gu[:, d_ff:], -SWIGLU_LIMIT, SWIGLU_LIMIT)
        act = (g * jax.nn.sigmoid(SWIGLU_ALPHA * g) * (u + 1.0)).astype(BF16)
        y = _dot(act, wd_s[cur]) + bd_ref[0]
        _store_packed(o_ref, 0, bm, _pack_words(y))

    @pl.when(i >= nb_ref[0])
    def _():
        o_ref[...] = jnp.zeros_like(o_ref)


def _experts(xg, pad_end, padded, layer, w_gu, b_gu, w_down, b_down, bm):
    depth, n_e, d, ff2 = w_gu.shape
    d_ff = ff2 // 2
    s_rows = d // 256
    n_blocks = xg.shape[0] // (bm * s_rows)
    wrows = _pick(d_ff, 128)
    n_gu, n_d = d // wrows, d_ff // wrows
    n_chunks = n_gu + n_d

    experts = jnp.arange(n_e, dtype=jnp.int32)
    block_start = jnp.arange(n_blocks, dtype=jnp.int32) * bm
    block_e = jnp.minimum(jnp.sum(pad_end[None, :] <= block_start[:, None], axis=1), n_e - 1).astype(jnp.int32)
    n_used = (pad_end[-1:] // bm).astype(jnp.int32)
    onehot = block_e[:, None] == experts[None, :]
    pick = lambda table: jnp.sum(jnp.where(onehot, table[None, :], 0), axis=1)
    blocks_of = padded // bm
    local = jnp.arange(n_blocks, dtype=jnp.int32) - pick((pad_end - padded) // bm)
    nblk = jnp.maximum(pick(blocks_of), 1)
    c0 = (n_chunks * local) // nblk
    c1 = (n_chunks * (local + 1)) // nblk
    live = blocks_of > 0
    later = (experts[None, :] > experts[:, None]) & live[None, :]
    next_of = jnp.min(jnp.where(later, experts[None, :], n_e), axis=1)
    next_of = jnp.where(next_of == n_e, -1, next_of)
    ordinal = jnp.cumsum(live.astype(jnp.int32)) - live.astype(jnp.int32)
    tables = [block_e, n_used, pick(next_of), c0, c1, pick(ordinal) % 2]
    tables = [t.astype(jnp.int32) for t in tables]

    used = lambda i, nb: jnp.minimum(i, nb[0] - 1)
    bias = lambda i, be, nb, *_: (layer, be[used(i, nb)], 0, 0)
    grid_spec = pltpu.PrefetchScalarGridSpec(
        num_scalar_prefetch=len(tables),
        grid=(n_blocks,),
        in_specs=[pl.BlockSpec((bm * s_rows, 128), lambda i, be, nb, *_: (used(i, nb), 0)),
                  pl.BlockSpec(memory_space=pl.ANY),
                  pl.BlockSpec((None, 1, 1, ff2), bias),
                  pl.BlockSpec(memory_space=pl.ANY),
                  pl.BlockSpec((None, 1, 1, d), bias)],
        out_specs=pl.BlockSpec((bm * s_rows, 128), lambda i, *_: (i, 0)),
        scratch_shapes=[pltpu.VMEM((2, d, ff2), BF16), pltpu.VMEM((2, d_ff, d), BF16), pltpu.VMEM((bm, d), BF16),
                        pltpu.VMEM((N_STAGE, wrows, ff2), F32), pltpu.VMEM((N_STAGE, wrows, d), F32),
                        pltpu.SemaphoreType.DMA((N_STAGE,)), pltpu.SemaphoreType.DMA((N_STAGE,))],
    )
    return pl.pallas_call(
        functools.partial(_expert_kernel, layer=layer, d_ff=d_ff, bm=bm, s_rows=s_rows,
                          wrows=wrows, n_gu=n_gu, n_d=n_d),
        grid_spec=grid_spec,
        out_shape=jax.ShapeDtypeStruct(xg.shape, jnp.uint32),
        compiler_params=_params(("arbitrary",), 58),
        name="experts",
    )(*tables, xg, w_gu, b_gu.reshape(depth, n_e, 1, ff2), w_down, b_down.reshape(depth, n_e, 1, d))


def _combine_kernel(dest_ref, g_ref, x_ref, gate_ref, lg_ref, lb_ref, y_ref, o_ref, idx_ref, buf_ref, sem, idx_sem,
                    *, alpha, rt, s_rows):
    load = pltpu.make_async_copy(dest_ref, idx_ref, idx_sem)
    load.start()
    load.wait()

    def token(t, c):
        for k in range(TOP_K):
            pltpu.make_async_copy(_row_slab(y_ref, idx_ref[k, t], s_rows),
                                  _row_slab(buf_ref, k * rt + t, s_rows), sem).start(priority=k % 2)
        return c

    lax.fori_loop(0, rt, token, 0)
    for k in range(TOP_K):
        part = pl.ds(k * rt * s_rows, rt * s_rows)
        pltpu.make_async_copy(y_ref.at[pl.ds(0, rt * s_rows), :], buf_ref.at[part, :], sem).wait()

    g = g_ref[...]
    acc = None
    for k in range(TOP_K):
        lo, hi = _load_packed(buf_ref, k * rt, rt, s_rows)
        term = g[:, k:k + 1] * jnp.concatenate(lo + hi, axis=1)
        acc = term if acc is None else acc + term
    o_ref[...] = _layer_norm(alpha * x_ref[...] + gate_ref[0] * acc, lg_ref[...], lb_ref[...])


def _combine(rs, y, dest, gates, x, mods, layer, ln_g, ln_b, alpha):
    rows, d = x.shape
    s_rows = d // 256
    rt = rs.tile(256)
    row = lambda i: (i, 0)
    const = lambda i: (0, 0)
    return pl.pallas_call(
        functools.partial(_combine_kernel, alpha=alpha, rt=rt, s_rows=s_rows),
        grid=(rows // rt,),
        in_specs=[pl.BlockSpec((TOP_K, rt), lambda i: (0, i)),
                  pl.BlockSpec((rt, TOP_K), row),
                  pl.BlockSpec((rt, d), row),
                  pl.BlockSpec((1, 1, d), rs.mod_map(layer, 5, rt)),
                  pl.BlockSpec((1, d), const),
                  pl.BlockSpec((1, d), const),
                  pl.BlockSpec(memory_space=pl.ANY)],
        out_specs=pl.BlockSpec((rt, d), row),
        out_shape=jax.ShapeDtypeStruct((rows, d), F32),
        scratch_shapes=[pltpu.SMEM((TOP_K, rt), jnp.int32),
                        pltpu.VMEM((TOP_K * rt * s_rows, 128), jnp.uint32),
                        pltpu.SemaphoreType.DMA,
                        pltpu.SemaphoreType.DMA],
        compiler_params=_params(("arbitrary",), 40),
        name="combine_ln",
    )(dest, gates, x, mods, ln_g, ln_b, y)


def _moe(rs, hp, logits_t, x, mods, layer, ln_g, ln_b, w_gu, b_gu, w_down, b_down, alpha):
    rows = x.shape[0]
    n_e = w_gu.shape[1]
    bm = 512 if rows * TOP_K >= 512 * n_e * 4 else 128
    e_t, g_t, r_t, cnt = _route(logits_t)
    counts = cnt[:, 0].astype(jnp.int32)
    padded = (counts + bm - 1) // bm * bm
    pad_end = jnp.cumsum(padded)
    first = pad_end - padded
    experts = jnp.arange(n_e, dtype=jnp.int32)
    first_of = jnp.sum(jnp.where(e_t[None] == experts[:, None, None], first[:, None, None], 0), axis=0)
    dest = first_of + r_t
    n_blocks = -(-(rows * TOP_K + n_e * (bm - 1)) // bm)
    xg = _dispatch(hp, dest, pad_end, padded, n_blocks * bm, bm, rs.tile(256))
    y = _experts(xg, pad_end, padded, layer, w_gu, b_gu, w_down, b_down, bm)
    return _combine(rs, y, dest, g_t.T, x, mods, layer, ln_g, ln_b, alpha)


def kernel(x, c, ctx, c_ctx, ada_w, ada_b, ln_g, ln_b, sc_w_in, sc_conv, sc_w_out, gla_w_in, gla_w_gate2, gla_b_gate, gla_norm, gla_w_out, lru_w_in, lru_conv, lru_conv_b, lru_w_gate, lru_b_gate, lru_lambda, lru_w_out, hg_w_in, hg_lb_raw, hg_norm, hg_w_out, moe_w_router, moe_b_router, moe_w_gu, moe_b_gu, moe_w_down, moe_b_down):
    batch, n_lat, d = x.shape
    n_ctx = ctx.shape[1]
    depth = ada_w.shape[0]
    alpha = (2 * depth) ** 0.25
    assert GRID_W & (GRID_W - 1) == 0 and n_ctx & (n_ctx - 1) == 0 and batch < MOD_ROWS
    rs = _Rows(batch, n_ctx, n_lat)

    cc = jnp.zeros((MOD_ROWS, d), F32).at[:batch].set(c).at[batch].set(c_ctx)
    mods = _adaln(cc, ada_w, ada_b).reshape(depth * MOD_ROWS * N_MOD, 1, d)
    xs = jnp.concatenate([ctx.reshape(-1, d), x.reshape(-1, d)], axis=0)

    for i in range(depth):
        kind, j = i % 4, i // 4
        with_ctx = i < depth - 1
        if kind == 0:
            w_in = sc_w_in[j].astype(BF16)
            a = _in_proj("sconv", rs, xs, mods, i, [(w_in, 0), (w_in, 1), (w_in, 2)], d, [BF16],
                         extra=[sc_conv[j]])
            w_out = sc_w_out[j]
        elif kind == 1:
            dv = gla_norm.shape[-1]
            dk = dv // 2
            heads = gla_w_gate2.shape[-1] // dk
            kd_w, vd_w = heads * dk, heads * dv
            n_main = 2 * kd_w + 2 * vd_w
            rank = gla_w_gate2.shape[2]
            w_in = gla_w_in[j]
            z = _in_proj("plain", rs, xs, mods, i, [(w_in[:, :n_main].astype(BF16), 0)], n_main, [BF16],
                         tn_pref=1024)
            w2 = jnp.zeros((2 * rank, 2 * kd_w), F32)
            w2 = w2.at[:rank, :kd_w].set(gla_w_gate2[j, 0]).at[rank:, kd_w:].set(gla_w_gate2[j, 1])
            gates = _gla_gates(rs, xs, mods, i, w_in[:, n_main:].astype(BF16), w2.astype(BF16),
                               gla_b_gate[j].reshape(1, 2 * kd_w))
            qkv = [(z, 0), (z, 1), (z, kd_w * 2 // vd_w)]
            scale = dk ** -0.5
            o_f = _gla_scan(rs, False, False, qkv, 0, gates, heads, dk, dv, scale)
            a = _gla_scan(rs, False, True, qkv, 1, gates, heads, dk, dv, scale,
                          final_args=(z, (2 * kd_w + vd_w) // vd_w, o_f, gla_norm[j].reshape(1, dv)))
            w_out = gla_w_out[j]
        elif kind == 2:
            w = lru_w_in.shape[-1] // 2
            w_in = lru_w_in[j].astype(BF16)
            y, xc = _in_proj("lru", rs, xs, mods, i, [(w_in, 0), (w_in, 1)], w, [BF16, BF16],
                             extra=[lru_conv[j], lru_conv_b[j].reshape(1, w)])
            log_sig = _log_sigmoid(lru_lambda[j])
            wg, bgate = lru_w_gate[j].astype(BF16), lru_b_gate[j]
            h_f = _lru_scan(rs, False, xc, wg[0], bgate[0], log_sig[0:1])
            a = _lru_scan(rs, True, xc, wg[1], bgate[1], log_sig[1:2], final_args=(h_f, y))
            w_out = lru_w_out[j]
        else:
            dk = hg_norm.shape[-1]
            heads = d // dk
            z = _in_proj("plain", rs, xs, mods, i, [(hg_w_in[j].astype(BF16), 0)], 5 * d, [BF16], tn_pref=1024)
            p = jax.nn.softmax(hg_lb_raw, axis=0)
            lower = (jnp.cumsum(p, axis=0)[i] - p[0]).reshape(1, d)
            scale = dk ** -0.5
            o_f = _gla_scan(rs, True, False, [(z, 0), (z, 1), (z, 3)], 0, lower, heads, dk, dk, scale)
            a = _gla_scan(rs, True, True, [(z, 0), (z, 2), (z, 3)], 0, lower, heads, dk, dk, scale,
                          final_args=(z, 4, o_f, hg_norm[j].reshape(1, dk)))
            w_out = hg_w_out[j]

        skip = 0
        if not with_ctx:
            skip = rs.ctx_rows
            assert skip % _Rows(batch, 0, n_lat).tile(256) == 0
            rs = _Rows(batch, 0, n_lat)
        xs, hp, logits_t = _out_proj(rs, a, w_out.astype(BF16), xs, mods, i,
                                     ln_g[i, 0].reshape(1, d), ln_b[i, 0].reshape(1, d),
                                     moe_w_router[i].T.astype(BF16), moe_b_router[i].reshape(-1, 1), alpha,
                                     skip_rows=skip)
        xs = _moe(rs, hp, logits_t, xs, mods, i, ln_g[i, 1].reshape(1, d), ln_b[i, 1].reshape(1, d),
                  moe_w_gu, moe_b_gu, moe_w_down, moe_b_down, alpha)
    return xs[rs.ctx_rows:].reshape(batch, n_lat, d)
```

```python
import functools

import jax
import jax.numpy as jnp
from jax import lax
from jax.experimental import pallas as pl
from jax.experimental.pallas import tpu as pltpu

F32 = jnp.float32
BF16 = jnp.bfloat16

GRID_W = 64
CHUNK = 64
TOP_K = 4
N_MOD = 6
MOD_ROWS = 8
GLA_NORMALIZER = 16.0
RG_C = 8.0
SWIGLU_LIMIT = 7.0
SWIGLU_ALPHA = 1.702
LN_EPS = 1e-5
RMS_EPS = 1e-6
MIB = 1024 * 1024


def _params(sem, vmem_mib):
    return pltpu.CompilerParams(dimension_semantics=sem, vmem_limit_bytes=vmem_mib * MIB)


def _dot(a, b):
    return jnp.dot(a, b, preferred_element_type=F32)


def _dot_nt(a, b):
    return lax.dot_general(a, b, (((1,), (1,)), ((), ())), preferred_element_type=F32)


def _dot_tn(a, b):
    return lax.dot_general(a, b, (((0,), (0,)), ((), ())), preferred_element_type=F32)


def _log_sigmoid(x):
    return jnp.minimum(x, 0.0) - jnp.log1p(jnp.exp(-jnp.abs(x)))


def _pick(n, pref):
    t = min(n, pref)
    while n % t:
        t //= 2
    return t


class _Rows:
    def __init__(self, batch, n_ctx, n_lat):
        self.batch, self.n_ctx, self.n_lat = batch, n_ctx, n_lat
        self.ctx_rows = batch * n_ctx
        self.rows = self.ctx_rows + batch * n_lat

    def tile(self, pref):
        t = min(pref, self.n_lat)
        while self.n_lat % t or self.ctx_rows % t:
            t //= 2
        return t

    def mod_map(self, layer, which, rt):
        def index(i, *_):
            r0 = i * rt
            row = jnp.where(r0 < self.ctx_rows, self.batch, (r0 - self.ctx_rows) // self.n_lat)
            return ((layer * MOD_ROWS + row) * N_MOD + which, 0, 0)
        return index


def _adaln_kernel(c_ref, w_ref, b_ref, o_ref):
    c = c_ref[...]
    a = (c * jax.nn.sigmoid(c)).astype(BF16)
    o_ref[0] = _dot(a, w_ref[0].astype(BF16)) + b_ref[0]


def _adaln(cc, ada_w, ada_b):
    depth, d, n = ada_w.shape
    tn = _pick(n, 1024)
    return pl.pallas_call(
        _adaln_kernel,
        grid=(depth, n // tn),
        in_specs=[pl.BlockSpec((MOD_ROWS, d), lambda l, j: (0, 0)),
                  pl.BlockSpec((1, d, tn), lambda l, j: (l, 0, j)),
                  pl.BlockSpec((1, 1, tn), lambda l, j: (l, 0, j))],
        out_specs=pl.BlockSpec((1, MOD_ROWS, tn), lambda l, j: (l, 0, j)),
        out_shape=jax.ShapeDtypeStruct((depth, MOD_ROWS, n), F32),
        compiler_params=_params(("arbitrary", "arbitrary"), 40),
        name="adaln",
    )(cc, ada_w, ada_b.reshape(depth, 1, n))


def _modulate(x_ref, sh_ref, sc_ref, h_ref):
    @pl.when(pl.program_id(1) == 0)
    def _():
        h_ref[...] = (x_ref[...] * (1.0 + sc_ref[0]) + sh_ref[0]).astype(BF16)


def _segment_pos(shape, ctx_tiles, seg_ctx, seg_lat):
    mask = jnp.where(pl.program_id(0) < ctx_tiles, seg_ctx - 1, seg_lat - 1)
    return lax.broadcasted_iota(jnp.int32, shape, 0) & mask, mask


def _shift_rows(p, pos, seg_last, offset):
    rt = p.shape[0]
    rolled = pltpu.roll(p, (-offset) % rt, 0)
    ok = (pos + offset >= 0) & (pos + offset <= seg_last)
    return jnp.where(ok, rolled, 0.0)


def _proj_kernel(x_ref, sh_ref, sc_ref, w_ref, o_ref, h_ref):
    _modulate(x_ref, sh_ref, sc_ref, h_ref)
    o_ref[...] = _dot(h_ref[...], w_ref[...]).astype(o_ref.dtype)


def _sconv_in_kernel(x_ref, sh_ref, sc_ref, wb_ref, wc_ref, wv_ref, cw_ref, o_ref, h_ref,
                     *, ctx_tiles, seg_ctx, seg_lat):
    _modulate(x_ref, sh_ref, sc_ref, h_ref)
    h = h_ref[...]
    bg = _dot(h, wb_ref[...])
    p = _dot(h, wc_ref[...]) * _dot(h, wv_ref[...])
    pos, last = _segment_pos(p.shape, ctx_tiles, seg_ctx, seg_lat)
    cw = cw_ref[...]
    conv = (cw[0:1] * _shift_rows(p, pos, last, -1) + cw[1:2] * p
            + cw[2:3] * _shift_rows(p, pos, last, 1))
    o_ref[...] = (bg * conv).astype(o_ref.dtype)


def _lru_in_kernel(x_ref, sh_ref, sc_ref, wy_ref, wx_ref, cw_ref, cb_ref, y_ref, xc_ref, h_ref,
                   *, ctx_tiles, seg_ctx, seg_lat):
    _modulate(x_ref, sh_ref, sc_ref, h_ref)
    h = h_ref[...]
    y_ref[...] = jax.nn.gelu(_dot(h, wy_ref[...])).astype(y_ref.dtype)
    xb = _dot(h, wx_ref[...])
    pos, last = _segment_pos(xb.shape, ctx_tiles, seg_ctx, seg_lat)
    cw = cw_ref[...]
    conv = (cw[0:1] * _shift_rows(xb, pos, last, -1) + cw[1:2] * xb
            + cw[2:3] * _shift_rows(xb, pos, last, 1) + cw[3:4] * _shift_rows(xb, pos, last, 2))
    xc_ref[...] = (conv + cb_ref[...]).astype(xc_ref.dtype)


def _in_proj(kind, rs, x, mods, layer, weights, n_out, out_dtypes, extra=(), tn_pref=512, rt_pref=1024):
    rows, d = x.shape
    rt = rs.tile(rt_pref)
    tn = _pick(n_out, tn_pref)
    nj = n_out // tn
    in_specs = [pl.BlockSpec((rt, d), lambda i, j: (i, 0)),
                pl.BlockSpec((1, 1, d), rs.mod_map(layer, 0, rt)),
                pl.BlockSpec((1, 1, d), rs.mod_map(layer, 1, rt))]
    args = [x, mods, mods]
    for w, off in weights:
        in_specs.append(pl.BlockSpec((d, tn), functools.partial(lambda i, j, o: (0, o * nj + j), o=off)))
        args.append(w)
    for e in extra:
        in_specs.append(pl.BlockSpec((e.shape[0], tn), lambda i, j: (0, j)))
        args.append(e)
    out_specs = [pl.BlockSpec((rt, tn), lambda i, j: (i, j)) for _ in out_dtypes]
    out_shape = [jax.ShapeDtypeStruct((rows, n_out), dt) for dt in out_dtypes]
    ctx_tiles = rs.ctx_rows // rt
    if kind == "plain":
        body = _proj_kernel
    elif kind == "sconv":
        body = functools.partial(_sconv_in_kernel, ctx_tiles=ctx_tiles, seg_ctx=rs.n_ctx, seg_lat=GRID_W)
    else:
        body = functools.partial(_lru_in_kernel, ctx_tiles=ctx_tiles, seg_ctx=rs.n_ctx, seg_lat=GRID_W)
    single = len(out_dtypes) == 1
    return pl.pallas_call(
        body,
        grid=(rows // rt, nj),
        in_specs=in_specs,
        out_specs=out_specs[0] if single else out_specs,
        out_shape=out_shape[0] if single else out_shape,
        scratch_shapes=[pltpu.VMEM((rt, d), BF16)],
        compiler_params=_params(("arbitrary", "arbitrary"), 52),
        name=kind + "_in_proj",
    )(*args)


def _gla_gate_kernel(x_ref, sh_ref, sc_ref, wr_ref, w2_ref, b2_ref, o_ref):
    h = (x_ref[...] * (1.0 + sc_ref[0]) + sh_ref[0]).astype(BF16)
    r = _dot(h, wr_ref[...]).astype(BF16)
    pre = _dot(r, w2_ref[...]) + b2_ref[...]
    o_ref[...] = _log_sigmoid(pre) * (1.0 / GLA_NORMALIZER)


def _gla_gates(rs, x, mods, layer, w_r, w2, b2):
    rows, d = x.shape
    rt = rs.tile(256)
    n = w2.shape[1]
    return pl.pallas_call(
        _gla_gate_kernel,
        grid=(rows // rt,),
        in_specs=[pl.BlockSpec((rt, d), lambda i: (i, 0)),
                  pl.BlockSpec((1, 1, d), rs.mod_map(layer, 0, rt)),
                  pl.BlockSpec((1, 1, d), rs.mod_map(layer, 1, rt)),
                  pl.BlockSpec(w_r.shape, lambda i: (0, 0)),
                  pl.BlockSpec(w2.shape, lambda i: (0, 0)),
                  pl.BlockSpec((1, n), lambda i: (0, 0))],
        out_specs=pl.BlockSpec((rt, n), lambda i: (i, 0)),
        out_shape=jax.ShapeDtypeStruct((rows, n), F32),
        compiler_params=_params(("arbitrary",), 40),
        name="gla_gates",
    )(x, mods, mods, w_r, w2, b2)


def _gla_scan_kernel(*refs, hgrn, reverse, final, heads, group, dk, dv, n_chunks, qscale):
    if hgrn:
        q_ref, f_ref, v_ref, lb_ref = refs[:4]
        rest = refs[4:]
    else:
        q_ref, k_ref, v_ref, g_ref = refs[:4]
        rest = refs[4:]
    if final:
        og_ref, of_ref, gain_ref, o_ref, st_ref = rest
    else:
        o_ref, st_ref = rest

    @pl.when(pl.program_id(1) == 0)
    def _():
        st_ref[...] = jnp.zeros_like(st_ref)

    tc = n_chunks * CHUNK
    row = lax.broadcasted_iota(jnp.int32, (tc, tc), 0)
    col = lax.broadcasted_iota(jnp.int32, (tc, tc), 1)
    shift = CHUNK.bit_length() - 1
    same_chunk = (row >> shift) == (col >> shift)
    tri = same_chunk & ((row <= col) if reverse else (row >= col))
    tri_b = jnp.where(tri, 1.0, 0.0).astype(BF16)
    order = range(n_chunks - 1, -1, -1) if reverse else range(n_chunks)

    def intra(h):
        kcols = pl.ds(pl.multiple_of(h * dk, dk), dk)
        vcols = pl.ds(pl.multiple_of(h * dv, dv), dv)
        qf = q_ref[:, kcols].astype(F32)
        if hgrn:
            lb = lb_ref[:, kcols]
            s = jax.nn.sigmoid(f_ref[:, kcols].astype(F32))
            qf = qf * jax.nn.sigmoid(qf)
            kf = (1.0 - lb) * (1.0 - s)
            g = jnp.log(lb + (1.0 - lb) * s)
        else:
            kf = k_ref[:, kcols].astype(F32)
            g = g_ref[:, kcols]
        v = v_ref[:, vcols]
        g_hi = g.astype(BF16)
        g_lo = (g - g_hi.astype(F32)).astype(BF16)
        b = _dot(tri_b, g_hi) + _dot(tri_b, g_lo)
        qd = (qf * (qscale * jnp.exp(b))).astype(BF16)
        kdf = kf * jnp.exp(-b)
        att = jnp.where(tri, _dot_nt(qd, kdf.astype(BF16)), 0.0).astype(BF16)
        return vcols, b, qd, kdf, v, _dot(att, v)

    def finish(vcols, outs):
        o = jnp.concatenate(outs, axis=0)
        if final:
            o = o + of_ref[:, vcols]
            o = o * lax.rsqrt(jnp.mean(o * o, axis=-1, keepdims=True) + RMS_EPS) * gain_ref[...]
            og = og_ref[:, vcols].astype(F32)
            o = o * (og * jax.nn.sigmoid(og))
        o_ref[:, vcols] = o.astype(o_ref.dtype)

    def head_group(i, carry):
        hs = [i * group + j for j in range(group)]
        parts = [intra(h) for h in hs]
        states = [st_ref[h] for h in hs]
        outs = [[None] * n_chunks for _ in hs]
        for c in order:
            rows = slice(c * CHUNK, (c + 1) * CHUNK)
            end = c * CHUNK if reverse else (c + 1) * CHUNK - 1
            for j, (_, b, qd, kdf, v, o_intra) in enumerate(parts):
                st = states[j]
                outs[j][c] = o_intra[rows] + _dot_nt(qd[rows], st.astype(BF16))
                dec = jnp.exp(b[end:end + 1])
                states[j] = st * dec + _dot_tn(v[rows], (kdf[rows] * dec).astype(BF16))
        for j, h in enumerate(hs):
            st_ref[h] = states[j]
            finish(parts[j][0], outs[j])
        return carry

    lax.fori_loop(0, heads // group, head_group, 0)


def _gla_scan(rs, hgrn, reverse, qkv, col_blocks, extra, heads, dk, dv, qscale, final_args=None):
    tc = min(256, rs.n_ctx)
    ctx_blocks, lat_blocks = rs.n_ctx // tc, rs.n_lat // tc
    steps = ctx_blocks + lat_blocks
    kd_w, vd_w = heads * dk, heads * dv

    def row_block(b, s):
        if reverse:
            ctx = b * ctx_blocks + (ctx_blocks - 1 - s)
            lat = rs.ctx_rows // tc + b * lat_blocks + (steps - 1 - s)
        else:
            ctx = b * ctx_blocks + s
            lat = rs.ctx_rows // tc + b * lat_blocks + (s - ctx_blocks)
        return jnp.where(s < ctx_blocks, ctx, lat)

    def spec(width, cb):
        return pl.BlockSpec((tc, width), lambda b, s: (row_block(b, s), cb))

    in_specs = [spec(kd_w, qkv[0][1]), spec(kd_w, qkv[1][1]), spec(vd_w, qkv[2][1])]
    args = [qkv[0][0], qkv[1][0], qkv[2][0]]
    if hgrn:
        in_specs.append(pl.BlockSpec((1, kd_w), lambda b, s: (0, 0)))
    else:
        in_specs.append(spec(kd_w, col_blocks))
    args.append(extra)
    final = final_args is not None
    if final:
        z, gcb, o_fwd, gain = final_args
        in_specs += [spec(vd_w, gcb), spec(vd_w, 0), pl.BlockSpec((1, dv), lambda b, s: (0, 0))]
        args += [z, o_fwd, gain]
    return pl.pallas_call(
        functools.partial(_gla_scan_kernel, hgrn=hgrn, reverse=reverse, final=final, heads=heads,
                          group=8 if heads % 8 == 0 else 4 if heads % 4 == 0 else 2 - heads % 2,
                          dk=dk, dv=dv, n_chunks=tc // CHUNK, qscale=qscale),
        grid=(rs.batch, steps),
        in_specs=in_specs,
        out_specs=spec(vd_w, 0),
        out_shape=jax.ShapeDtypeStruct((rs.rows, vd_w), BF16 if final else F32),
        scratch_shapes=[pltpu.VMEM((heads, dv, dk), F32)],
        compiler_params=_params(("arbitrary", "arbitrary"), 40),
        name=("hgrn" if hgrn else "gla") + ("_bwd" if reverse else "_fwd"),
    )(*args)


def _lru_scan_kernel(*refs, reverse, final, n_blk, bw):
    if final:
        xc_ref, wg_ref, bg_ref, ls_ref, hf_ref, y_ref, o_ref, a_ref, u_ref, h_ref = refs
    else:
        xc_ref, wg_ref, bg_ref, ls_ref, o_ref, a_ref, u_ref, h_ref = refs
    tc, w = xc_ref.shape
    groups = tc // 8

    @pl.when(pl.program_id(1) == 0)
    def _():
        h_ref[...] = jnp.zeros_like(h_ref)

    for n in range(n_blk):
        cols = slice(n * bw, (n + 1) * bw)
        xb = xc_ref[:, cols]
        gr = jax.nn.sigmoid(_dot(xb, wg_ref[0, n]) + bg_ref[0:1, cols])
        gi = jax.nn.sigmoid(_dot(xb, wg_ref[1, n]) + bg_ref[1:2, cols])
        log_a = RG_C * gr * ls_ref[:, cols]
        a = jnp.exp(log_a)
        a_ref[:, cols] = a
        u_ref[:, cols] = jnp.sqrt(-jnp.tanh(log_a) * (a * a + 1.0)) * gi * xb.astype(F32)

    sub = lax.broadcasted_iota(jnp.int32, (8, w), 0)

    def group(i, h):
        g = (groups - 1 - i) if reverse else i
        rows = pl.ds(pl.multiple_of(g * 8, 8), 8)
        a, u = a_ref[rows, :], u_ref[rows, :]
        for k in (1, 2, 4):
            if reverse:
                ok = sub < 8 - k
                a_s, u_s = pltpu.roll(a, 8 - k, 0), pltpu.roll(u, 8 - k, 0)
            else:
                ok = sub >= k
                a_s, u_s = pltpu.roll(a, k, 0), pltpu.roll(u, k, 0)
            u = jnp.where(ok, a * u_s + u, u)
            a = jnp.where(ok, a * a_s, a)
        out = a * h + u
        last = out[0:1] if reverse else out[7:8]
        res = out
        if final:
            res = (out + hf_ref[rows, :]) * y_ref[rows, :].astype(F32)
        o_ref[rows, :] = res.astype(o_ref.dtype)
        return jnp.broadcast_to(last, (8, w))

    h_ref[...] = lax.fori_loop(0, groups, group, h_ref[...])


def _lru_scan(rs, reverse, xc, w_gate, b_gate, log_sig_lam, final_args=None):
    tc = min(256, rs.n_ctx)
    ctx_blocks, lat_blocks = rs.n_ctx // tc, rs.n_lat // tc
    steps = ctx_blocks + lat_blocks
    w = xc.shape[1]
    n_blk, bw = w_gate.shape[1], w_gate.shape[2]

    def row_block(b, s):
        if reverse:
            ctx = b * ctx_blocks + (ctx_blocks - 1 - s)
            lat = rs.ctx_rows // tc + b * lat_blocks + (steps - 1 - s)
        else:
            ctx = b * ctx_blocks + s
            lat = rs.ctx_rows // tc + b * lat_blocks + (s - ctx_blocks)
        return jnp.where(s < ctx_blocks, ctx, lat)

    stream = pl.BlockSpec((tc, w), lambda b, s: (row_block(b, s), 0))
    in_specs = [stream,
                pl.BlockSpec(w_gate.shape, lambda b, s: (0, 0, 0, 0)),
                pl.BlockSpec((2, w), lambda b, s: (0, 0)),
                pl.BlockSpec((1, w), lambda b, s: (0, 0))]
    args = [xc, w_gate, b_gate, log_sig_lam]
    final = final_args is not None
    if final:
        in_specs += [stream, stream]
        args += list(final_args)
    return pl.pallas_call(
        functools.partial(_lru_scan_kernel, reverse=reverse, final=final, n_blk=n_blk, bw=bw),
        grid=(rs.batch, steps),
        in_specs=in_specs,
        out_specs=stream,
        out_shape=jax.ShapeDtypeStruct((rs.rows, w), BF16 if final else F32),
        scratch_shapes=[pltpu.VMEM((tc, w), F32), pltpu.VMEM((tc, w), F32), pltpu.VMEM((8, w), F32)],
        compiler_params=_params(("arbitrary", "arbitrary"), 40),
        name="lru_bwd" if reverse else "lru_fwd",
    )(*args)


def _layer_norm(z, g, b):
    mu = jnp.mean(z, axis=-1, keepdims=True)
    zc = z - mu
    var = jnp.mean(zc * zc, axis=-1, keepdims=True)
    return zc * lax.rsqrt(var + LN_EPS) * g + b


def _pack_words(v):
    half = v.shape[1] // 2
    vb = v.astype(BF16).astype(F32)
    lo = lax.bitcast_convert_type(vb[:, :half], jnp.uint32) >> 16
    hi = lax.bitcast_convert_type(vb[:, half:], jnp.uint32) & jnp.uint32(0xFFFF0000)
    return hi | lo


def _store_packed(ref, row0, n, words):
    s_rows = words.shape[1] // 128
    for s in range(s_rows):
        ref[pl.ds(row0 * s_rows + s, n, stride=s_rows), :] = words[:, s * 128:(s + 1) * 128]


def _load_packed(ref, row0, n, s_rows):
    lo, hi = [], []
    for s in range(s_rows):
        w = ref[pl.ds(row0 * s_rows + s, n, stride=s_rows), :]
        lo.append(lax.bitcast_convert_type(w << 16, F32))
        hi.append(lax.bitcast_convert_type(w & jnp.uint32(0xFFFF0000), F32))
    return lo, hi


def _out_proj_kernel(a_ref, w_ref, x_ref, gate_ref, sh_ref, sc_ref, lg_ref, lb_ref, wr_ref, br_ref,
                     xo_ref, hp_ref, lt_ref, *, alpha):
    y = _dot(a_ref[...], w_ref[...])
    x_new = _layer_norm(alpha * x_ref[...] + gate_ref[0] * y, lg_ref[...], lb_ref[...])
    xo_ref[...] = x_new
    h = x_new * (1.0 + sc_ref[0]) + sh_ref[0]
    _store_packed(hp_ref, 0, h.shape[0], _pack_words(h))
    lt_ref[...] = _dot_nt(wr_ref[...], h.astype(BF16)) + br_ref[...]


def _out_proj(rs, a, w_out, x, mods, layer, ln_g, ln_b, w_router_t, b_router, alpha, skip_rows=0):
    rows, d = rs.rows, x.shape[1]
    k = a.shape[1]
    n_e = w_router_t.shape[0]
    rt = rs.tile(256)
    skip = skip_rows // rt
    row = lambda i: (i, 0)
    row_in = lambda i: (i + skip, 0)
    const = lambda i: (0, 0)
    return pl.pallas_call(
        functools.partial(_out_proj_kernel, alpha=alpha),
        grid=(rows // rt,),
        in_specs=[pl.BlockSpec((rt, k), row_in),
                  pl.BlockSpec((k, d), const),
                  pl.BlockSpec((rt, d), row_in),
                  pl.BlockSpec((1, 1, d), rs.mod_map(layer, 2, rt)),
                  pl.BlockSpec((1, 1, d), rs.mod_map(layer, 3, rt)),
                  pl.BlockSpec((1, 1, d), rs.mod_map(layer, 4, rt)),
                  pl.BlockSpec((1, d), const),
                  pl.BlockSpec((1, d), const),
                  pl.BlockSpec((n_e, d), const),
                  pl.BlockSpec((n_e, 1), const)],
        out_specs=[pl.BlockSpec((rt, d), row),
                   pl.BlockSpec((rt * (d // 256), 128), row),
                   pl.BlockSpec((n_e, rt), lambda i: (0, i))],
        out_shape=[jax.ShapeDtypeStruct((rows, d), F32),
                   jax.ShapeDtypeStruct((rows * (d // 256), 128), jnp.uint32),
                   jax.ShapeDtypeStruct((n_e, rows), F32)],
        compiler_params=_params(("arbitrary",), 48),
        name="out_proj_ln",
    )(a, w_out, x, mods, mods, mods, ln_g, ln_b, w_router_t, b_router)


def _route_kernel(lt_ref, e_ref, g_ref, r_ref, cnt_ref, carry_ref):
    n_e, tr = lt_ref.shape

    @pl.when(pl.program_id(0) == 0)
    def _():
        carry_ref[...] = jnp.zeros_like(carry_ref)

    lg = lt_ref[...]
    eid = lax.broadcasted_iota(jnp.int32, (n_e, tr), 0)
    vals, idxs, sels = [], [], []
    for _ in range(TOP_K):
        m = jnp.max(lg, axis=0, keepdims=True)
        idx = jnp.min(jnp.where(lg == m, eid, n_e), axis=0, keepdims=True)
        sel = eid == idx
        vals.append(m)
        idxs.append(idx)
        sels.append(sel)
        lg = jnp.where(sel, -jnp.inf, lg)
    ex = [jnp.exp(v - vals[0]) for v in vals]
    denom = ex[0] + ex[1] + ex[2] + ex[3]
    chosen = sels[0] | sels[1] | sels[2] | sels[3]
    member = jnp.where(chosen, 1.0, 0.0)
    srow = lax.broadcasted_iota(jnp.int32, (tr, tr), 0)
    scol = lax.broadcasted_iota(jnp.int32, (tr, tr), 1)
    before = jnp.where(srow < scol, 1.0, 0.0).astype(BF16)
    prefix = _dot(member.astype(BF16), before) + carry_ref[...]
    for k in range(TOP_K):
        e_ref[k:k + 1, :] = idxs[k]
        g_ref[k:k + 1, :] = ex[k] / denom
        r_ref[k:k + 1, :] = jnp.sum(jnp.where(sels[k], prefix, 0.0), axis=0, keepdims=True).astype(jnp.int32)
    carry_ref[...] = carry_ref[...] + jnp.sum(member, axis=1, keepdims=True)
    cnt_ref[...] = carry_ref[...]


def _route(logits_t):
    n_e, rows = logits_t.shape
    tr = _pick(rows, 512)
    tok = pl.BlockSpec((TOP_K, tr), lambda i: (0, i))
    return pl.pallas_call(
        _route_kernel,
        grid=(rows // tr,),
        in_specs=[pl.BlockSpec((n_e, tr), lambda i: (0, i))],
        out_specs=[tok, tok, tok, pl.BlockSpec((n_e, 1), lambda i: (0, 0))],
        out_shape=[jax.ShapeDtypeStruct((TOP_K, rows), jnp.int32),
                   jax.ShapeDtypeStruct((TOP_K, rows), F32),
                   jax.ShapeDtypeStruct((TOP_K, rows), jnp.int32),
                   jax.ShapeDtypeStruct((n_e, 1), F32)],
        scratch_shapes=[pltpu.VMEM((n_e, 1), F32)],
        compiler_params=_params(("arbitrary",), 32),
        name="route",
    )(logits_t)


def _row_slab(ref, row, s_rows):
    return ref.at[pl.ds(pl.multiple_of(row * s_rows, s_rows), s_rows), :]


def _dispatch_kernel(pe_ref, pd_ref, dest_ref, hp_ref, xg_ref, idx_ref, zero_ref, sem, idx_sem,
                     *, rt, bm, s_rows, n_e, n_blocks):
    @pl.when(pl.program_id(0) == 0)
    def _():
        zero_ref[...] = jnp.zeros_like(zero_ref)
        def fill(last_row):
            first = pl.multiple_of((last_row - bm) * s_rows, s_rows)
            return pltpu.make_async_copy(zero_ref, xg_ref.at[pl.ds(first, bm * s_rows), :], sem)

        def start(e, c):
            @pl.when(pd_ref[e] > 0)
            def _():
                fill(pe_ref[e]).start()
            return c

        def wait(e, c):
            @pl.when(pd_ref[e] > 0)
            def _():
                fill(pe_ref[e]).wait()
            return c

        lax.fori_loop(0, n_e, start, 0)
        lax.fori_loop(0, n_e, wait, 0)
        n_used = pe_ref[n_e - 1] // bm
        lax.fori_loop(n_used, n_blocks, lambda j, c: (fill((j + 1) * bm).start(), c)[1], 0)
        lax.fori_loop(n_used, n_blocks, lambda j, c: (fill((j + 1) * bm).wait(), c)[1], 0)

    load = pltpu.make_async_copy(dest_ref, idx_ref, idx_sem)
    load.start()
    load.wait()

    def token(t, c):
        src = _row_slab(hp_ref, t, s_rows)
        for k in range(TOP_K):
            pltpu.make_async_copy(src, _row_slab(xg_ref, idx_ref[k, t], s_rows), sem).start(priority=k % 2)
        return c

    lax.fori_loop(0, rt, token, 0)
    for _ in range(TOP_K):
        pltpu.make_async_copy(hp_ref, xg_ref.at[pl.ds(0, rt * s_rows), :], sem).wait()


def _dispatch(hp, dest, pad_end, padded, n_rows, bm, rt):
    s_rows = hp.shape[0] // dest.shape[1]
    rows = dest.shape[1]
    n_e = pad_end.shape[0]
    grid_spec = pltpu.PrefetchScalarGridSpec(
        num_scalar_prefetch=2,
        grid=(rows // rt,),
        in_specs=[pl.BlockSpec((TOP_K, rt), lambda i, pe, pd: (0, i)),
                  pl.BlockSpec((rt * s_rows, 128), lambda i, pe, pd: (i, 0))],
        out_specs=pl.BlockSpec(memory_space=pl.ANY),
        scratch_shapes=[pltpu.SMEM((TOP_K, rt), jnp.int32),
                        pltpu.VMEM((bm * s_rows, 128), jnp.uint32),
                        pltpu.SemaphoreType.DMA,
                        pltpu.SemaphoreType.DMA],
    )
    return pl.pallas_call(
        functools.partial(_dispatch_kernel, rt=rt, bm=bm, s_rows=s_rows, n_e=n_e, n_blocks=n_rows // bm),
        grid_spec=grid_spec,
        out_shape=jax.ShapeDtypeStruct((n_rows * s_rows, 128), jnp.uint32),
        compiler_params=_params(("arbitrary",), 32),
        name="dispatch",
    )(pad_end, padded, dest, hp)


N_STAGE = 8


def _expert_kernel(be_ref, nb_ref, nx_ref, c0_ref, c1_ref, par_ref,
                   x_ref, wgu_hbm, bgu_ref, wd_hbm, bd_ref, o_ref,
                   wgu_s, wd_s, xb_s, st_ref, sem,
                   *, layer, d_ff, bm, s_rows, wrows, n_gu, n_d):
    i = pl.program_id(0)
    n_chunks = n_gu + n_d

    ff2, d = wgu_s.shape[2], wd_s.shape[2]

    def gu_copy(e, c):
        rows = pl.ds(pl.multiple_of(c * wrows, wrows), wrows)
        slot = c % N_STAGE
        return pltpu.make_async_copy(wgu_hbm.at[layer, e, rows, :], st_ref.at[slot, :, pl.ds(0, ff2)], sem.at[slot])

    def down_copy(e, c):
        rows = pl.ds(pl.multiple_of((c - n_gu) * wrows, wrows), wrows)
        slot = c % N_STAGE
        return pltpu.make_async_copy(wd_hbm.at[layer, e, rows, :], st_ref.at[slot, :, pl.ds(0, d)], sem.at[slot])

    def start(e, c):
        @pl.when(c < n_gu)
        def _():
            gu_copy(e, c).start()

        @pl.when((c >= n_gu) & (c < n_chunks))
        def _():
            down_copy(e, c).start()

    def land(e, c, buf):
        @pl.when(c < n_gu)
        def _():
            gu_copy(e, c).wait()
            rows = pl.ds(pl.multiple_of(c * wrows, wrows), wrows)
            wgu_s[buf, rows, :] = st_ref[c % N_STAGE, :, 0:ff2].astype(BF16)

        @pl.when(c >= n_gu)
        def _():
            down_copy(e, c).wait()
            rows = pl.ds(pl.multiple_of((c - n_gu) * wrows, wrows), wrows)
            wd_s[buf, rows, :] = st_ref[c % N_STAGE, :, 0:d].astype(BF16)

    def prime(e):
        for c in range(N_STAGE - 1):
            start(e, jnp.int32(c))

    def stream(e, lo, hi, buf):
        def body(c, carry):
            start(e, c + (N_STAGE - 1))
            land(e, c, buf)
            return carry
        lax.fori_loop(lo, hi, body, 0)

    @pl.when(i < nb_ref[0])
    def _():
        e, nxt, cur = be_ref[i], nx_ref[i], par_ref[i]
        first = (i == 0) | (e != be_ref[jnp.maximum(i - 1, 0)])

        @pl.when(i == 0)
        def _():
            prime(e)
            stream(e, 0, n_chunks, cur)

        @pl.when(first & (nxt >= 0))
        def _():
            prime(nxt)

        @pl.when(nxt >= 0)
        def _():
            stream(nxt, c0_ref[i], c1_ref[i], 1 - cur)

        half = s_rows * 128
        lo, hi = _load_packed(x_ref, 0, bm, s_rows)
        for s in range(s_rows):
            xb_s[:, s * 128:(s + 1) * 128] = lo[s].astype(BF16)
            xb_s[:, half + s * 128:half + (s + 1) * 128] = hi[s].astype(BF16)
        gu = _dot(xb_s[...], wgu_s[cur]) + bgu_ref[0]
        g = jnp.minimum(gu[:, :d_ff], SWIGLU_LIMIT)
        u = jnp.clip(gu[:, d_ff:], -SWIGLU_LIMIT, SWIGLU_LIMIT)
        act = (g * jax.nn.sigmoid(SWIGLU_ALPHA * g) * (u + 1.0)).astype(BF16)
        y = _dot(act, wd_s[cur]) + bd_ref[0]
        _store_packed(o_ref, 0, bm, _pack_words(y))

    @pl.when(i >= nb_ref[0])
    def _():
        o_ref[...] = jnp.zeros_like(o_ref)


def _experts(xg, pad_end, padded, layer, w_gu, b_gu, w_down, b_down, bm):
    depth, n_e, d, ff2 = w_gu.shape
    d_ff = ff2 // 2
    s_rows = d // 256
    n_blocks = xg.shape[0] // (bm * s_rows)
    wrows = _pick(d_ff, 128)
    n_gu, n_d = d // wrows, d_ff // wrows
    n_chunks = n_gu + n_d

    experts = jnp.arange(n_e, dtype=jnp.int32)
    block_start = jnp.arange(n_blocks, dtype=jnp.int32) * bm
    block_e = jnp.minimum(jnp.sum(pad_end[None, :] <= block_start[:, None], axis=1), n_e - 1).astype(jnp.int32)
    n_used = (pad_end[-1:] // bm).astype(jnp.int32)
    onehot = block_e[:, None] == experts[None, :]
    pick = lambda table: jnp.sum(jnp.where(onehot, table[None, :], 0), axis=1)
    blocks_of = padded // bm
    local = jnp.arange(n_blocks, dtype=jnp.int32) - pick((pad_end - padded) // bm)
    nblk = pick(blocks_of)
    later_blocks = jnp.maximum(nblk - 1, 1)
    share = lambda l: jnp.where(nblk > 1, (n_chunks * jnp.maximum(l, 0)) // later_blocks, n_chunks * (l + 1))
    c0, c1 = share(local - 1), share(local)
    live = blocks_of > 0
    later = (experts[None, :] > experts[:, None]) & live[None, :]
    next_of = jnp.min(jnp.where(later, experts[None, :], n_e), axis=1)
    next_of = jnp.where(next_of == n_e, -1, next_of)
    ordinal = jnp.cumsum(live.astype(jnp.int32)) - live.astype(jnp.int32)
    tables = [block_e, n_used, pick(next_of), c0, c1, pick(ordinal) % 2]
    tables = [t.astype(jnp.int32) for t in tables]

    used = lambda i, nb: jnp.minimum(i, nb[0] - 1)
    bias = lambda i, be, nb, *_: (layer, be[used(i, nb)], 0, 0)
    grid_spec = pltpu.PrefetchScalarGridSpec(
        num_scalar_prefetch=len(tables),
        grid=(n_blocks,),
        in_specs=[pl.BlockSpec((bm * s_rows, 128), lambda i, be, nb, *_: (used(i, nb), 0)),
                  pl.BlockSpec(memory_space=pl.ANY),
                  pl.BlockSpec((None, 1, 1, ff2), bias),
                  pl.BlockSpec(memory_space=pl.ANY),
                  pl.BlockSpec((None, 1, 1, d), bias)],
        out_specs=pl.BlockSpec((bm * s_rows, 128), lambda i, *_: (i, 0)),
        scratch_shapes=[pltpu.VMEM((2, d, ff2), BF16), pltpu.VMEM((2, d_ff, d), BF16), pltpu.VMEM((bm, d), BF16),
                        pltpu.VMEM((N_STAGE, wrows, max(ff2, d)), F32),
                        pltpu.SemaphoreType.DMA((N_STAGE,))],
    )
    return pl.pallas_call(
        functools.partial(_expert_kernel, layer=layer, d_ff=d_ff, bm=bm, s_rows=s_rows,
                          wrows=wrows, n_gu=n_gu, n_d=n_d),
        grid_spec=grid_spec,
        out_shape=jax.ShapeDtypeStruct(xg.shape, jnp.uint32),
        compiler_params=_params(("arbitrary",), 58),
        name="experts",
    )(*tables, xg, w_gu, b_gu.reshape(depth, n_e, 1, ff2), w_down, b_down.reshape(depth, n_e, 1, d))


COMBINE_SUB = 32


def _combine_kernel(dcur_ref, dnext_ref, g_ref, x_ref, gate_ref, lg_ref, lb_ref, y_ref, o_ref,
                    idx_ref, buf_ref, sem, idx_sem, *, alpha, rt, s_rows, n_tiles):
    i = pl.program_id(0)
    cur, nxt = i % 2, (i + 1) % 2
    sub = min(COMBINE_SUB, rt)

    def gather(slot, t, k):
        dst = buf_ref.at[slot, pl.ds(pl.multiple_of((k * rt + t) * s_rows, s_rows), s_rows), :]
        return pltpu.make_async_copy(_row_slab(y_ref, idx_ref[slot, k, t], s_rows), dst, sem.at[slot])

    def retire(slot):
        for k in range(TOP_K):
            part = pl.ds(k * rt * s_rows, rt * s_rows)
            pltpu.make_async_copy(y_ref.at[pl.ds(0, rt * s_rows), :], buf_ref.at[slot, part, :], sem.at[slot]).wait()

    def load_indices(src_ref, slot):
        load = pltpu.make_async_copy(src_ref, idx_ref.at[slot], idx_sem)
        load.start()
        load.wait()

    @pl.when(i == 0)
    def _():
        load_indices(dcur_ref, 0)

        def token(t, c):
            for k in range(TOP_K):
                gather(0, t, k).start(priority=k % 2)
            return c
        lax.fori_loop(0, rt, token, 0)

    load_indices(dnext_ref, nxt)
    retire(cur)

    def rows_step(j, carry):
        r0 = pl.multiple_of(j * sub, sub)
        g = g_ref[pl.ds(r0, sub), :]
        lo_parts, hi_parts = [], []
        for s in range(s_rows):
            for tt in range(s * sub // s_rows, (s + 1) * sub // s_rows):
                for k in range(TOP_K):
                    gather(nxt, r0 + tt, k).start(priority=k % 2)
            lo = hi = None
            for k in range(TOP_K):
                w = buf_ref[cur, pl.ds((k * rt + r0) * s_rows + s, sub, stride=s_rows), :]
                gk = g[:, k:k + 1]
                lo_k = gk * lax.bitcast_convert_type(w << 16, F32)
                hi_k = gk * lax.bitcast_convert_type(w & jnp.uint32(0xFFFF0000), F32)
                lo = lo_k if lo is None else lo + lo_k
                hi = hi_k if hi is None else hi + hi_k
            lo_parts.append(lo)
            hi_parts.append(hi)
        acc = jnp.concatenate(lo_parts + hi_parts, axis=1)
        z = alpha * x_ref[pl.ds(r0, sub), :] + gate_ref[0] * acc
        o_ref[pl.ds(r0, sub), :] = _layer_norm(z, lg_ref[...], lb_ref[...])
        return carry

    lax.fori_loop(0, rt // sub, rows_step, 0)

    @pl.when(i == n_tiles - 1)
    def _():
        retire(nxt)


def _combine(rs, y, dest, gates, x, mods, layer, ln_g, ln_b, alpha):
    rows, d = x.shape
    s_rows = d // 256
    rt = rs.tile(256)
    n_tiles = rows // rt
    row = lambda i: (i, 0)
    const = lambda i: (0, 0)
    return pl.pallas_call(
        functools.partial(_combine_kernel, alpha=alpha, rt=rt, s_rows=s_rows, n_tiles=n_tiles),
        grid=(n_tiles,),
        in_specs=[pl.BlockSpec((TOP_K, rt), lambda i: (0, i)),
                  pl.BlockSpec((TOP_K, rt), lambda i: (0, jnp.minimum(i + 1, n_tiles - 1))),
                  pl.BlockSpec((rt, TOP_K), row),
                  pl.BlockSpec((rt, d), row),
                  pl.BlockSpec((1, 1, d), rs.mod_map(layer, 5, rt)),
                  pl.BlockSpec((1, d), const),
                  pl.BlockSpec((1, d), const),
                  pl.BlockSpec(memory_space=pl.ANY)],
        out_specs=pl.BlockSpec((rt, d), row),
        out_shape=jax.ShapeDtypeStruct((rows, d), F32),
        scratch_shapes=[pltpu.SMEM((2, TOP_K, rt), jnp.int32),
                        pltpu.VMEM((2, TOP_K * rt * s_rows, 128), jnp.uint32),
                        pltpu.SemaphoreType.DMA((2,)),
                        pltpu.SemaphoreType.DMA],
        compiler_params=_params(("arbitrary",), 40),
        name="combine_ln",
    )(dest, dest, gates, x, mods, ln_g, ln_b, y)


def _moe(rs, hp, logits_t, x, mods, layer, ln_g, ln_b, w_gu, b_gu, w_down, b_down, alpha):
    rows = x.shape[0]
    n_e = w_gu.shape[1]
    bm = 512 if rows * TOP_K >= 512 * n_e * 4 else 128
    e_t, g_t, r_t, cnt = _route(logits_t)
    counts = cnt[:, 0].astype(jnp.int32)
    padded = (counts + bm - 1) // bm * bm
    pad_end = jnp.cumsum(padded)
    first = pad_end - padded
    experts = jnp.arange(n_e, dtype=jnp.int32)
    first_of = jnp.sum(jnp.where(e_t[None] == experts[:, None, None], first[:, None, None], 0), axis=0)
    dest = first_of + r_t
    n_blocks = -(-(rows * TOP_K + n_e * (bm - 1)) // bm)
    xg = _dispatch(hp, dest, pad_end, padded, n_blocks * bm, bm, rs.tile(256))
    y = _experts(xg, pad_end, padded, layer, w_gu, b_gu, w_down, b_down, bm)
    return _combine(rs, y, dest, g_t.T, x, mods, layer, ln_g, ln_b, alpha)


def kernel(x, c, ctx, c_ctx, ada_w, ada_b, ln_g, ln_b, sc_w_in, sc_conv, sc_w_out, gla_w_in, gla_w_gate2, gla_b_gate, gla_norm, gla_w_out, lru_w_in, lru_conv, lru_conv_b, lru_w_gate, lru_b_gate, lru_lambda, lru_w_out, hg_w_in, hg_lb_raw, hg_norm, hg_w_out, moe_w_router, moe_b_router, moe_w_gu, moe_b_gu, moe_w_down, moe_b_down):
    batch, n_lat, d = x.shape
    n_ctx = ctx.shape[1]
    depth = ada_w.shape[0]
    alpha = (2 * depth) ** 0.25
    assert GRID_W & (GRID_W - 1) == 0 and n_ctx & (n_ctx - 1) == 0 and batch < MOD_ROWS
    rs = _Rows(batch, n_ctx, n_lat)

    cc = jnp.zeros((MOD_ROWS, d), F32).at[:batch].set(c).at[batch].set(c_ctx)
    mods = _adaln(cc, ada_w, ada_b).reshape(depth * MOD_ROWS * N_MOD, 1, d)
    xs = jnp.concatenate([ctx.reshape(-1, d), x.reshape(-1, d)], axis=0)

    for i in range(depth):
        kind, j = i % 4, i // 4
        with_ctx = i < depth - 1
        if kind == 0:
            w_in = sc_w_in[j].astype(BF16)
            a = _in_proj("sconv", rs, xs, mods, i, [(w_in, 0), (w_in, 1), (w_in, 2)], d, [BF16],
                         extra=[sc_conv[j]])
            w_out = sc_w_out[j]
        elif kind == 1:
            dv = gla_norm.shape[-1]
            dk = dv // 2
            heads = gla_w_gate2.shape[-1] // dk
            kd_w, vd_w = heads * dk, heads * dv
            n_main = 2 * kd_w + 2 * vd_w
            rank = gla_w_gate2.shape[2]
            w_in = gla_w_in[j]
            z = _in_proj("plain", rs, xs, mods, i, [(w_in[:, :n_main].astype(BF16), 0)], n_main, [BF16],
                         tn_pref=1024)
            w2 = jnp.zeros((2 * rank, 2 * kd_w), F32)
            w2 = w2.at[:rank, :kd_w].set(gla_w_gate2[j, 0]).at[rank:, kd_w:].set(gla_w_gate2[j, 1])
            gates = _gla_gates(rs, xs, mods, i, w_in[:, n_main:].astype(BF16), w2.astype(BF16),
                               gla_b_gate[j].reshape(1, 2 * kd_w))
            qkv = [(z, 0), (z, 1), (z, kd_w * 2 // vd_w)]
            scale = dk ** -0.5
            o_f = _gla_scan(rs, False, False, qkv, 0, gates, heads, dk, dv, scale)
            a = _gla_scan(rs, False, True, qkv, 1, gates, heads, dk, dv, scale,
                          final_args=(z, (2 * kd_w + vd_w) // vd_w, o_f, gla_norm[j].reshape(1, dv)))
            w_out = gla_w_out[j]
        elif kind == 2:
            w = lru_w_in.shape[-1] // 2
            w_in = lru_w_in[j].astype(BF16)
            y, xc = _in_proj("lru", rs, xs, mods, i, [(w_in, 0), (w_in, 1)], w, [BF16, BF16],
                             extra=[lru_conv[j], lru_conv_b[j].reshape(1, w)])
            log_sig = _log_sigmoid(lru_lambda[j])
            wg, bgate = lru_w_gate[j].astype(BF16), lru_b_gate[j]
            h_f = _lru_scan(rs, False, xc, wg[0], bgate[0], log_sig[0:1])
            a = _lru_scan(rs, True, xc, wg[1], bgate[1], log_sig[1:2], final_args=(h_f, y))
            w_out = lru_w_out[j]
        else:
            dk = hg_norm.shape[-1]
            heads = d // dk
            z = _in_proj("plain", rs, xs, mods, i, [(hg_w_in[j].astype(BF16), 0)], 5 * d, [BF16], tn_pref=1024)
            p = jax.nn.softmax(hg_lb_raw, axis=0)
            lower = (jnp.cumsum(p, axis=0)[i] - p[0]).reshape(1, d)
            scale = dk ** -0.5
            o_f = _gla_scan(rs, True, False, [(z, 0), (z, 1), (z, 3)], 0, lower, heads, dk, dk, scale)
            a = _gla_scan(rs, True, True, [(z, 0), (z, 2), (z, 3)], 0, lower, heads, dk, dk, scale,
                          final_args=(z, 4, o_f, hg_norm[j].reshape(1, dk)))
            w_out = hg_w_out[j]

        skip = 0
        if not with_ctx:
            skip = rs.ctx_rows
            assert skip % _Rows(batch, 0, n_lat).tile(256) == 0
            rs = _Rows(batch, 0, n_lat)
        xs, hp, logits_t = _out_proj(rs, a, w_out.astype(BF16), xs, mods, i,
                                     ln_g[i, 0].reshape(1, d), ln_b[i, 0].reshape(1, d),
                                     moe_w_router[i].T.astype(BF16), moe_b_router[i].reshape(-1, 1), alpha,
                                     skip_rows=skip)
        xs = _moe(rs, hp, logits_t, xs, mods, i, ln_g[i, 1].reshape(1, d), ln_b[i, 1].reshape(1, d),
                  moe_w_gu, moe_b_gu, moe_w_down, moe_b_down, alpha)
    return xs[rs.ctx_rows:].reshape(batch, n_lat, d)
```

```python
import functools

import jax
import jax.numpy as jnp
from jax import lax
from jax.experimental import pallas as pl
from jax.experimental.pallas import tpu as pltpu

F32 = jnp.float32
BF16 = jnp.bfloat16

GRID_W = 64
CHUNK = 64
TOP_K = 4
N_MOD = 6
MOD_ROWS = 8
GLA_NORMALIZER = 16.0
RG_C = 8.0
SWIGLU_LIMIT = 7.0
SWIGLU_ALPHA = 1.702
LN_EPS = 1e-5
RMS_EPS = 1e-6
MIB = 1024 * 1024


def _params(sem, vmem_mib):
    return pltpu.CompilerParams(dimension_semantics=sem, vmem_limit_bytes=vmem_mib * MIB)


def _dot(a, b):
    return jnp.dot(a, b, preferred_element_type=F32)


def _dot_nt(a, b):
    return lax.dot_general(a, b, (((1,), (1,)), ((), ())), preferred_element_type=F32)


def _dot_tn(a, b):
    return lax.dot_general(a, b, (((0,), (0,)), ((), ())), preferred_element_type=F32)


def _log_sigmoid(x):
    return jnp.minimum(x, 0.0) - jnp.log1p(jnp.exp(-jnp.abs(x)))


def _pick(n, pref):
    t = min(n, pref)
    while n % t:
        t //= 2
    return t


class _Rows:
    def __init__(self, batch, n_ctx, n_lat):
        self.batch, self.n_ctx, self.n_lat = batch, n_ctx, n_lat
        self.ctx_rows = batch * n_ctx
        self.rows = self.ctx_rows + batch * n_lat

    def tile(self, pref):
        t = min(pref, self.n_lat)
        while self.n_lat % t or self.ctx_rows % t:
            t //= 2
        return t

    def mod_map(self, layer, which, rt):
        def index(i, *_):
            r0 = i * rt
            row = jnp.where(r0 < self.ctx_rows, self.batch, (r0 - self.ctx_rows) // self.n_lat)
            return ((layer * MOD_ROWS + row) * N_MOD + which, 0, 0)
        return index


def _adaln_kernel(c_ref, w_ref, b_ref, o_ref):
    c = c_ref[...]
    a = (c * jax.nn.sigmoid(c)).astype(BF16)
    o_ref[0] = _dot(a, w_ref[0].astype(BF16)) + b_ref[0]


def _adaln(cc, ada_w, ada_b):
    depth, d, n = ada_w.shape
    tn = _pick(n, 1024)
    return pl.pallas_call(
        _adaln_kernel,
        grid=(depth, n // tn),
        in_specs=[pl.BlockSpec((MOD_ROWS, d), lambda l, j: (0, 0)),
                  pl.BlockSpec((1, d, tn), lambda l, j: (l, 0, j)),
                  pl.BlockSpec((1, 1, tn), lambda l, j: (l, 0, j))],
        out_specs=pl.BlockSpec((1, MOD_ROWS, tn), lambda l, j: (l, 0, j)),
        out_shape=jax.ShapeDtypeStruct((depth, MOD_ROWS, n), F32),
        compiler_params=_params(("arbitrary", "arbitrary"), 40),
        name="adaln",
    )(cc, ada_w, ada_b.reshape(depth, 1, n))


def _modulate(x_ref, sh_ref, sc_ref, h_ref):
    @pl.when(pl.program_id(1) == 0)
    def _():
        h_ref[...] = (x_ref[...] * (1.0 + sc_ref[0]) + sh_ref[0]).astype(BF16)


def _segment_pos(shape, ctx_tiles, seg_ctx, seg_lat):
    mask = jnp.where(pl.program_id(0) < ctx_tiles, seg_ctx - 1, seg_lat - 1)
    return lax.broadcasted_iota(jnp.int32, shape, 0) & mask, mask


def _shift_rows(p, pos, seg_last, offset):
    rt = p.shape[0]
    rolled = pltpu.roll(p, (-offset) % rt, 0)
    ok = (pos + offset >= 0) & (pos + offset <= seg_last)
    return jnp.where(ok, rolled, 0.0)


def _col_parts(width, part=256):
    part = min(part, width)
    return [slice(c, c + part) for c in range(0, width, part)]


def _proj_kernel(x_ref, sh_ref, sc_ref, w_ref, o_ref, h_ref):
    _modulate(x_ref, sh_ref, sc_ref, h_ref)
    o_ref[...] = _dot(h_ref[...], w_ref[...]).astype(o_ref.dtype)


def _sconv_in_kernel(x_ref, sh_ref, sc_ref, wb_ref, wc_ref, wv_ref, cw_ref, o_ref, h_ref,
                     *, ctx_tiles, seg_ctx, seg_lat):
    _modulate(x_ref, sh_ref, sc_ref, h_ref)
    h = h_ref[...]
    for cols in _col_parts(o_ref.shape[1]):
        bg = _dot(h, wb_ref[:, cols])
        p = _dot(h, wc_ref[:, cols]) * _dot(h, wv_ref[:, cols])
        pos, last = _segment_pos(p.shape, ctx_tiles, seg_ctx, seg_lat)
        cw = cw_ref[:, cols]
        conv = (cw[0:1] * _shift_rows(p, pos, last, -1) + cw[1:2] * p
                + cw[2:3] * _shift_rows(p, pos, last, 1))
        o_ref[:, cols] = (bg * conv).astype(o_ref.dtype)


def _lru_in_kernel(x_ref, sh_ref, sc_ref, wy_ref, wx_ref, cw_ref, cb_ref, y_ref, xc_ref, h_ref,
                   *, ctx_tiles, seg_ctx, seg_lat):
    _modulate(x_ref, sh_ref, sc_ref, h_ref)
    h = h_ref[...]
    for cols in _col_parts(y_ref.shape[1]):
        y_ref[:, cols] = jax.nn.gelu(_dot(h, wy_ref[:, cols])).astype(y_ref.dtype)
        xb = _dot(h, wx_ref[:, cols])
        pos, last = _segment_pos(xb.shape, ctx_tiles, seg_ctx, seg_lat)
        cw = cw_ref[:, cols]
        conv = (cw[0:1] * _shift_rows(xb, pos, last, -1) + cw[1:2] * xb
                + cw[2:3] * _shift_rows(xb, pos, last, 1) + cw[3:4] * _shift_rows(xb, pos, last, 2))
        xc_ref[:, cols] = (conv + cb_ref[:, cols]).astype(xc_ref.dtype)


def _in_proj(kind, rs, x, mods, layer, weights, n_out, out_dtypes, extra=(), tn_pref=512, rt_pref=1024):
    rows, d = x.shape
    rt = rs.tile(rt_pref)
    tn = _pick(n_out, tn_pref)
    nj = n_out // tn
    in_specs = [pl.BlockSpec((rt, d), lambda i, j: (i, 0)),
                pl.BlockSpec((1, 1, d), rs.mod_map(layer, 0, rt)),
                pl.BlockSpec((1, 1, d), rs.mod_map(layer, 1, rt))]
    args = [x, mods, mods]
    for w, off in weights:
        in_specs.append(pl.BlockSpec((d, tn), functools.partial(lambda i, j, o: (0, o * nj + j), o=off)))
        args.append(w)
    for e in extra:
        in_specs.append(pl.BlockSpec((e.shape[0], tn), lambda i, j: (0, j)))
        args.append(e)
    out_specs = [pl.BlockSpec((rt, tn), lambda i, j: (i, j)) for _ in out_dtypes]
    out_shape = [jax.ShapeDtypeStruct((rows, n_out), dt) for dt in out_dtypes]
    ctx_tiles = rs.ctx_rows // rt
    if kind == "plain":
        body = _proj_kernel
    elif kind == "sconv":
        body = functools.partial(_sconv_in_kernel, ctx_tiles=ctx_tiles, seg_ctx=rs.n_ctx, seg_lat=GRID_W)
    else:
        body = functools.partial(_lru_in_kernel, ctx_tiles=ctx_tiles, seg_ctx=rs.n_ctx, seg_lat=GRID_W)
    single = len(out_dtypes) == 1
    return pl.pallas_call(
        body,
        grid=(rows // rt, nj),
        in_specs=in_specs,
        out_specs=out_specs[0] if single else out_specs,
        out_shape=out_shape[0] if single else out_shape,
        scratch_shapes=[pltpu.VMEM((rt, d), BF16)],
        compiler_params=_params(("arbitrary", "arbitrary"), 52),
        name=kind + "_in_proj",
    )(*args)


def _gla_gate_kernel(x_ref, sh_ref, sc_ref, wr_ref, w2_ref, b2_ref, o_ref):
    h = (x_ref[...] * (1.0 + sc_ref[0]) + sh_ref[0]).astype(BF16)
    r = _dot(h, wr_ref[...]).astype(BF16)
    pre = _dot(r, w2_ref[...]) + b2_ref[...]
    o_ref[...] = _log_sigmoid(pre) * (1.0 / GLA_NORMALIZER)


def _gla_gates(rs, x, mods, layer, w_r, w2, b2):
    rows, d = x.shape
    rt = rs.tile(256)
    n = w2.shape[1]
    return pl.pallas_call(
        _gla_gate_kernel,
        grid=(rows // rt,),
        in_specs=[pl.BlockSpec((rt, d), lambda i: (i, 0)),
                  pl.BlockSpec((1, 1, d), rs.mod_map(layer, 0, rt)),
                  pl.BlockSpec((1, 1, d), rs.mod_map(layer, 1, rt)),
                  pl.BlockSpec(w_r.shape, lambda i: (0, 0)),
                  pl.BlockSpec(w2.shape, lambda i: (0, 0)),
                  pl.BlockSpec((1, n), lambda i: (0, 0))],
        out_specs=pl.BlockSpec((rt, n), lambda i: (i, 0)),
        out_shape=jax.ShapeDtypeStruct((rows, n), F32),
        compiler_params=_params(("arbitrary",), 40),
        name="gla_gates",
    )(x, mods, mods, w_r, w2, b2)


def _gla_scan_kernel(*refs, hgrn, reverse, final, heads, group, dk, dv, n_chunks, qscale):
    if hgrn:
        q_ref, f_ref, v_ref, lb_ref = refs[:4]
        rest = refs[4:]
    else:
        q_ref, k_ref, v_ref, g_ref = refs[:4]
        rest = refs[4:]
    if final:
        og_ref, of_ref, gain_ref, o_ref, st_ref = rest
    else:
        o_ref, st_ref = rest

    @pl.when(pl.program_id(1) == 0)
    def _():
        st_ref[...] = jnp.zeros_like(st_ref)

    tc = n_chunks * CHUNK
    row = lax.broadcasted_iota(jnp.int32, (tc, tc), 0)
    col = lax.broadcasted_iota(jnp.int32, (tc, tc), 1)
    shift = CHUNK.bit_length() - 1
    same_chunk = (row >> shift) == (col >> shift)
    tri = same_chunk & ((row <= col) if reverse else (row >= col))
    tri_b = jnp.where(tri, 1.0, 0.0).astype(BF16)
    order = range(n_chunks - 1, -1, -1) if reverse else range(n_chunks)

    def intra(h):
        kcols = pl.ds(pl.multiple_of(h * dk, dk), dk)
        vcols = pl.ds(pl.multiple_of(h * dv, dv), dv)
        qf = q_ref[:, kcols].astype(F32)
        if hgrn:
            lb = lb_ref[:, kcols]
            s = jax.nn.sigmoid(f_ref[:, kcols].astype(F32))
            qf = qf * jax.nn.sigmoid(qf)
            kf = (1.0 - lb) * (1.0 - s)
            g = jnp.log(lb + (1.0 - lb) * s)
        else:
            kf = k_ref[:, kcols].astype(F32)
            g = g_ref[:, kcols]
        v = v_ref[:, vcols]
        g_hi = g.astype(BF16)
        g_lo = (g - g_hi.astype(F32)).astype(BF16)
        b = _dot(tri_b, g_hi) + _dot(tri_b, g_lo)
        qd = (qf * (qscale * jnp.exp(b))).astype(BF16)
        kdf = kf * jnp.exp(-b)
        att = jnp.where(tri, _dot_nt(qd, kdf.astype(BF16)), 0.0).astype(BF16)
        return vcols, b, qd, kdf, v, _dot(att, v)

    def finish(vcols, outs):
        o = jnp.concatenate(outs, axis=0)
        if final:
            o = o + of_ref[:, vcols]
            o = o * lax.rsqrt(jnp.mean(o * o, axis=-1, keepdims=True) + RMS_EPS) * gain_ref[...]
            og = og_ref[:, vcols].astype(F32)
            o = o * (og * jax.nn.sigmoid(og))
        o_ref[:, vcols] = o.astype(o_ref.dtype)

    def head_group(i, carry):
        hs = [i * group + j for j in range(group)]
        parts = [intra(h) for h in hs]
        states = [st_ref[h] for h in hs]
        outs = [[None] * n_chunks for _ in hs]
        for c in order:
            rows = slice(c * CHUNK, (c + 1) * CHUNK)
            end = c * CHUNK if reverse else (c + 1) * CHUNK - 1
            for j, (_, b, qd, kdf, v, o_intra) in enumerate(parts):
                st = states[j]
                outs[j][c] = o_intra[rows] + _dot_nt(qd[rows], st.astype(BF16))
                dec = jnp.exp(b[end:end + 1])
                states[j] = st * dec + _dot_tn(v[rows], (kdf[rows] * dec).astype(BF16))
        for j, h in enumerate(hs):
            st_ref[h] = states[j]
            finish(parts[j][0], outs[j])
        return carry

    lax.fori_loop(0, heads // group, head_group, 0)


def _gla_scan(rs, hgrn, reverse, qkv, col_blocks, extra, heads, dk, dv, qscale, final_args=None):
    tc = min(256, rs.n_ctx)
    ctx_blocks, lat_blocks = rs.n_ctx // tc, rs.n_lat // tc
    steps = ctx_blocks + lat_blocks
    kd_w, vd_w = heads * dk, heads * dv

    def row_block(b, s):
        if reverse:
            ctx = b * ctx_blocks + (ctx_blocks - 1 - s)
            lat = rs.ctx_rows // tc + b * lat_blocks + (steps - 1 - s)
        else:
            ctx = b * ctx_blocks + s
            lat = rs.ctx_rows // tc + b * lat_blocks + (s - ctx_blocks)
        return jnp.where(s < ctx_blocks, ctx, lat)

    def spec(width, cb):
        return pl.BlockSpec((tc, width), lambda b, s: (row_block(b, s), cb))

    in_specs = [spec(kd_w, qkv[0][1]), spec(kd_w, qkv[1][1]), spec(vd_w, qkv[2][1])]
    args = [qkv[0][0], qkv[1][0], qkv[2][0]]
    if hgrn:
        in_specs.append(pl.BlockSpec((1, kd_w), lambda b, s: (0, 0)))
    else:
        in_specs.append(spec(kd_w, col_blocks))
    args.append(extra)
    final = final_args is not None
    if final:
        z, gcb, o_fwd, gain = final_args
        in_specs += [spec(vd_w, gcb), spec(vd_w, 0), pl.BlockSpec((1, dv), lambda b, s: (0, 0))]
        args += [z, o_fwd, gain]
    return pl.pallas_call(
        functools.partial(_gla_scan_kernel, hgrn=hgrn, reverse=reverse, final=final, heads=heads,
                          group=8 if heads % 8 == 0 else 4 if heads % 4 == 0 else 2 - heads % 2,
                          dk=dk, dv=dv, n_chunks=tc // CHUNK, qscale=qscale),
        grid=(rs.batch, steps),
        in_specs=in_specs,
        out_specs=spec(vd_w, 0),
        out_shape=jax.ShapeDtypeStruct((rs.rows, vd_w), BF16 if final else F32),
        scratch_shapes=[pltpu.VMEM((heads, dv, dk), F32)],
        compiler_params=_params(("arbitrary", "arbitrary"), 40),
        name=("hgrn" if hgrn else "gla") + ("_bwd" if reverse else "_fwd"),
    )(*args)


def _lru_scan_kernel(*refs, reverse, final, n_blk, bw):
    if final:
        xc_ref, wg_ref, bg_ref, ls_ref, hf_ref, y_ref, o_ref, a_ref, u_ref, h_ref = refs
    else:
        xc_ref, wg_ref, bg_ref, ls_ref, o_ref, a_ref, u_ref, h_ref = refs
    tc, w = xc_ref.shape
    groups = tc // 8

    @pl.when(pl.program_id(1) == 0)
    def _():
        h_ref[...] = jnp.zeros_like(h_ref)

    for n in range(n_blk):
        cols = slice(n * bw, (n + 1) * bw)
        xb = xc_ref[:, cols]
        gr = jax.nn.sigmoid(_dot(xb, wg_ref[0, n]) + bg_ref[0:1, cols])
        gi = jax.nn.sigmoid(_dot(xb, wg_ref[1, n]) + bg_ref[1:2, cols])
        log_a = RG_C * gr * ls_ref[:, cols]
        a = jnp.exp(log_a)
        a_ref[:, cols] = a
        u_ref[:, cols] = jnp.sqrt(-jnp.tanh(log_a) * (a * a + 1.0)) * gi * xb.astype(F32)

    sub = lax.broadcasted_iota(jnp.int32, (8, w), 0)

    def group(i, h):
        g = (groups - 1 - i) if reverse else i
        rows = pl.ds(pl.multiple_of(g * 8, 8), 8)
        a, u = a_ref[rows, :], u_ref[rows, :]
        for k in (1, 2, 4):
            if reverse:
                ok = sub < 8 - k
                a_s, u_s = pltpu.roll(a, 8 - k, 0), pltpu.roll(u, 8 - k, 0)
            else:
                ok = sub >= k
                a_s, u_s = pltpu.roll(a, k, 0), pltpu.roll(u, k, 0)
            u = jnp.where(ok, a * u_s + u, u)
            a = jnp.where(ok, a * a_s, a)
        out = a * h + u
        last = out[0:1] if reverse else out[7:8]
        res = out
        if final:
            res = (out + hf_ref[rows, :]) * y_ref[rows, :].astype(F32)
        o_ref[rows, :] = res.astype(o_ref.dtype)
        return jnp.broadcast_to(last, (8, w))

    h_ref[...] = lax.fori_loop(0, groups, group, h_ref[...])


def _lru_scan(rs, reverse, xc, w_gate, b_gate, log_sig_lam, final_args=None):
    tc = min(256, rs.n_ctx)
    ctx_blocks, lat_blocks = rs.n_ctx // tc, rs.n_lat // tc
    steps = ctx_blocks + lat_blocks
    w = xc.shape[1]
    n_blk, bw = w_gate.shape[1], w_gate.shape[2]

    def row_block(b, s):
        if reverse:
            ctx = b * ctx_blocks + (ctx_blocks - 1 - s)
            lat = rs.ctx_rows // tc + b * lat_blocks + (steps - 1 - s)
        else:
            ctx = b * ctx_blocks + s
            lat = rs.ctx_rows // tc + b * lat_blocks + (s - ctx_blocks)
        return jnp.where(s < ctx_blocks, ctx, lat)

    stream = pl.BlockSpec((tc, w), lambda b, s: (row_block(b, s), 0))
    in_specs = [stream,
                pl.BlockSpec(w_gate.shape, lambda b, s: (0, 0, 0, 0)),
                pl.BlockSpec((2, w), lambda b, s: (0, 0)),
                pl.BlockSpec((1, w), lambda b, s: (0, 0))]
    args = [xc, w_gate, b_gate, log_sig_lam]
    final = final_args is not None
    if final:
        in_specs += [stream, stream]
        args += list(final_args)
    return pl.pallas_call(
        functools.partial(_lru_scan_kernel, reverse=reverse, final=final, n_blk=n_blk, bw=bw),
        grid=(rs.batch, steps),
        in_specs=in_specs,
        out_specs=stream,
        out_shape=jax.ShapeDtypeStruct((rs.rows, w), BF16 if final else F32),
        scratch_shapes=[pltpu.VMEM((tc, w), F32), pltpu.VMEM((tc, w), F32), pltpu.VMEM((8, w), F32)],
        compiler_params=_params(("arbitrary", "arbitrary"), 40),
        name="lru_bwd" if reverse else "lru_fwd",
    )(*args)


def _layer_norm(z, g, b):
    mu = jnp.mean(z, axis=-1, keepdims=True)
    zc = z - mu
    var = jnp.mean(zc * zc, axis=-1, keepdims=True)
    return zc * lax.rsqrt(var + LN_EPS) * g + b


def _pack_words(v):
    half = v.shape[1] // 2
    vb = v.astype(BF16).astype(F32)
    lo = lax.bitcast_convert_type(vb[:, :half], jnp.uint32) >> 16
    hi = lax.bitcast_convert_type(vb[:, half:], jnp.uint32) & jnp.uint32(0xFFFF0000)
    return hi | lo


def _store_packed(ref, row0, n, words):
    s_rows = words.shape[1] // 128
    for s in range(s_rows):
        ref[pl.ds(row0 * s_rows + s, n, stride=s_rows), :] = words[:, s * 128:(s + 1) * 128]


def _load_packed(ref, row0, n, s_rows):
    lo, hi = [], []
    for s in range(s_rows):
        w = ref[pl.ds(row0 * s_rows + s, n, stride=s_rows), :]
        lo.append(lax.bitcast_convert_type(w << 16, F32))
        hi.append(lax.bitcast_convert_type(w & jnp.uint32(0xFFFF0000), F32))
    return lo, hi


def _out_proj_kernel(a_ref, w_ref, x_ref, gate_ref, sh_ref, sc_ref, lg_ref, lb_ref, wr_ref, br_ref,
                     xo_ref, hp_ref, lt_ref, ya_ref, yb_ref, *, alpha):
    i = pl.program_id(0)

    @pl.when(i == 0)
    def _():
        yb_ref[...] = jnp.zeros_like(yb_ref)

    def step(prev_ref, new_ref):
        new_ref[...] = _dot(a_ref[...], w_ref[...])
        x_new = _layer_norm(alpha * x_ref[...] + gate_ref[0] * prev_ref[...], lg_ref[...], lb_ref[...])
        xo_ref[...] = x_new
        h = x_new * (1.0 + sc_ref[0]) + sh_ref[0]
        _store_packed(hp_ref, 0, h.shape[0], _pack_words(h))
        lt_ref[...] = _dot_nt(wr_ref[...], h.astype(BF16)) + br_ref[...]

    @pl.when(i % 2 == 0)
    def _():
        step(yb_ref, ya_ref)

    @pl.when(i % 2 == 1)
    def _():
        step(ya_ref, yb_ref)


def _out_proj(rs, a, w_out, x, mods, layer, ln_g, ln_b, w_router_t, b_router, alpha, skip_rows=0):
    rows, d = rs.rows, x.shape[1]
    k = a.shape[1]
    n_e = w_router_t.shape[0]
    rt = rs.tile(256)
    n_tiles = rows // rt
    skip = skip_rows // rt
    lag = lambda i: jnp.maximum(i - 1, 0)
    row = lambda i: (lag(i), 0)
    const = lambda i: (0, 0)
    mod = lambda which: (lambda i: rs.mod_map(layer, which, rt)(lag(i)))
    return pl.pallas_call(
        functools.partial(_out_proj_kernel, alpha=alpha),
        grid=(n_tiles + 1,),
        in_specs=[pl.BlockSpec((rt, k), lambda i: (jnp.minimum(i, n_tiles - 1) + skip, 0)),
                  pl.BlockSpec((k, d), const),
                  pl.BlockSpec((rt, d), lambda i: (lag(i) + skip, 0)),
                  pl.BlockSpec((1, 1, d), mod(2)),
                  pl.BlockSpec((1, 1, d), mod(3)),
                  pl.BlockSpec((1, 1, d), mod(4)),
                  pl.BlockSpec((1, d), const),
                  pl.BlockSpec((1, d), const),
                  pl.BlockSpec((n_e, d), const),
                  pl.BlockSpec((n_e, 1), const)],
        out_specs=[pl.BlockSpec((rt, d), row),
                   pl.BlockSpec((rt * (d // 256), 128), row),
                   pl.BlockSpec((n_e, rt), lambda i: (0, lag(i)))],
        out_shape=[jax.ShapeDtypeStruct((rows, d), F32),
                   jax.ShapeDtypeStruct((rows * (d // 256), 128), jnp.uint32),
                   jax.ShapeDtypeStruct((n_e, rows), F32)],
        scratch_shapes=[pltpu.VMEM((rt, d), F32), pltpu.VMEM((rt, d), F32)],
        compiler_params=_params(("arbitrary",), 52),
        name="out_proj_ln",
    )(a, w_out, x, mods, mods, mods, ln_g, ln_b, w_router_t, b_router)


def _route_kernel(lt_ref, e_ref, g_ref, r_ref, cnt_ref, carry_ref):
    n_e, tr = lt_ref.shape

    @pl.when(pl.program_id(0) == 0)
    def _():
        carry_ref[...] = jnp.zeros_like(carry_ref)

    lg = lt_ref[...]
    eid = lax.broadcasted_iota(jnp.int32, (n_e, tr), 0)
    vals, idxs, sels = [], [], []
    for _ in range(TOP_K):
        m = jnp.max(lg, axis=0, keepdims=True)
        idx = jnp.min(jnp.where(lg == m, eid, n_e), axis=0, keepdims=True)
        sel = eid == idx
        vals.append(m)
        idxs.append(idx)
        sels.append(sel)
        lg = jnp.where(sel, -jnp.inf, lg)
    ex = [jnp.exp(v - vals[0]) for v in vals]
    denom = ex[0] + ex[1] + ex[2] + ex[3]
    chosen = sels[0] | sels[1] | sels[2] | sels[3]
    member = jnp.where(chosen, 1.0, 0.0)
    srow = lax.broadcasted_iota(jnp.int32, (tr, tr), 0)
    scol = lax.broadcasted_iota(jnp.int32, (tr, tr), 1)
    before = jnp.where(srow < scol, 1.0, 0.0).astype(BF16)
    prefix = _dot(member.astype(BF16), before) + carry_ref[...]
    for k in range(TOP_K):
        e_ref[k:k + 1, :] = idxs[k]
        g_ref[k:k + 1, :] = ex[k] / denom
        r_ref[k:k + 1, :] = jnp.sum(jnp.where(sels[k], prefix, 0.0), axis=0, keepdims=True).astype(jnp.int32)
    carry_ref[...] = carry_ref[...] + jnp.sum(member, axis=1, keepdims=True)
    cnt_ref[...] = carry_ref[...]


def _route(logits_t):
    n_e, rows = logits_t.shape
    tr = _pick(rows, 512)
    tok = pl.BlockSpec((TOP_K, tr), lambda i: (0, i))
    return pl.pallas_call(
        _route_kernel,
        grid=(rows // tr,),
        in_specs=[pl.BlockSpec((n_e, tr), lambda i: (0, i))],
        out_specs=[tok, tok, tok, pl.BlockSpec((n_e, 1), lambda i: (0, 0))],
        out_shape=[jax.ShapeDtypeStruct((TOP_K, rows), jnp.int32),
                   jax.ShapeDtypeStruct((TOP_K, rows), F32),
                   jax.ShapeDtypeStruct((TOP_K, rows), jnp.int32),
                   jax.ShapeDtypeStruct((n_e, 1), F32)],
        scratch_shapes=[pltpu.VMEM((n_e, 1), F32)],
        compiler_params=_params(("arbitrary",), 32),
        name="route",
    )(logits_t)


def _row_slab(ref, row, s_rows):
    return ref.at[pl.ds(pl.multiple_of(row * s_rows, s_rows), s_rows), :]


def _dispatch_kernel(pe_ref, pd_ref, dest_ref, hp_ref, xg_ref, idx_ref, zero_ref, sem, idx_sem,
                     *, rt, bm, s_rows, n_e, n_blocks):
    @pl.when(pl.program_id(0) == 0)
    def _():
        zero_ref[...] = jnp.zeros_like(zero_ref)
        def fill(last_row):
            first = pl.multiple_of((last_row - bm) * s_rows, s_rows)
            return pltpu.make_async_copy(zero_ref, xg_ref.at[pl.ds(first, bm * s_rows), :], sem)

        def start(e, c):
            @pl.when(pd_ref[e] > 0)
            def _():
                fill(pe_ref[e]).start()
            return c

        def wait(e, c):
            @pl.when(pd_ref[e] > 0)
            def _():
                fill(pe_ref[e]).wait()
            return c

        lax.fori_loop(0, n_e, start, 0)
        lax.fori_loop(0, n_e, wait, 0)
        n_used = pe_ref[n_e - 1] // bm
        lax.fori_loop(n_used, n_blocks, lambda j, c: (fill((j + 1) * bm).start(), c)[1], 0)
        lax.fori_loop(n_used, n_blocks, lambda j, c: (fill((j + 1) * bm).wait(), c)[1], 0)

    load = pltpu.make_async_copy(dest_ref, idx_ref, idx_sem)
    load.start()
    load.wait()

    def token(t, c):
        src = _row_slab(hp_ref, t, s_rows)
        for k in range(TOP_K):
            pltpu.make_async_copy(src, _row_slab(xg_ref, idx_ref[k, t], s_rows), sem).start(priority=k % 2)
        return c

    lax.fori_loop(0, rt, token, 0)
    for _ in range(TOP_K):
        pltpu.make_async_copy(hp_ref, xg_ref.at[pl.ds(0, rt * s_rows), :], sem).wait()


def _dispatch(hp, dest, pad_end, padded, n_rows, bm, rt):
    s_rows = hp.shape[0] // dest.shape[1]
    rows = dest.shape[1]
    n_e = pad_end.shape[0]
    grid_spec = pltpu.PrefetchScalarGridSpec(
        num_scalar_prefetch=2,
        grid=(rows // rt,),
        in_specs=[pl.BlockSpec((TOP_K, rt), lambda i, pe, pd: (0, i)),
                  pl.BlockSpec((rt * s_rows, 128), lambda i, pe, pd: (i, 0))],
        out_specs=pl.BlockSpec(memory_space=pl.ANY),
        scratch_shapes=[pltpu.SMEM((TOP_K, rt), jnp.int32),
                        pltpu.VMEM((bm * s_rows, 128), jnp.uint32),
                        pltpu.SemaphoreType.DMA,
                        pltpu.SemaphoreType.DMA],
    )
    return pl.pallas_call(
        functools.partial(_dispatch_kernel, rt=rt, bm=bm, s_rows=s_rows, n_e=n_e, n_blocks=n_rows // bm),
        grid_spec=grid_spec,
        out_shape=jax.ShapeDtypeStruct((n_rows * s_rows, 128), jnp.uint32),
        compiler_params=_params(("arbitrary",), 32),
        name="dispatch",
    )(pad_end, padded, dest, hp)


N_STAGE = 8

def _expert_kernel(be_ref, nb_ref, nx_ref, c0_ref, c1_ref, par_ref, nv_ref,
                   x_ref, wgu_hbm, bgu_ref, wd_hbm, bd_ref, o_ref,
                   wgu_s, wd_s, xb_s, st_ref, sem,
                   *, layer, d_ff, bm, s_rows, wrows, n_gu, n_d):
    i = pl.program_id(0)
    n_chunks = n_gu + n_d

    ff2, d = wgu_s.shape[2], wd_s.shape[2]

    def gu_copy(e, c):
        rows = pl.ds(pl.multiple_of(c * wrows, wrows), wrows)
        slot = c % N_STAGE
        return pltpu.make_async_copy(wgu_hbm.at[layer, e, rows, :], st_ref.at[slot, :, pl.ds(0, ff2)], sem.at[slot])

    def down_copy(e, c):
        rows = pl.ds(pl.multiple_of((c - n_gu) * wrows, wrows), wrows)
        slot = c % N_STAGE
        return pltpu.make_async_copy(wd_hbm.at[layer, e, rows, :], st_ref.at[slot, :, pl.ds(0, d)], sem.at[slot])

    def start(e, c):
        @pl.when(c < n_gu)
        def _():
            gu_copy(e, c).start()

        @pl.when((c >= n_gu) & (c < n_chunks))
        def _():
            down_copy(e, c).start()

    def land(e, c, buf):
        @pl.when(c < n_gu)
        def _():
            gu_copy(e, c).wait()
            rows = pl.ds(pl.multiple_of(c * wrows, wrows), wrows)
            wgu_s[buf, rows, :] = st_ref[c % N_STAGE, :, 0:ff2].astype(BF16)

        @pl.when(c >= n_gu)
        def _():
            down_copy(e, c).wait()
            rows = pl.ds(pl.multiple_of((c - n_gu) * wrows, wrows), wrows)
            wd_s[buf, rows, :] = st_ref[c % N_STAGE, :, 0:d].astype(BF16)

    def prime(e):
        for c in range(N_STAGE - 1):
            start(e, jnp.int32(c))

    def stream(e, lo, hi, buf):
        def body(c, carry):
            start(e, c + (N_STAGE - 1))
            land(e, c, buf)
            return carry
        lax.fori_loop(lo, hi, body, 0)

    @pl.when(i < nb_ref[0])
    def _():
        e, nxt, cur = be_ref[i], nx_ref[i], par_ref[i]
        first = (i == 0) | (e != be_ref[jnp.maximum(i - 1, 0)])

        @pl.when(i == 0)
        def _():
            prime(e)
            stream(e, 0, n_chunks, cur)

        @pl.when(first & (nxt >= 0))
        def _():
            prime(nxt)

        @pl.when(nxt >= 0)
        def _():
            stream(nxt, c0_ref[i], c1_ref[i], 1 - cur)

        half = s_rows * 128

        def ffn(rows):
            lo, hi = _load_packed(x_ref, 0, rows, s_rows)
            for s in range(s_rows):
                xb_s[0:rows, s * 128:(s + 1) * 128] = lo[s].astype(BF16)
                xb_s[0:rows, half + s * 128:half + (s + 1) * 128] = hi[s].astype(BF16)
            gu = _dot(xb_s[0:rows, :], wgu_s[cur]) + bgu_ref[0]
            g = jnp.minimum(gu[:, :d_ff], SWIGLU_LIMIT)
            u = jnp.clip(gu[:, d_ff:], -SWIGLU_LIMIT, SWIGLU_LIMIT)
            act = (g * jax.nn.sigmoid(SWIGLU_ALPHA * g) * (u + 1.0)).astype(BF16)
            y = _dot(act, wd_s[cur]) + bd_ref[0]
            _store_packed(o_ref, 0, rows, _pack_words(y))
            if rows < bm:
                o_ref[rows * s_rows:bm * s_rows, :] = jnp.zeros(((bm - rows) * s_rows, 128), o_ref.dtype)

        @pl.when(nv_ref[i] > bm // 2)
        def _():
            ffn(bm)

        @pl.when(nv_ref[i] <= bm // 2)
        def _():
            ffn(bm // 2)

    @pl.when(i >= nb_ref[0])
    def _():
        o_ref[...] = jnp.zeros_like(o_ref)


def _experts(xg, counts, pad_end, padded, layer, w_gu, b_gu, w_down, b_down, bm):
    depth, n_e, d, ff2 = w_gu.shape
    d_ff = ff2 // 2
    s_rows = d // 256
    n_blocks = xg.shape[0] // (bm * s_rows)
    wrows = _pick(d_ff, 128)
    n_gu, n_d = d // wrows, d_ff // wrows
    n_chunks = n_gu + n_d

    experts = jnp.arange(n_e, dtype=jnp.int32)
    block_start = jnp.arange(n_blocks, dtype=jnp.int32) * bm
    block_e = jnp.minimum(jnp.sum(pad_end[None, :] <= block_start[:, None], axis=1), n_e - 1).astype(jnp.int32)
    n_used = (pad_end[-1:] // bm).astype(jnp.int32)
    onehot = block_e[:, None] == experts[None, :]
    pick = lambda table: jnp.sum(jnp.where(onehot, table[None, :], 0), axis=1)
    blocks_of = padded // bm
    local = jnp.arange(n_blocks, dtype=jnp.int32) - pick((pad_end - padded) // bm)
    nblk = pick(blocks_of)
    later_blocks = jnp.maximum(nblk - 1, 1)
    share = lambda l: jnp.where(nblk > 1, (n_chunks * jnp.maximum(l, 0)) // later_blocks, n_chunks * (l + 1))
    c0, c1 = share(local - 1), share(local)
    live = blocks_of > 0
    later = (experts[None, :] > experts[:, None]) & live[None, :]
    next_of = jnp.min(jnp.where(later, experts[None, :], n_e), axis=1)
    next_of = jnp.where(next_of == n_e, -1, next_of)
    ordinal = jnp.cumsum(live.astype(jnp.int32)) - live.astype(jnp.int32)
    valid = jnp.clip(pick(counts) - local * bm, 0, bm)
    tables = [block_e, n_used, pick(next_of), c0, c1, pick(ordinal) % 2, valid]
    tables = [t.astype(jnp.int32) for t in tables]

    used = lambda i, nb: jnp.minimum(i, nb[0] - 1)
    bias = lambda i, be, nb, *_: (layer, be[used(i, nb)], 0, 0)
    grid_spec = pltpu.PrefetchScalarGridSpec(
        num_scalar_prefetch=len(tables),
        grid=(n_blocks,),
        in_specs=[pl.BlockSpec((bm * s_rows, 128), lambda i, be, nb, *_: (used(i, nb), 0)),
                  pl.BlockSpec(memory_space=pl.ANY),
                  pl.BlockSpec((None, 1, 1, ff2), bias),
                  pl.BlockSpec(memory_space=pl.ANY),
                  pl.BlockSpec((None, 1, 1, d), bias)],
        out_specs=pl.BlockSpec((bm * s_rows, 128), lambda i, *_: (i, 0)),
        scratch_shapes=[pltpu.VMEM((2, d, ff2), BF16), pltpu.VMEM((2, d_ff, d), BF16), pltpu.VMEM((bm, d), BF16),
                        pltpu.VMEM((N_STAGE, wrows, max(ff2, d)), F32),
                        pltpu.SemaphoreType.DMA((N_STAGE,))],
    )
    return pl.pallas_call(
        functools.partial(_expert_kernel, layer=layer, d_ff=d_ff, bm=bm, s_rows=s_rows,
                          wrows=wrows, n_gu=n_gu, n_d=n_d),
        grid_spec=grid_spec,
        out_shape=jax.ShapeDtypeStruct(xg.shape, jnp.uint32),
        compiler_params=_params(("arbitrary",), 58),
        name="experts",
    )(*tables, xg, w_gu, b_gu.reshape(depth, n_e, 1, ff2), w_down, b_down.reshape(depth, n_e, 1, d))


COMBINE_SUB = 32


def _combine_kernel(dcur_ref, dnext_ref, g_ref, x_ref, gate_ref, lg_ref, lb_ref, y_ref, o_ref,
                    idx_ref, buf_ref, sem, idx_sem, *, alpha, rt, s_rows, n_tiles):
    i = pl.program_id(0)
    cur, nxt = i % 2, (i + 1) % 2
    sub = min(COMBINE_SUB, rt)

    def gather(slot, t, k):
        dst = buf_ref.at[slot, pl.ds(pl.multiple_of((k * rt + t) * s_rows, s_rows), s_rows), :]
        return pltpu.make_async_copy(_row_slab(y_ref, idx_ref[slot, k, t], s_rows), dst, sem.at[slot])

    def retire(slot):
        for k in range(TOP_K):
            part = pl.ds(k * rt * s_rows, rt * s_rows)
            pltpu.make_async_copy(y_ref.at[pl.ds(0, rt * s_rows), :], buf_ref.at[slot, part, :], sem.at[slot]).wait()

    def load_indices(src_ref, slot):
        load = pltpu.make_async_copy(src_ref, idx_ref.at[slot], idx_sem)
        load.start()
        load.wait()

    @pl.when(i == 0)
    def _():
        load_indices(dcur_ref, 0)

        def token(t, c):
            for k in range(TOP_K):
                gather(0, t, k).start(priority=k % 2)
            return c
        lax.fori_loop(0, rt, token, 0)

    load_indices(dnext_ref, nxt)
    retire(cur)

    def rows_step(j, carry):
        r0 = pl.multiple_of(j * sub, sub)
        g = g_ref[pl.ds(r0, sub), :]
        lo_parts, hi_parts = [], []
        for s in range(s_rows):
            for tt in range(s * sub // s_rows, (s + 1) * sub // s_rows):
                for k in range(TOP_K):
                    gather(nxt, r0 + tt, k).start(priority=k % 2)
            lo = hi = None
            for k in range(TOP_K):
                w = buf_ref[cur, pl.ds((k * rt + r0) * s_rows + s, sub, stride=s_rows), :]
                gk = g[:, k:k + 1]
                lo_k = gk * lax.bitcast_convert_type(w << 16, F32)
                hi_k = gk * lax.bitcast_convert_type(w & jnp.uint32(0xFFFF0000), F32)
                lo = lo_k if lo is None else lo + lo_k
                hi = hi_k if hi is None else hi + hi_k
            lo_parts.append(lo)
            hi_parts.append(hi)
        acc = jnp.concatenate(lo_parts + hi_parts, axis=1)
        z = alpha * x_ref[pl.ds(r0, sub), :] + gate_ref[0] * acc
        o_ref[pl.ds(r0, sub), :] = _layer_norm(z, lg_ref[...], lb_ref[...])
        return carry

    lax.fori_loop(0, rt // sub, rows_step, 0)

    @pl.when(i == n_tiles - 1)
    def _():
        retire(nxt)


def _combine(rs, y, dest, gates, x, mods, layer, ln_g, ln_b, alpha):
    rows, d = x.shape
    s_rows = d // 256
    rt = rs.tile(256)
    n_tiles = rows // rt
    row = lambda i: (i, 0)
    const = lambda i: (0, 0)
    return pl.pallas_call(
        functools.partial(_combine_kernel, alpha=alpha, rt=rt, s_rows=s_rows, n_tiles=n_tiles),
        grid=(n_tiles,),
        in_specs=[pl.BlockSpec((TOP_K, rt), lambda i: (0, i)),
                  pl.BlockSpec((TOP_K, rt), lambda i: (0, jnp.minimum(i + 1, n_tiles - 1))),
                  pl.BlockSpec((rt, TOP_K), row),
                  pl.BlockSpec((rt, d), row),
                  pl.BlockSpec((1, 1, d), rs.mod_map(layer, 5, rt)),
                  pl.BlockSpec((1, d), const),
                  pl.BlockSpec((1, d), const),
                  pl.BlockSpec(memory_space=pl.ANY)],
        out_specs=pl.BlockSpec((rt, d), row),
        out_shape=jax.ShapeDtypeStruct((rows, d), F32),
        scratch_shapes=[pltpu.SMEM((2, TOP_K, rt), jnp.int32),
                        pltpu.VMEM((2, TOP_K * rt * s_rows, 128), jnp.uint32),
                        pltpu.SemaphoreType.DMA((2,)),
                        pltpu.SemaphoreType.DMA],
        compiler_params=_params(("arbitrary",), 40),
        name="combine_ln",
    )(dest, dest, gates, x, mods, ln_g, ln_b, y)


def _moe(rs, hp, logits_t, x, mods, layer, ln_g, ln_b, w_gu, b_gu, w_down, b_down, alpha):
    rows = x.shape[0]
    n_e = w_gu.shape[1]
    bm = 512 if rows * TOP_K >= 512 * n_e * 4 else 128
    e_t, g_t, r_t, cnt = _route(logits_t)
    counts = cnt[:, 0].astype(jnp.int32)
    padded = (counts + bm - 1) // bm * bm
    pad_end = jnp.cumsum(padded)
    first = pad_end - padded
    experts = jnp.arange(n_e, dtype=jnp.int32)
    first_of = jnp.sum(jnp.where(e_t[None] == experts[:, None, None], first[:, None, None], 0), axis=0)
    dest = first_of + r_t
    n_blocks = -(-(rows * TOP_K + n_e * (bm - 1)) // bm)
    xg = _dispatch(hp, dest, pad_end, padded, n_blocks * bm, bm, rs.tile(256))
    y = _experts(xg, counts, pad_end, padded, layer, w_gu, b_gu, w_down, b_down, bm)
    return _combine(rs, y, dest, g_t.T, x, mods, layer, ln_g, ln_b, alpha)


def kernel(x, c, ctx, c_ctx, ada_w, ada_b, ln_g, ln_b, sc_w_in, sc_conv, sc_w_out, gla_w_in, gla_w_gate2, gla_b_gate, gla_norm, gla_w_out, lru_w_in, lru_conv, lru_conv_b, lru_w_gate, lru_b_gate, lru_lambda, lru_w_out, hg_w_in, hg_lb_raw, hg_norm, hg_w_out, moe_w_router, moe_b_router, moe_w_gu, moe_b_gu, moe_w_down, moe_b_down):
    batch, n_lat, d = x.shape
    n_ctx = ctx.shape[1]
    depth = ada_w.shape[0]
    alpha = (2 * depth) ** 0.25
    assert GRID_W & (GRID_W - 1) == 0 and n_ctx & (n_ctx - 1) == 0 and batch < MOD_ROWS
    rs = _Rows(batch, n_ctx, n_lat)

    cc = jnp.zeros((MOD_ROWS, d), F32).at[:batch].set(c).at[batch].set(c_ctx)
    mods = _adaln(cc, ada_w, ada_b).reshape(depth * MOD_ROWS * N_MOD, 1, d)
    xs = jnp.concatenate([ctx.reshape(-1, d), x.reshape(-1, d)], axis=0)

    for i in range(depth):
        kind, j = i % 4, i // 4
        with_ctx = i < depth - 1
        if kind == 0:
            w_in = sc_w_in[j].astype(BF16)
            a = _in_proj("sconv", rs, xs, mods, i, [(w_in, 0), (w_in, 1), (w_in, 2)], d, [BF16],
                         extra=[sc_conv[j]])
            w_out = sc_w_out[j]
        elif kind == 1:
            dv = gla_norm.shape[-1]
            dk = dv // 2
            heads = gla_w_gate2.shape[-1] // dk
            kd_w, vd_w = heads * dk, heads * dv
            n_main = 2 * kd_w + 2 * vd_w
            rank = gla_w_gate2.shape[2]
            w_in = gla_w_in[j]
            z = _in_proj("plain", rs, xs, mods, i, [(w_in[:, :n_main].astype(BF16), 0)], n_main, [BF16],
                         tn_pref=1024)
            w2 = jnp.zeros((2 * rank, 2 * kd_w), F32)
            w2 = w2.at[:rank, :kd_w].set(gla_w_gate2[j, 0]).at[rank:, kd_w:].set(gla_w_gate2[j, 1])
            gates = _gla_gates(rs, xs, mods, i, w_in[:, n_main:].astype(BF16), w2.astype(BF16),
                               gla_b_gate[j].reshape(1, 2 * kd_w))
            qkv = [(z, 0), (z, 1), (z, kd_w * 2 // vd_w)]
            scale = dk ** -0.5
            o_f = _gla_scan(rs, False, False, qkv, 0, gates, heads, dk, dv, scale)
            a = _gla_scan(rs, False, True, qkv, 1, gates, heads, dk, dv, scale,
                          final_args=(z, (2 * kd_w + vd_w) // vd_w, o_f, gla_norm[j].reshape(1, dv)))
            w_out = gla_w_out[j]
        elif kind == 2:
            w = lru_w_in.shape[-1] // 2
            w_in = lru_w_in[j].astype(BF16)
            y, xc = _in_proj("lru", rs, xs, mods, i, [(w_in, 0), (w_in, 1)], w, [BF16, BF16],
                             extra=[lru_conv[j], lru_conv_b[j].reshape(1, w)])
            log_sig = _log_sigmoid(lru_lambda[j])
            wg, bgate = lru_w_gate[j].astype(BF16), lru_b_gate[j]
            h_f = _lru_scan(rs, False, xc, wg[0], bgate[0], log_sig[0:1])
            a = _lru_scan(rs, True, xc, wg[1], bgate[1], log_sig[1:2], final_args=(h_f, y))
            w_out = lru_w_out[j]
        else:
            dk = hg_norm.shape[-1]
            heads = d // dk
            z = _in_proj("plain", rs, xs, mods, i, [(hg_w_in[j].astype(BF16), 0)], 5 * d, [BF16], tn_pref=1024)
            p = jax.nn.softmax(hg_lb_raw, axis=0)
            lower = (jnp.cumsum(p, axis=0)[i] - p[0]).reshape(1, d)
            scale = dk ** -0.5
            o_f = _gla_scan(rs, True, False, [(z, 0), (z, 1), (z, 3)], 0, lower, heads, dk, dk, scale)
            a = _gla_scan(rs, True, True, [(z, 0), (z, 2), (z, 3)], 0, lower, heads, dk, dk, scale,
                          final_args=(z, 4, o_f, hg_norm[j].reshape(1, dk)))
            w_out = hg_w_out[j]

        skip = 0
        if not with_ctx:
            skip = rs.ctx_rows
            assert skip % _Rows(batch, 0, n_lat).tile(256) == 0
            rs = _Rows(batch, 0, n_lat)
        xs, hp, logits_t = _out_proj(rs, a, w_out.astype(BF16), xs, mods, i,
                                     ln_g[i, 0].reshape(1, d), ln_b[i, 0].reshape(1, d),
                                     moe_w_router[i].T.astype(BF16), moe_b_router[i].reshape(-1, 1), alpha,
                                     skip_rows=skip)
        xs = _moe(rs, hp, logits_t, xs, mods, i, ln_g[i, 1].reshape(1, d), ln_b[i, 1].reshape(1, d),
                  moe_w_gu, moe_b_gu, moe_w_down, moe_b_down, alpha)
    return xs[rs.ctx_rows:].reshape(batch, n_lat, d)
```

```python
import functools

import jax
import jax.numpy as jnp
from jax import lax
from jax.experimental import pallas as pl
from jax.experimental.pallas import tpu as pltpu

F32 = jnp.float32
BF16 = jnp.bfloat16

GRID_W = 64
CHUNK = 64
TOP_K = 4
N_MOD = 6
MOD_ROWS = 8
GLA_NORMALIZER = 16.0
RG_C = 8.0
SWIGLU_LIMIT = 7.0
SWIGLU_ALPHA = 1.702
LN_EPS = 1e-5
RMS_EPS = 1e-6
MIB = 1024 * 1024


def _params(sem, vmem_mib):
    return pltpu.CompilerParams(dimension_semantics=sem, vmem_limit_bytes=vmem_mib * MIB)


def _dot(a, b):
    return jnp.dot(a, b, preferred_element_type=F32)


def _dot_nt(a, b):
    return lax.dot_general(a, b, (((1,), (1,)), ((), ())), preferred_element_type=F32)


def _dot_tn(a, b):
    return lax.dot_general(a, b, (((0,), (0,)), ((), ())), preferred_element_type=F32)


def _log_sigmoid(x):
    return jnp.minimum(x, 0.0) - jnp.log1p(jnp.exp(-jnp.abs(x)))


def _pick(n, pref):
    t = min(n, pref)
    while n % t:
        t //= 2
    return t


class _Rows:
    def __init__(self, batch, n_ctx, n_lat):
        self.batch, self.n_ctx, self.n_lat = batch, n_ctx, n_lat
        self.ctx_rows = batch * n_ctx
        self.rows = self.ctx_rows + batch * n_lat

    def tile(self, pref):
        t = min(pref, self.n_lat)
        while self.n_lat % t or self.ctx_rows % t:
            t //= 2
        return t

    def mod_map(self, layer, which, rt):
        def index(i, *_):
            r0 = i * rt
            row = jnp.where(r0 < self.ctx_rows, self.batch, (r0 - self.ctx_rows) // self.n_lat)
            return ((layer * MOD_ROWS + row) * N_MOD + which, 0, 0)
        return index


def _adaln_kernel(c_ref, w_ref, b_ref, o_ref):
    c = c_ref[...]
    a = (c * jax.nn.sigmoid(c)).astype(BF16)
    o_ref[0] = _dot(a, w_ref[0].astype(BF16)) + b_ref[0]


def _adaln(cc, ada_w, ada_b):
    depth, d, n = ada_w.shape
    tn = _pick(n, 1024)
    return pl.pallas_call(
        _adaln_kernel,
        grid=(depth, n // tn),
        in_specs=[pl.BlockSpec((MOD_ROWS, d), lambda l, j: (0, 0)),
                  pl.BlockSpec((1, d, tn), lambda l, j: (l, 0, j)),
                  pl.BlockSpec((1, 1, tn), lambda l, j: (l, 0, j))],
        out_specs=pl.BlockSpec((1, MOD_ROWS, tn), lambda l, j: (l, 0, j)),
        out_shape=jax.ShapeDtypeStruct((depth, MOD_ROWS, n), F32),
        compiler_params=_params(("arbitrary", "arbitrary"), 40),
        name="adaln",
    )(cc, ada_w, ada_b.reshape(depth, 1, n))


def _modulate(x_ref, sh_ref, sc_ref, h_ref):
    @pl.when(pl.program_id(1) == 0)
    def _():
        h_ref[...] = (x_ref[...] * (1.0 + sc_ref[0]) + sh_ref[0]).astype(BF16)


def _segment_pos(shape, ctx_tiles, seg_ctx, seg_lat):
    mask = jnp.where(pl.program_id(0) < ctx_tiles, seg_ctx - 1, seg_lat - 1)
    return lax.broadcasted_iota(jnp.int32, shape, 0) & mask, mask


def _shift_rows(p, pos, seg_last, offset):
    rt = p.shape[0]
    rolled = pltpu.roll(p, (-offset) % rt, 0)
    ok = (pos + offset >= 0) & (pos + offset <= seg_last)
    return jnp.where(ok, rolled, 0.0)


def _col_parts(width, part=256):
    part = min(part, width)
    return [slice(c, c + part) for c in range(0, width, part)]


def _proj_kernel(x_ref, sh_ref, sc_ref, w_ref, o_ref, h_ref):
    _modulate(x_ref, sh_ref, sc_ref, h_ref)
    o_ref[...] = _dot(h_ref[...], w_ref[...]).astype(o_ref.dtype)


def _sconv_in_kernel(x_ref, sh_ref, sc_ref, wb_ref, wc_ref, wv_ref, cw_ref, o_ref, h_ref,
                     *, ctx_tiles, seg_ctx, seg_lat):
    _modulate(x_ref, sh_ref, sc_ref, h_ref)
    h = h_ref[...]
    for cols in _col_parts(o_ref.shape[1]):
        bg = _dot(h, wb_ref[:, cols])
        p = _dot(h, wc_ref[:, cols]) * _dot(h, wv_ref[:, cols])
        pos, last = _segment_pos(p.shape, ctx_tiles, seg_ctx, seg_lat)
        cw = cw_ref[:, cols]
        conv = (cw[0:1] * _shift_rows(p, pos, last, -1) + cw[1:2] * p
                + cw[2:3] * _shift_rows(p, pos, last, 1))
        o_ref[:, cols] = (bg * conv).astype(o_ref.dtype)


def _lru_in_kernel(x_ref, sh_ref, sc_ref, wy_ref, wx_ref, cw_ref, cb_ref, y_ref, xc_ref, h_ref,
                   *, ctx_tiles, seg_ctx, seg_lat):
    _modulate(x_ref, sh_ref, sc_ref, h_ref)
    h = h_ref[...]
    for cols in _col_parts(y_ref.shape[1]):
        y_ref[:, cols] = jax.nn.gelu(_dot(h, wy_ref[:, cols])).astype(y_ref.dtype)
        xb = _dot(h, wx_ref[:, cols])
        pos, last = _segment_pos(xb.shape, ctx_tiles, seg_ctx, seg_lat)
        cw = cw_ref[:, cols]
        conv = (cw[0:1] * _shift_rows(xb, pos, last, -1) + cw[1:2] * xb
                + cw[2:3] * _shift_rows(xb, pos, last, 1) + cw[3:4] * _shift_rows(xb, pos, last, 2))
        xc_ref[:, cols] = (conv + cb_ref[:, cols]).astype(xc_ref.dtype)


def _in_proj(kind, rs, x, mods, layer, weights, n_out, out_dtypes, extra=(), tn_pref=512, rt_pref=1024):
    rows, d = x.shape
    rt = rs.tile(rt_pref)
    tn = _pick(n_out, tn_pref)
    nj = n_out // tn
    in_specs = [pl.BlockSpec((rt, d), lambda i, j: (i, 0)),
                pl.BlockSpec((1, 1, d), rs.mod_map(layer, 0, rt)),
                pl.BlockSpec((1, 1, d), rs.mod_map(layer, 1, rt))]
    args = [x, mods, mods]
    for w, off in weights:
        in_specs.append(pl.BlockSpec((d, tn), functools.partial(lambda i, j, o: (0, o * nj + j), o=off)))
        args.append(w)
    for e in extra:
        in_specs.append(pl.BlockSpec((e.shape[0], tn), lambda i, j: (0, j)))
        args.append(e)
    out_specs = [pl.BlockSpec((rt, tn), lambda i, j: (i, j)) for _ in out_dtypes]
    out_shape = [jax.ShapeDtypeStruct((rows, n_out), dt) for dt in out_dtypes]
    ctx_tiles = rs.ctx_rows // rt
    if kind == "plain":
        body = _proj_kernel
    elif kind == "sconv":
        body = functools.partial(_sconv_in_kernel, ctx_tiles=ctx_tiles, seg_ctx=rs.n_ctx, seg_lat=GRID_W)
    else:
        body = functools.partial(_lru_in_kernel, ctx_tiles=ctx_tiles, seg_ctx=rs.n_ctx, seg_lat=GRID_W)
    single = len(out_dtypes) == 1
    return pl.pallas_call(
        body,
        grid=(rows // rt, nj),
        in_specs=in_specs,
        out_specs=out_specs[0] if single else out_specs,
        out_shape=out_shape[0] if single else out_shape,
        scratch_shapes=[pltpu.VMEM((rt, d), BF16)],
        compiler_params=_params(("arbitrary", "arbitrary"), 52),
        name=kind + "_in_proj",
    )(*args)


def _chunk_tri(n, reverse):
    row = lax.broadcasted_iota(jnp.int32, (n, n), 0)
    col = lax.broadcasted_iota(jnp.int32, (n, n), 1)
    shift = CHUNK.bit_length() - 1
    return ((row >> shift) == (col >> shift)) & ((row <= col) if reverse else (row >= col))


def _chunk_cumsum(tri, g):
    tri_b = jnp.where(tri, 1.0, 0.0).astype(BF16)
    g_hi = g.astype(BF16)
    g_lo = (g - g_hi.astype(F32)).astype(BF16)
    return _dot(tri_b, g_hi) + _dot(tri_b, g_lo)


def _gla_gate_kernel(x_ref, sh_ref, sc_ref, wr_ref, w2_ref, b2_ref, o_ref):
    h = (x_ref[...] * (1.0 + sc_ref[0]) + sh_ref[0]).astype(BF16)
    r = _dot(h, wr_ref[...]).astype(BF16)
    pre = _dot(r, w2_ref[...]) + b2_ref[...]
    g = _log_sigmoid(pre) * (1.0 / GLA_NORMALIZER)
    n, half = g.shape[0], g.shape[1] // 2
    o_ref[:, :half] = _chunk_cumsum(_chunk_tri(n, False), g[:, :half])
    o_ref[:, half:] = _chunk_cumsum(_chunk_tri(n, True), g[:, half:])


def _gla_gates(rs, x, mods, layer, w_r, w2, b2):
    rows, d = x.shape
    rt = rs.tile(256)
    n = w2.shape[1]
    return pl.pallas_call(
        _gla_gate_kernel,
        grid=(rows // rt,),
        in_specs=[pl.BlockSpec((rt, d), lambda i: (i, 0)),
                  pl.BlockSpec((1, 1, d), rs.mod_map(layer, 0, rt)),
                  pl.BlockSpec((1, 1, d), rs.mod_map(layer, 1, rt)),
                  pl.BlockSpec(w_r.shape, lambda i: (0, 0)),
                  pl.BlockSpec(w2.shape, lambda i: (0, 0)),
                  pl.BlockSpec((1, n), lambda i: (0, 0))],
        out_specs=pl.BlockSpec((rt, n), lambda i: (i, 0)),
        out_shape=jax.ShapeDtypeStruct((rows, n), F32),
        compiler_params=_params(("arbitrary",), 40),
        name="gla_gates",
    )(x, mods, mods, w_r, w2, b2)


def _gla_scan_kernel(*refs, hgrn, reverse, final, heads, group, dk, dv, n_chunks, qscale):
    if hgrn:
        q_ref, f_ref, v_ref, lb_ref = refs[:4]
        rest = refs[4:]
    else:
        q_ref, k_ref, v_ref, g_ref = refs[:4]
        rest = refs[4:]
    if final:
        og_ref, of_ref, gain_ref, o_ref, st_ref = rest
    else:
        o_ref, st_ref = rest

    @pl.when(pl.program_id(1) == 0)
    def _():
        st_ref[...] = jnp.zeros_like(st_ref)

    tc = n_chunks * CHUNK
    tri = _chunk_tri(tc, reverse)
    order = range(n_chunks - 1, -1, -1) if reverse else range(n_chunks)

    def intra(h):
        kcols = pl.ds(pl.multiple_of(h * dk, dk), dk)
        vcols = pl.ds(pl.multiple_of(h * dv, dv), dv)
        qf = q_ref[:, kcols].astype(F32)
        if hgrn:
            lb = lb_ref[:, kcols]
            s = jax.nn.sigmoid(f_ref[:, kcols].astype(F32))
            qf = qf * jax.nn.sigmoid(qf)
            kf = (1.0 - lb) * (1.0 - s)
            b = _chunk_cumsum(tri, jnp.log(lb + (1.0 - lb) * s))
        else:
            kf = k_ref[:, kcols].astype(F32)
            b = g_ref[:, kcols]
        v = v_ref[:, vcols]
        qd = (qf * (qscale * jnp.exp(b))).astype(BF16)
        kdf = kf * jnp.exp(-b)
        att = jnp.where(tri, _dot_nt(qd, kdf.astype(BF16)), 0.0).astype(BF16)
        return vcols, b, qd, kdf, v, _dot(att, v)

    def finish(vcols, outs):
        o = jnp.concatenate(outs, axis=0)
        if final:
            o = o + of_ref[:, vcols]
            o = o * lax.rsqrt(jnp.mean(o * o, axis=-1, keepdims=True) + RMS_EPS) * gain_ref[...]
            og = og_ref[:, vcols].astype(F32)
            o = o * (og * jax.nn.sigmoid(og))
        o_ref[:, vcols] = o.astype(o_ref.dtype)

    def head_group(i, carry):
        hs = [i * group + j for j in range(group)]
        parts = [intra(h) for h in hs]
        states = [st_ref[h] for h in hs]
        outs = [[None] * n_chunks for _ in hs]
        for c in order:
            rows = slice(c * CHUNK, (c + 1) * CHUNK)
            end = c * CHUNK if reverse else (c + 1) * CHUNK - 1
            for j, (_, b, qd, kdf, v, o_intra) in enumerate(parts):
                st = states[j]
                outs[j][c] = o_intra[rows] + _dot_nt(qd[rows], st.astype(BF16))
                dec = jnp.exp(b[end:end + 1])
                states[j] = st * dec + _dot_tn(v[rows], (kdf[rows] * dec).astype(BF16))
        for j, h in enumerate(hs):
            st_ref[h] = states[j]
            finish(parts[j][0], outs[j])
        return carry

    lax.fori_loop(0, heads // group, head_group, 0)


def _gla_scan(rs, hgrn, reverse, qkv, col_blocks, extra, heads, dk, dv, qscale, final_args=None):
    tc = min(256, rs.n_ctx)
    ctx_blocks, lat_blocks = rs.n_ctx // tc, rs.n_lat // tc
    steps = ctx_blocks + lat_blocks
    kd_w, vd_w = heads * dk, heads * dv

    def row_block(b, s):
        if reverse:
            ctx = b * ctx_blocks + (ctx_blocks - 1 - s)
            lat = rs.ctx_rows // tc + b * lat_blocks + (steps - 1 - s)
        else:
            ctx = b * ctx_blocks + s
            lat = rs.ctx_rows // tc + b * lat_blocks + (s - ctx_blocks)
        return jnp.where(s < ctx_blocks, ctx, lat)

    def spec(width, cb):
        return pl.BlockSpec((tc, width), lambda b, s: (row_block(b, s), cb))

    in_specs = [spec(kd_w, qkv[0][1]), spec(kd_w, qkv[1][1]), spec(vd_w, qkv[2][1])]
    args = [qkv[0][0], qkv[1][0], qkv[2][0]]
    if hgrn:
        in_specs.append(pl.BlockSpec((1, kd_w), lambda b, s: (0, 0)))
    else:
        in_specs.append(spec(kd_w, col_blocks))
    args.append(extra)
    final = final_args is not None
    if final:
        z, gcb, o_fwd, gain = final_args
        in_specs += [spec(vd_w, gcb), spec(vd_w, 0), pl.BlockSpec((1, dv), lambda b, s: (0, 0))]
        args += [z, o_fwd, gain]
    return pl.pallas_call(
        functools.partial(_gla_scan_kernel, hgrn=hgrn, reverse=reverse, final=final, heads=heads,
                          group=8 if heads % 8 == 0 else 4 if heads % 4 == 0 else 2 - heads % 2,
                          dk=dk, dv=dv, n_chunks=tc // CHUNK, qscale=qscale),
        grid=(rs.batch, steps),
        in_specs=in_specs,
        out_specs=spec(vd_w, 0),
        out_shape=jax.ShapeDtypeStruct((rs.rows, vd_w), BF16 if final else F32),
        scratch_shapes=[pltpu.VMEM((heads, dv, dk), F32)],
        compiler_params=_params(("arbitrary", "arbitrary"), 40),
        name=("hgrn" if hgrn else "gla") + ("_bwd" if reverse else "_fwd"),
    )(*args)


def _lru_scan_kernel(*refs, reverse, final, n_blk, bw):
    if final:
        xc_ref, wg_ref, bg_ref, ls_ref, hf_ref, y_ref, o_ref, a_ref, u_ref, h_ref = refs
    else:
        xc_ref, wg_ref, bg_ref, ls_ref, o_ref, a_ref, u_ref, h_ref = refs
    tc, w = xc_ref.shape
    groups = tc // 8

    @pl.when(pl.program_id(1) == 0)
    def _():
        h_ref[...] = jnp.zeros_like(h_ref)

    for n in range(n_blk):
        cols = slice(n * bw, (n + 1) * bw)
        xb = xc_ref[:, cols]
        gr = jax.nn.sigmoid(_dot(xb, wg_ref[0, n]) + bg_ref[0:1, cols])
        gi = jax.nn.sigmoid(_dot(xb, wg_ref[1, n]) + bg_ref[1:2, cols])
        log_a = RG_C * gr * ls_ref[:, cols]
        a = jnp.exp(log_a)
        a_ref[:, cols] = a
        u_ref[:, cols] = jnp.sqrt(-jnp.tanh(log_a) * (a * a + 1.0)) * gi * xb.astype(F32)

    sub = lax.broadcasted_iota(jnp.int32, (8, w), 0)

    def group(i, h):
        g = (groups - 1 - i) if reverse else i
        rows = pl.ds(pl.multiple_of(g * 8, 8), 8)
        a, u = a_ref[rows, :], u_ref[rows, :]
        for k in (1, 2, 4):
            if reverse:
                ok = sub < 8 - k
                a_s, u_s = pltpu.roll(a, 8 - k, 0), pltpu.roll(u, 8 - k, 0)
            else:
                ok = sub >= k
                a_s, u_s = pltpu.roll(a, k, 0), pltpu.roll(u, k, 0)
            u = jnp.where(ok, a * u_s + u, u)
            a = jnp.where(ok, a * a_s, a)
        out = a * h + u
        last = out[0:1] if reverse else out[7:8]
        res = out
        if final:
            res = (out + hf_ref[rows, :]) * y_ref[rows, :].astype(F32)
        o_ref[rows, :] = res.astype(o_ref.dtype)
        return jnp.broadcast_to(last, (8, w))

    h_ref[...] = lax.fori_loop(0, groups, group, h_ref[...])


def _lru_scan(rs, reverse, xc, w_gate, b_gate, log_sig_lam, final_args=None):
    tc = min(256, rs.n_ctx)
    ctx_blocks, lat_blocks = rs.n_ctx // tc, rs.n_lat // tc
    steps = ctx_blocks + lat_blocks
    w = xc.shape[1]
    n_blk, bw = w_gate.shape[1], w_gate.shape[2]

    def row_block(b, s):
        if reverse:
            ctx = b * ctx_blocks + (ctx_blocks - 1 - s)
            lat = rs.ctx_rows // tc + b * lat_blocks + (steps - 1 - s)
        else:
            ctx = b * ctx_blocks + s
            lat = rs.ctx_rows // tc + b * lat_blocks + (s - ctx_blocks)
        return jnp.where(s < ctx_blocks, ctx, lat)

    stream = pl.BlockSpec((tc, w), lambda b, s: (row_block(b, s), 0))
    in_specs = [stream,
                pl.BlockSpec(w_gate.shape, lambda b, s: (0, 0, 0, 0)),
                pl.BlockSpec((2, w), lambda b, s: (0, 0)),
                pl.BlockSpec((1, w), lambda b, s: (0, 0))]
    args = [xc, w_gate, b_gate, log_sig_lam]
    final = final_args is not None
    if final:
        in_specs += [stream, stream]
        args += list(final_args)
    return pl.pallas_call(
        functools.partial(_lru_scan_kernel, reverse=reverse, final=final, n_blk=n_blk, bw=bw),
        grid=(rs.batch, steps),
        in_specs=in_specs,
        out_specs=stream,
        out_shape=jax.ShapeDtypeStruct((rs.rows, w), BF16 if final else F32),
        scratch_shapes=[pltpu.VMEM((tc, w), F32), pltpu.VMEM((tc, w), F32), pltpu.VMEM((8, w), F32)],
        compiler_params=_params(("arbitrary", "arbitrary"), 40),
        name="lru_bwd" if reverse else "lru_fwd",
    )(*args)


def _layer_norm(z, g, b):
    mu = jnp.mean(z, axis=-1, keepdims=True)
    zc = z - mu
    var = jnp.mean(zc * zc, axis=-1, keepdims=True)
    return zc * lax.rsqrt(var + LN_EPS) * g + b


def _pack_words(v):
    half = v.shape[1] // 2
    vb = v.astype(BF16).astype(F32)
    lo = lax.bitcast_convert_type(vb[:, :half], jnp.uint32) >> 16
    hi = lax.bitcast_convert_type(vb[:, half:], jnp.uint32) & jnp.uint32(0xFFFF0000)
    return hi | lo


def _store_packed(ref, row0, n, words):
    s_rows = words.shape[1] // 128
    for s in range(s_rows):
        ref[pl.ds(row0 * s_rows + s, n, stride=s_rows), :] = words[:, s * 128:(s + 1) * 128]


def _load_packed(ref, row0, n, s_rows):
    lo, hi = [], []
    for s in range(s_rows):
        w = ref[pl.ds(row0 * s_rows + s, n, stride=s_rows), :]
        lo.append(lax.bitcast_convert_type(w << 16, F32))
        hi.append(lax.bitcast_convert_type(w & jnp.uint32(0xFFFF0000), F32))
    return lo, hi


def _out_proj_kernel(a_ref, w_ref, x_ref, gate_ref, sh_ref, sc_ref, lg_ref, lb_ref, wr_ref, br_ref,
                     xo_ref, hp_ref, lt_ref, ya_ref, yb_ref, *, alpha):
    i = pl.program_id(0)

    @pl.when(i == 0)
    def _():
        yb_ref[...] = jnp.zeros_like(yb_ref)

    def step(prev_ref, new_ref):
        new_ref[...] = _dot(a_ref[...], w_ref[...])
        x_new = _layer_norm(alpha * x_ref[...] + gate_ref[0] * prev_ref[...], lg_ref[...], lb_ref[...])
        xo_ref[...] = x_new
        h = x_new * (1.0 + sc_ref[0]) + sh_ref[0]
        _store_packed(hp_ref, 0, h.shape[0], _pack_words(h))
        lt_ref[...] = _dot_nt(wr_ref[...], h.astype(BF16)) + br_ref[...]

    @pl.when(i % 2 == 0)
    def _():
        step(yb_ref, ya_ref)

    @pl.when(i % 2 == 1)
    def _():
        step(ya_ref, yb_ref)


def _out_proj(rs, a, w_out, x, mods, layer, ln_g, ln_b, w_router_t, b_router, alpha, skip_rows=0):
    rows, d = rs.rows, x.shape[1]
    k = a.shape[1]
    n_e = w_router_t.shape[0]
    rt = rs.tile(256)
    n_tiles = rows // rt
    skip = skip_rows // rt
    lag = lambda i: jnp.maximum(i - 1, 0)
    row = lambda i: (lag(i), 0)
    const = lambda i: (0, 0)
    mod = lambda which: (lambda i: rs.mod_map(layer, which, rt)(lag(i)))
    return pl.pallas_call(
        functools.partial(_out_proj_kernel, alpha=alpha),
        grid=(n_tiles + 1,),
        in_specs=[pl.BlockSpec((rt, k), lambda i: (jnp.minimum(i, n_tiles - 1) + skip, 0)),
                  pl.BlockSpec((k, d), const),
                  pl.BlockSpec((rt, d), lambda i: (lag(i) + skip, 0)),
                  pl.BlockSpec((1, 1, d), mod(2)),
                  pl.BlockSpec((1, 1, d), mod(3)),
                  pl.BlockSpec((1, 1, d), mod(4)),
                  pl.BlockSpec((1, d), const),
                  pl.BlockSpec((1, d), const),
                  pl.BlockSpec((n_e, d), const),
                  pl.BlockSpec((n_e, 1), const)],
        out_specs=[pl.BlockSpec((rt, d), row),
                   pl.BlockSpec((rt * (d // 256), 128), row),
                   pl.BlockSpec((n_e, rt), lambda i: (0, lag(i)))],
        out_shape=[jax.ShapeDtypeStruct((rows, d), F32),
                   jax.ShapeDtypeStruct((rows * (d // 256), 128), jnp.uint32),
                   jax.ShapeDtypeStruct((n_e, rows), F32)],
        scratch_shapes=[pltpu.VMEM((rt, d), F32), pltpu.VMEM((rt, d), F32)],
        compiler_params=_params(("arbitrary",), 52),
        name="out_proj_ln",
    )(a, w_out, x, mods, mods, mods, ln_g, ln_b, w_router_t, b_router)


def _route_kernel(lt_ref, e_ref, g_ref, r_ref, cnt_ref, carry_ref):
    n_e, tr = lt_ref.shape

    @pl.when(pl.program_id(0) == 0)
    def _():
        carry_ref[...] = jnp.zeros_like(carry_ref)

    lg = lt_ref[...]
    eid = lax.broadcasted_iota(jnp.int32, (n_e, tr), 0)
    vals, idxs, sels = [], [], []
    for _ in range(TOP_K):
        m = jnp.max(lg, axis=0, keepdims=True)
        idx = jnp.min(jnp.where(lg == m, eid, n_e), axis=0, keepdims=True)
        sel = eid == idx
        vals.append(m)
        idxs.append(idx)
        sels.append(sel)
        lg = jnp.where(sel, -jnp.inf, lg)
    ex = [jnp.exp(v - vals[0]) for v in vals]
    denom = ex[0] + ex[1] + ex[2] + ex[3]
    chosen = sels[0] | sels[1] | sels[2] | sels[3]
    member = jnp.where(chosen, 1.0, 0.0)
    srow = lax.broadcasted_iota(jnp.int32, (tr, tr), 0)
    scol = lax.broadcasted_iota(jnp.int32, (tr, tr), 1)
    before = jnp.where(srow < scol, 1.0, 0.0).astype(BF16)
    prefix = _dot(member.astype(BF16), before) + carry_ref[...]
    for k in range(TOP_K):
        e_ref[k:k + 1, :] = idxs[k]
        g_ref[k:k + 1, :] = ex[k] / denom
        r_ref[k:k + 1, :] = jnp.sum(jnp.where(sels[k], prefix, 0.0), axis=0, keepdims=True).astype(jnp.int32)
    carry_ref[...] = carry_ref[...] + jnp.sum(member, axis=1, keepdims=True)
    cnt_ref[...] = carry_ref[...]


def _route(logits_t):
    n_e, rows = logits_t.shape
    tr = _pick(rows, 512)
    tok = pl.BlockSpec((TOP_K, tr), lambda i: (0, i))
    return pl.pallas_call(
        _route_kernel,
        grid=(rows // tr,),
        in_specs=[pl.BlockSpec((n_e, tr), lambda i: (0, i))],
        out_specs=[tok, tok, tok, pl.BlockSpec((n_e, 1), lambda i: (0, 0))],
        out_shape=[jax.ShapeDtypeStruct((TOP_K, rows), jnp.int32),
                   jax.ShapeDtypeStruct((TOP_K, rows), F32),
                   jax.ShapeDtypeStruct((TOP_K, rows), jnp.int32),
                   jax.ShapeDtypeStruct((n_e, 1), F32)],
        scratch_shapes=[pltpu.VMEM((n_e, 1), F32)],
        compiler_params=_params(("arbitrary",), 32),
        name="route",
    )(logits_t)


def _row_slab(ref, row, s_rows):
    return ref.at[pl.ds(pl.multiple_of(row * s_rows, s_rows), s_rows), :]


def _dispatch_kernel(pe_ref, pd_ref, dest_ref, hp_ref, xg_ref, idx_ref, zero_ref, sem, idx_sem,
                     *, rt, bm, s_rows, n_e, n_blocks):
    @pl.when(pl.program_id(0) == 0)
    def _():
        zero_ref[...] = jnp.zeros_like(zero_ref)
        def fill(last_row):
            first = pl.multiple_of((last_row - bm) * s_rows, s_rows)
            return pltpu.make_async_copy(zero_ref, xg_ref.at[pl.ds(first, bm * s_rows), :], sem)

        def start(e, c):
            @pl.when(pd_ref[e] > 0)
            def _():
                fill(pe_ref[e]).start()
            return c

        def wait(e, c):
            @pl.when(pd_ref[e] > 0)
            def _():
                fill(pe_ref[e]).wait()
            return c

        lax.fori_loop(0, n_e, start, 0)
        lax.fori_loop(0, n_e, wait, 0)
        n_used = pe_ref[n_e - 1] // bm
        lax.fori_loop(n_used, n_blocks, lambda j, c: (fill((j + 1) * bm).start(), c)[1], 0)
        lax.fori_loop(n_used, n_blocks, lambda j, c: (fill((j + 1) * bm).wait(), c)[1], 0)

    load = pltpu.make_async_copy(dest_ref, idx_ref, idx_sem)
    load.start()
    load.wait()

    def token(t, c):
        src = _row_slab(hp_ref, t, s_rows)
        for k in range(TOP_K):
            pltpu.make_async_copy(src, _row_slab(xg_ref, idx_ref[k, t], s_rows), sem).start(priority=k % 2)
        return c

    lax.fori_loop(0, rt, token, 0)
    for _ in range(TOP_K):
        pltpu.make_async_copy(hp_ref, xg_ref.at[pl.ds(0, rt * s_rows), :], sem).wait()


def _dispatch(hp, dest, pad_end, padded, n_rows, bm, rt):
    s_rows = hp.shape[0] // dest.shape[1]
    rows = dest.shape[1]
    n_e = pad_end.shape[0]
    grid_spec = pltpu.PrefetchScalarGridSpec(
        num_scalar_prefetch=2,
        grid=(rows // rt,),
        in_specs=[pl.BlockSpec((TOP_K, rt), lambda i, pe, pd: (0, i)),
                  pl.BlockSpec((rt * s_rows, 128), lambda i, pe, pd: (i, 0))],
        out_specs=pl.BlockSpec(memory_space=pl.ANY),
        scratch_shapes=[pltpu.SMEM((TOP_K, rt), jnp.int32),
                        pltpu.VMEM((bm * s_rows, 128), jnp.uint32),
                        pltpu.SemaphoreType.DMA,
                        pltpu.SemaphoreType.DMA],
    )
    return pl.pallas_call(
        functools.partial(_dispatch_kernel, rt=rt, bm=bm, s_rows=s_rows, n_e=n_e, n_blocks=n_rows // bm),
        grid_spec=grid_spec,
        out_shape=jax.ShapeDtypeStruct((n_rows * s_rows, 128), jnp.uint32),
        compiler_params=_params(("arbitrary",), 32),
        name="dispatch",
    )(pad_end, padded, dest, hp)


N_STAGE = 8

def _expert_kernel(be_ref, nb_ref, nx_ref, c0_ref, c1_ref, par_ref, nv_ref,
                   x_ref, wgu_hbm, bgu_ref, wd_hbm, bd_ref, o_ref,
                   wgu_s, wd_s, xb_s, st_ref, sem,
                   *, layer, d_ff, bm, s_rows, wrows, n_gu, n_d):
    i = pl.program_id(0)
    n_chunks = n_gu + n_d

    ff2, d = wgu_s.shape[2], wd_s.shape[2]

    def gu_copy(e, c):
        rows = pl.ds(pl.multiple_of(c * wrows, wrows), wrows)
        slot = c % N_STAGE
        return pltpu.make_async_copy(wgu_hbm.at[layer, e, rows, :], st_ref.at[slot, :, pl.ds(0, ff2)], sem.at[slot])

    def down_copy(e, c):
        rows = pl.ds(pl.multiple_of((c - n_gu) * wrows, wrows), wrows)
        slot = c % N_STAGE
        return pltpu.make_async_copy(wd_hbm.at[layer, e, rows, :], st_ref.at[slot, :, pl.ds(0, d)], sem.at[slot])

    def start(e, c):
        @pl.when(c < n_gu)
        def _():
            gu_copy(e, c).start()

        @pl.when((c >= n_gu) & (c < n_chunks))
        def _():
            down_copy(e, c).start()

    def land(e, c, buf):
        @pl.when(c < n_gu)
        def _():
            gu_copy(e, c).wait()
            rows = pl.ds(pl.multiple_of(c * wrows, wrows), wrows)
            wgu_s[buf, rows, :] = st_ref[c % N_STAGE, :, 0:ff2].astype(BF16)

        @pl.when(c >= n_gu)
        def _():
            down_copy(e, c).wait()
            rows = pl.ds(pl.multiple_of((c - n_gu) * wrows, wrows), wrows)
            wd_s[buf, rows, :] = st_ref[c % N_STAGE, :, 0:d].astype(BF16)

    def prime(e):
        for c in range(N_STAGE - 1):
            start(e, jnp.int32(c))

    def stream(e, lo, hi, buf):
        def body(c, carry):
            start(e, c + (N_STAGE - 1))
            land(e, c, buf)
            return carry
        lax.fori_loop(lo, hi, body, 0)

    @pl.when(i < nb_ref[0])
    def _():
        e, nxt, cur = be_ref[i], nx_ref[i], par_ref[i]
        first = (i == 0) | (e != be_ref[jnp.maximum(i - 1, 0)])

        @pl.when(i == 0)
        def _():
            prime(e)
            stream(e, 0, n_chunks, cur)

        @pl.when(first & (nxt >= 0))
        def _():
            prime(nxt)

        @pl.when(nxt >= 0)
        def _():
            stream(nxt, c0_ref[i], c1_ref[i], 1 - cur)

        half = s_rows * 128

        def ffn(rows):
            lo, hi = _load_packed(x_ref, 0, rows, s_rows)
            for s in range(s_rows):
                xb_s[0:rows, s * 128:(s + 1) * 128] = lo[s].astype(BF16)
                xb_s[0:rows, half + s * 128:half + (s + 1) * 128] = hi[s].astype(BF16)
            gu = _dot(xb_s[0:rows, :], wgu_s[cur]) + bgu_ref[0]
            g = jnp.minimum(gu[:, :d_ff], SWIGLU_LIMIT)
            u = jnp.clip(gu[:, d_ff:], -SWIGLU_LIMIT, SWIGLU_LIMIT)
            act = (g * jax.nn.sigmoid(SWIGLU_ALPHA * g) * (u + 1.0)).astype(BF16)
            y = _dot(act, wd_s[cur]) + bd_ref[0]
            _store_packed(o_ref, 0, rows, _pack_words(y))
            if rows < bm:
                o_ref[rows * s_rows:bm * s_rows, :] = jnp.zeros(((bm - rows) * s_rows, 128), o_ref.dtype)

        @pl.when(nv_ref[i] > bm // 2)
        def _():
            ffn(bm)

        @pl.when(nv_ref[i] <= bm // 2)
        def _():
            ffn(bm // 2)

    @pl.when(i >= nb_ref[0])
    def _():
        o_ref[...] = jnp.zeros_like(o_ref)


def _experts(xg, counts, pad_end, padded, layer, w_gu, b_gu, w_down, b_down, bm):
    depth, n_e, d, ff2 = w_gu.shape
    d_ff = ff2 // 2
    s_rows = d // 256
    n_blocks = xg.shape[0] // (bm * s_rows)
    wrows = _pick(d_ff, 128)
    n_gu, n_d = d // wrows, d_ff // wrows
    n_chunks = n_gu + n_d

    experts = jnp.arange(n_e, dtype=jnp.int32)
    block_start = jnp.arange(n_blocks, dtype=jnp.int32) * bm
    block_e = jnp.minimum(jnp.sum(pad_end[None, :] <= block_start[:, None], axis=1), n_e - 1).astype(jnp.int32)
    n_used = (pad_end[-1:] // bm).astype(jnp.int32)
    onehot = block_e[:, None] == experts[None, :]
    pick = lambda table: jnp.sum(jnp.where(onehot, table[None, :], 0), axis=1)
    blocks_of = padded // bm
    local = jnp.arange(n_blocks, dtype=jnp.int32) - pick((pad_end - padded) // bm)
    nblk = pick(blocks_of)
    later_blocks = jnp.maximum(nblk - 1, 1)
    share = lambda l: jnp.where(nblk > 1, (n_chunks * jnp.maximum(l, 0)) // later_blocks, n_chunks * (l + 1))
    c0, c1 = share(local - 1), share(local)
    live = blocks_of > 0
    later = (experts[None, :] > experts[:, None]) & live[None, :]
    next_of = jnp.min(jnp.where(later, experts[None, :], n_e), axis=1)
    next_of = jnp.where(next_of == n_e, -1, next_of)
    ordinal = jnp.cumsum(live.astype(jnp.int32)) - live.astype(jnp.int32)
    valid = jnp.clip(pick(counts) - local * bm, 0, bm)
    tables = [block_e, n_used, pick(next_of), c0, c1, pick(ordinal) % 2, valid]
    tables = [t.astype(jnp.int32) for t in tables]

    used = lambda i, nb: jnp.minimum(i, nb[0] - 1)
    bias = lambda i, be, nb, *_: (layer, be[used(i, nb)], 0, 0)
    grid_spec = pltpu.PrefetchScalarGridSpec(
        num_scalar_prefetch=len(tables),
        grid=(n_blocks,),
        in_specs=[pl.BlockSpec((bm * s_rows, 128), lambda i, be, nb, *_: (used(i, nb), 0)),
                  pl.BlockSpec(memory_space=pl.ANY),
                  pl.BlockSpec((None, 1, 1, ff2), bias),
                  pl.BlockSpec(memory_space=pl.ANY),
                  pl.BlockSpec((None, 1, 1, d), bias)],
        out_specs=pl.BlockSpec((bm * s_rows, 128), lambda i, *_: (i, 0)),
        scratch_shapes=[pltpu.VMEM((2, d, ff2), BF16), pltpu.VMEM((2, d_ff, d), BF16), pltpu.VMEM((bm, d), BF16),
                        pltpu.VMEM((N_STAGE, wrows, max(ff2, d)), F32),
                        pltpu.SemaphoreType.DMA((N_STAGE,))],
    )
    return pl.pallas_call(
        functools.partial(_expert_kernel, layer=layer, d_ff=d_ff, bm=bm, s_rows=s_rows,
                          wrows=wrows, n_gu=n_gu, n_d=n_d),
        grid_spec=grid_spec,
        out_shape=jax.ShapeDtypeStruct(xg.shape, jnp.uint32),
        compiler_params=_params(("arbitrary",), 58),
        name="experts",
    )(*tables, xg, w_gu, b_gu.reshape(depth, n_e, 1, ff2), w_down, b_down.reshape(depth, n_e, 1, d))


COMBINE_SUB = 32


def _combine_kernel(dcur_ref, dnext_ref, g_ref, x_ref, gate_ref, lg_ref, lb_ref, y_ref, o_ref,
                    idx_ref, buf_ref, sem, idx_sem, *, alpha, rt, s_rows, n_tiles):
    i = pl.program_id(0)
    cur, nxt = i % 2, (i + 1) % 2
    sub = min(COMBINE_SUB, rt)

    def gather(slot, t, k):
        dst = buf_ref.at[slot, pl.ds(pl.multiple_of((k * rt + t) * s_rows, s_rows), s_rows), :]
        return pltpu.make_async_copy(_row_slab(y_ref, idx_ref[slot, k, t], s_rows), dst, sem.at[slot])

    def retire(slot):
        for k in range(TOP_K):
            part = pl.ds(k * rt * s_rows, rt * s_rows)
            pltpu.make_async_copy(y_ref.at[pl.ds(0, rt * s_rows), :], buf_ref.at[slot, part, :], sem.at[slot]).wait()

    def load_indices(src_ref, slot):
        load = pltpu.make_async_copy(src_ref, idx_ref.at[slot], idx_sem)
        load.start()
        load.wait()

    @pl.when(i == 0)
    def _():
        load_indices(dcur_ref, 0)

        def token(t, c):
            for k in range(TOP_K):
                gather(0, t, k).start(priority=k % 2)
            return c
        lax.fori_loop(0, rt, token, 0)

    load_indices(dnext_ref, nxt)
    retire(cur)

    def rows_step(j, carry):
        r0 = pl.multiple_of(j * sub, sub)
        g = g_ref[pl.ds(r0, sub), :]
        lo_parts, hi_parts = [], []
        for s in range(s_rows):
            for tt in range(s * sub // s_rows, (s + 1) * sub // s_rows):
                for k in range(TOP_K):
                    gather(nxt, r0 + tt, k).start(priority=k % 2)
            lo = hi = None
            for k in range(TOP_K):
                w = buf_ref[cur, pl.ds((k * rt + r0) * s_rows + s, sub, stride=s_rows), :]
                gk = g[:, k:k + 1]
                lo_k = gk * lax.bitcast_convert_type(w << 16, F32)
                hi_k = gk * lax.bitcast_convert_type(w & jnp.uint32(0xFFFF0000), F32)
                lo = lo_k if lo is None else lo + lo_k
                hi = hi_k if hi is None else hi + hi_k
            lo_parts.append(lo)
            hi_parts.append(hi)
        acc = jnp.concatenate(lo_parts + hi_parts, axis=1)
        z = alpha * x_ref[pl.ds(r0, sub), :] + gate_ref[0] * acc
        o_ref[pl.ds(r0, sub), :] = _layer_norm(z, lg_ref[...], lb_ref[...])
        return carry

    lax.fori_loop(0, rt // sub, rows_step, 0)

    @pl.when(i == n_tiles - 1)
    def _():
        retire(nxt)


def _combine(rs, y, dest, gates, x, mods, layer, ln_g, ln_b, alpha):
    rows, d = x.shape
    s_rows = d // 256
    rt = rs.tile(256)
    n_tiles = rows // rt
    row = lambda i: (i, 0)
    const = lambda i: (0, 0)
    return pl.pallas_call(
        functools.partial(_combine_kernel, alpha=alpha, rt=rt, s_rows=s_rows, n_tiles=n_tiles),
        grid=(n_tiles,),
        in_specs=[pl.BlockSpec((TOP_K, rt), lambda i: (0, i)),
                  pl.BlockSpec((TOP_K, rt), lambda i: (0, jnp.minimum(i + 1, n_tiles - 1))),
                  pl.BlockSpec((rt, TOP_K), row),
                  pl.BlockSpec((rt, d), row),
                  pl.BlockSpec((1, 1, d), rs.mod_map(layer, 5, rt)),
                  pl.BlockSpec((1, d), const),
                  pl.BlockSpec((1, d), const),
                  pl.BlockSpec(memory_space=pl.ANY)],
        out_specs=pl.BlockSpec((rt, d), row),
        out_shape=jax.ShapeDtypeStruct((rows, d), F32),
        scratch_shapes=[pltpu.SMEM((2, TOP_K, rt), jnp.int32),
                        pltpu.VMEM((2, TOP_K * rt * s_rows, 128), jnp.uint32),
                        pltpu.SemaphoreType.DMA((2,)),
                        pltpu.SemaphoreType.DMA],
        compiler_params=_params(("arbitrary",), 40),
        name="combine_ln",
    )(dest, dest, gates, x, mods, ln_g, ln_b, y)


def _moe(rs, hp, logits_t, x, mods, layer, ln_g, ln_b, w_gu, b_gu, w_down, b_down, alpha):
    rows = x.shape[0]
    n_e = w_gu.shape[1]
    bm = 512 if rows * TOP_K >= 512 * n_e * 4 else 128
    e_t, g_t, r_t, cnt = _route(logits_t)
    counts = cnt[:, 0].astype(jnp.int32)
    padded = (counts + bm - 1) // bm * bm
    pad_end = jnp.cumsum(padded)
    first = pad_end - padded
    experts = jnp.arange(n_e, dtype=jnp.int32)
    first_of = jnp.sum(jnp.where(e_t[None] == experts[:, None, None], first[:, None, None], 0), axis=0)
    dest = first_of + r_t
    n_blocks = -(-(rows * TOP_K + n_e * (bm - 1)) // bm)
    xg = _dispatch(hp, dest, pad_end, padded, n_blocks * bm, bm, rs.tile(256))
    y = _experts(xg, counts, pad_end, padded, layer, w_gu, b_gu, w_down, b_down, bm)
    return _combine(rs, y, dest, g_t.T, x, mods, layer, ln_g, ln_b, alpha)


def kernel(x, c, ctx, c_ctx, ada_w, ada_b, ln_g, ln_b, sc_w_in, sc_conv, sc_w_out, gla_w_in, gla_w_gate2, gla_b_gate, gla_norm, gla_w_out, lru_w_in, lru_conv, lru_conv_b, lru_w_gate, lru_b_gate, lru_lambda, lru_w_out, hg_w_in, hg_lb_raw, hg_norm, hg_w_out, moe_w_router, moe_b_router, moe_w_gu, moe_b_gu, moe_w_down, moe_b_down):
    batch, n_lat, d = x.shape
    n_ctx = ctx.shape[1]
    depth = ada_w.shape[0]
    alpha = (2 * depth) ** 0.25
    assert GRID_W & (GRID_W - 1) == 0 and n_ctx & (n_ctx - 1) == 0 and batch < MOD_ROWS
    rs = _Rows(batch, n_ctx, n_lat)

    cc = jnp.zeros((MOD_ROWS, d), F32).at[:batch].set(c).at[batch].set(c_ctx)
    mods = _adaln(cc, ada_w, ada_b).reshape(depth * MOD_ROWS * N_MOD, 1, d)
    xs = jnp.concatenate([ctx.reshape(-1, d), x.reshape(-1, d)], axis=0)

    for i in range(depth):
        kind, j = i % 4, i // 4
        with_ctx = i < depth - 1
        if kind == 0:
            w_in = sc_w_in[j].astype(BF16)
            a = _in_proj("sconv", rs, xs, mods, i, [(w_in, 0), (w_in, 1), (w_in, 2)], d, [BF16],
                         extra=[sc_conv[j]])
            w_out = sc_w_out[j]
        elif kind == 1:
            dv = gla_norm.shape[-1]
            dk = dv // 2
            heads = gla_w_gate2.shape[-1] // dk
            kd_w, vd_w = heads * dk, heads * dv
            n_main = 2 * kd_w + 2 * vd_w
            rank = gla_w_gate2.shape[2]
            w_in = gla_w_in[j]
            z = _in_proj("plain", rs, xs, mods, i, [(w_in[:, :n_main].astype(BF16), 0)], n_main, [BF16],
                         tn_pref=1024)
            w2 = jnp.zeros((2 * rank, 2 * kd_w), F32)
            w2 = w2.at[:rank, :kd_w].set(gla_w_gate2[j, 0]).at[rank:, kd_w:].set(gla_w_gate2[j, 1])
            gates = _gla_gates(rs, xs, mods, i, w_in[:, n_main:].astype(BF16), w2.astype(BF16),
                               gla_b_gate[j].reshape(1, 2 * kd_w))
            qkv = [(z, 0), (z, 1), (z, kd_w * 2 // vd_w)]
            scale = dk ** -0.5
            o_f = _gla_scan(rs, False, False, qkv, 0, gates, heads, dk, dv, scale)
            a = _gla_scan(rs, False, True, qkv, 1, gates, heads, dk, dv, scale,
                          final_args=(z, (2 * kd_w + vd_w) // vd_w, o_f, gla_norm[j].reshape(1, dv)))
            w_out = gla_w_out[j]
        elif kind == 2:
            w = lru_w_in.shape[-1] // 2
            w_in = lru_w_in[j].astype(BF16)
            y, xc = _in_proj("lru", rs, xs, mods, i, [(w_in, 0), (w_in, 1)], w, [BF16, BF16],
                             extra=[lru_conv[j], lru_conv_b[j].reshape(1, w)])
            log_sig = _log_sigmoid(lru_lambda[j])
            wg, bgate = lru_w_gate[j].astype(BF16), lru_b_gate[j]
            h_f = _lru_scan(rs, False, xc, wg[0], bgate[0], log_sig[0:1])
            a = _lru_scan(rs, True, xc, wg[1], bgate[1], log_sig[1:2], final_args=(h_f, y))
            w_out = lru_w_out[j]
        else:
            dk = hg_norm.shape[-1]
            heads = d // dk
            z = _in_proj("plain", rs, xs, mods, i, [(hg_w_in[j].astype(BF16), 0)], 5 * d, [BF16], tn_pref=1024)
            p = jax.nn.softmax(hg_lb_raw, axis=0)
            lower = (jnp.cumsum(p, axis=0)[i] - p[0]).reshape(1, d)
            scale = dk ** -0.5
            o_f = _gla_scan(rs, True, False, [(z, 0), (z, 1), (z, 3)], 0, lower, heads, dk, dk, scale)
            a = _gla_scan(rs, True, True, [(z, 0), (z, 2), (z, 3)], 0, lower, heads, dk, dk, scale,
                          final_args=(z, 4, o_f, hg_norm[j].reshape(1, dk)))
            w_out = hg_w_out[j]

        skip = 0
        if not with_ctx:
            skip = rs.ctx_rows
            assert skip % _Rows(batch, 0, n_lat).tile(256) == 0
            rs = _Rows(batch, 0, n_lat)
        xs, hp, logits_t = _out_proj(rs, a, w_out.astype(BF16), xs, mods, i,
                                     ln_g[i, 0].reshape(1, d), ln_b[i, 0].reshape(1, d),
                                     moe_w_router[i].T.astype(BF16), moe_b_router[i].reshape(-1, 1), alpha,
                                     skip_rows=skip)
        xs = _moe(rs, hp, logits_t, xs, mods, i, ln_g[i, 1].reshape(1, d), ln_b[i, 1].reshape(1, d),
                  moe_w_gu, moe_b_gu, moe_w_down, moe_b_down, alpha)
    return xs[rs.ctx_rows:].reshape(batch, n_lat, d)
```

```python
import functools

import jax
import jax.numpy as jnp
from jax import lax
from jax.experimental import pallas as pl
from jax.experimental.pallas import tpu as pltpu

F32 = jnp.float32
BF16 = jnp.bfloat16

GRID_W = 64
CHUNK = 64
TOP_K = 4
N_MOD = 6
MOD_ROWS = 8
GLA_NORMALIZER = 16.0
RG_C = 8.0
SWIGLU_LIMIT = 7.0
SWIGLU_ALPHA = 1.702
LN_EPS = 1e-5
RMS_EPS = 1e-6
MIB = 1024 * 1024


def _params(sem, vmem_mib):
    return pltpu.CompilerParams(dimension_semantics=sem, vmem_limit_bytes=vmem_mib * MIB)


def _dot(a, b):
    return jnp.dot(a, b, preferred_element_type=F32)


def _dot_nt(a, b):
    return lax.dot_general(a, b, (((1,), (1,)), ((), ())), preferred_element_type=F32)


def _dot_tn(a, b):
    return lax.dot_general(a, b, (((0,), (0,)), ((), ())), preferred_element_type=F32)


def _log_sigmoid(x):
    return jnp.minimum(x, 0.0) - jnp.log1p(jnp.exp(-jnp.abs(x)))


def _pick(n, pref):
    t = min(n, pref)
    while n % t:
        t //= 2
    return t


class _Rows:
    def __init__(self, batch, n_ctx, n_lat):
        self.batch, self.n_ctx, self.n_lat = batch, n_ctx, n_lat
        self.ctx_rows = batch * n_ctx
        self.rows = self.ctx_rows + batch * n_lat

    def tile(self, pref):
        t = min(pref, self.n_lat)
        while self.n_lat % t or self.ctx_rows % t:
            t //= 2
        return t

    def mod_map(self, layer, which, rt):
        def index(i, *_):
            r0 = i * rt
            row = jnp.where(r0 < self.ctx_rows, self.batch, (r0 - self.ctx_rows) // self.n_lat)
            return ((layer * MOD_ROWS + row) * N_MOD + which, 0, 0)
        return index


def _adaln_kernel(c_ref, w_ref, b_ref, o_ref):
    c = c_ref[...]
    a = (c * jax.nn.sigmoid(c)).astype(BF16)
    o_ref[0] = _dot(a, w_ref[0].astype(BF16)) + b_ref[0]


def _adaln(cc, ada_w, ada_b):
    depth, d, n = ada_w.shape
    tn = _pick(n, 1024)
    return pl.pallas_call(
        _adaln_kernel,
        grid=(depth, n // tn),
        in_specs=[pl.BlockSpec((MOD_ROWS, d), lambda l, j: (0, 0)),
                  pl.BlockSpec((1, d, tn), lambda l, j: (l, 0, j)),
                  pl.BlockSpec((1, 1, tn), lambda l, j: (l, 0, j))],
        out_specs=pl.BlockSpec((1, MOD_ROWS, tn), lambda l, j: (l, 0, j)),
        out_shape=jax.ShapeDtypeStruct((depth, MOD_ROWS, n), F32),
        compiler_params=_params(("arbitrary", "arbitrary"), 40),
        name="adaln",
    )(cc, ada_w, ada_b.reshape(depth, 1, n))


def _modulate(x_ref, sh_ref, sc_ref, h_ref):
    @pl.when(pl.program_id(1) == 0)
    def _():
        h_ref[...] = (x_ref[...] * (1.0 + sc_ref[0]) + sh_ref[0]).astype(BF16)


def _segment_pos(shape, ctx_tiles, seg_ctx, seg_lat):
    mask = jnp.where(pl.program_id(0) < ctx_tiles, seg_ctx - 1, seg_lat - 1)
    return lax.broadcasted_iota(jnp.int32, shape, 0) & mask, mask


def _shift_rows(p, pos, seg_last, offset):
    rt = p.shape[0]
    rolled = pltpu.roll(p, (-offset) % rt, 0)
    ok = (pos + offset >= 0) & (pos + offset <= seg_last)
    return jnp.where(ok, rolled, 0.0)


def _col_parts(width, part=256):
    part = min(part, width)
    return [slice(c, c + part) for c in range(0, width, part)]


def _proj_kernel(x_ref, sh_ref, sc_ref, w_ref, o_ref, h_ref):
    _modulate(x_ref, sh_ref, sc_ref, h_ref)
    o_ref[...] = _dot(h_ref[...], w_ref[...]).astype(o_ref.dtype)


def _sconv_in_kernel(x_ref, sh_ref, sc_ref, wb_ref, wc_ref, wv_ref, cw_ref, o_ref, h_ref,
                     *, ctx_tiles, seg_ctx, seg_lat):
    _modulate(x_ref, sh_ref, sc_ref, h_ref)
    h = h_ref[...]
    for cols in _col_parts(o_ref.shape[1]):
        bg = _dot(h, wb_ref[:, cols])
        p = _dot(h, wc_ref[:, cols]) * _dot(h, wv_ref[:, cols])
        pos, last = _segment_pos(p.shape, ctx_tiles, seg_ctx, seg_lat)
        cw = cw_ref[:, cols]
        conv = (cw[0:1] * _shift_rows(p, pos, last, -1) + cw[1:2] * p
                + cw[2:3] * _shift_rows(p, pos, last, 1))
        o_ref[:, cols] = (bg * conv).astype(o_ref.dtype)


def _lru_in_kernel(x_ref, sh_ref, sc_ref, wy_ref, wx_ref, cw_ref, cb_ref, y_ref, xc_ref, h_ref,
                   *, ctx_tiles, seg_ctx, seg_lat):
    _modulate(x_ref, sh_ref, sc_ref, h_ref)
    h = h_ref[...]
    for cols in _col_parts(y_ref.shape[1]):
        y_ref[:, cols] = jax.nn.gelu(_dot(h, wy_ref[:, cols])).astype(y_ref.dtype)
        xb = _dot(h, wx_ref[:, cols])
        pos, last = _segment_pos(xb.shape, ctx_tiles, seg_ctx, seg_lat)
        cw = cw_ref[:, cols]
        conv = (cw[0:1] * _shift_rows(xb, pos, last, -1) + cw[1:2] * xb
                + cw[2:3] * _shift_rows(xb, pos, last, 1) + cw[3:4] * _shift_rows(xb, pos, last, 2))
        xc_ref[:, cols] = (conv + cb_ref[:, cols]).astype(xc_ref.dtype)


def _in_proj(kind, rs, x, mods, layer, weights, n_out, out_dtypes, extra=(), tn_pref=512, rt_pref=1024):
    rows, d = x.shape
    rt = rs.tile(rt_pref)
    tn = _pick(n_out, tn_pref)
    nj = n_out // tn
    in_specs = [pl.BlockSpec((rt, d), lambda i, j: (i, 0)),
                pl.BlockSpec((1, 1, d), rs.mod_map(layer, 0, rt)),
                pl.BlockSpec((1, 1, d), rs.mod_map(layer, 1, rt))]
    args = [x, mods, mods]
    for w, off in weights:
        in_specs.append(pl.BlockSpec((d, tn), functools.partial(lambda i, j, o: (0, o * nj + j), o=off)))
        args.append(w)
    for e in extra:
        in_specs.append(pl.BlockSpec((e.shape[0], tn), lambda i, j: (0, j)))
        args.append(e)
    out_specs = [pl.BlockSpec((rt, tn), lambda i, j: (i, j)) for _ in out_dtypes]
    out_shape = [jax.ShapeDtypeStruct((rows, n_out), dt) for dt in out_dtypes]
    ctx_tiles = rs.ctx_rows // rt
    if kind == "plain":
        body = _proj_kernel
    elif kind == "sconv":
        body = functools.partial(_sconv_in_kernel, ctx_tiles=ctx_tiles, seg_ctx=rs.n_ctx, seg_lat=GRID_W)
    else:
        body = functools.partial(_lru_in_kernel, ctx_tiles=ctx_tiles, seg_ctx=rs.n_ctx, seg_lat=GRID_W)
    single = len(out_dtypes) == 1
    return pl.pallas_call(
        body,
        grid=(rows // rt, nj),
        in_specs=in_specs,
        out_specs=out_specs[0] if single else out_specs,
        out_shape=out_shape[0] if single else out_shape,
        scratch_shapes=[pltpu.VMEM((rt, d), BF16)],
        compiler_params=_params(("arbitrary", "arbitrary"), 52),
        name=kind + "_in_proj",
    )(*args)


def _chunk_tri(n, reverse):
    row = lax.broadcasted_iota(jnp.int32, (n, n), 0)
    col = lax.broadcasted_iota(jnp.int32, (n, n), 1)
    shift = CHUNK.bit_length() - 1
    return ((row >> shift) == (col >> shift)) & ((row <= col) if reverse else (row >= col))


def _chunk_cumsum(tri, g):
    tri_b = jnp.where(tri, 1.0, 0.0).astype(BF16)
    g_hi = g.astype(BF16)
    g_lo = (g - g_hi.astype(F32)).astype(BF16)
    return _dot(tri_b, g_hi) + _dot(tri_b, g_lo)


def _chunk_cumsum_steps(g, reverse):
    n = g.shape[0]
    pos = lax.broadcasted_iota(jnp.int32, g.shape, 0) & (CHUNK - 1)
    k = 1
    while k < CHUNK:
        if reverse:
            g = g + jnp.where(pos < CHUNK - k, pltpu.roll(g, n - k, 0), 0.0)
        else:
            g = g + jnp.where(pos >= k, pltpu.roll(g, k, 0), 0.0)
        k *= 2
    return g


def _gla_gate_kernel(x_ref, sh_ref, sc_ref, wr_ref, w2_ref, b2_ref, o_ref):
    h = (x_ref[...] * (1.0 + sc_ref[0]) + sh_ref[0]).astype(BF16)
    r = _dot(h, wr_ref[...]).astype(BF16)
    pre = _dot(r, w2_ref[...]) + b2_ref[...]
    g = _log_sigmoid(pre) * (1.0 / GLA_NORMALIZER)
    n, half = g.shape[0], g.shape[1] // 2
    o_ref[:, :half] = _chunk_cumsum(_chunk_tri(n, False), g[:, :half])
    o_ref[:, half:] = _chunk_cumsum(_chunk_tri(n, True), g[:, half:])


def _gla_gates(rs, x, mods, layer, w_r, w2, b2):
    rows, d = x.shape
    rt = rs.tile(256)
    n = w2.shape[1]
    return pl.pallas_call(
        _gla_gate_kernel,
        grid=(rows // rt,),
        in_specs=[pl.BlockSpec((rt, d), lambda i: (i, 0)),
                  pl.BlockSpec((1, 1, d), rs.mod_map(layer, 0, rt)),
                  pl.BlockSpec((1, 1, d), rs.mod_map(layer, 1, rt)),
                  pl.BlockSpec(w_r.shape, lambda i: (0, 0)),
                  pl.BlockSpec(w2.shape, lambda i: (0, 0)),
                  pl.BlockSpec((1, n), lambda i: (0, 0))],
        out_specs=pl.BlockSpec((rt, n), lambda i: (i, 0)),
        out_shape=jax.ShapeDtypeStruct((rows, n), F32),
        compiler_params=_params(("arbitrary",), 40),
        name="gla_gates",
    )(x, mods, mods, w_r, w2, b2)


def _gla_scan_kernel(*refs, hgrn, reverse, final, heads, group, dk, dv, n_chunks, qscale):
    if hgrn:
        q_ref, f_ref, v_ref, lb_ref = refs[:4]
        rest = refs[4:]
    else:
        q_ref, k_ref, v_ref, g_ref = refs[:4]
        rest = refs[4:]
    if final:
        og_ref, of_ref, gain_ref, o_ref, st_ref = rest
    else:
        o_ref, st_ref = rest

    @pl.when(pl.program_id(1) == 0)
    def _():
        st_ref[...] = jnp.zeros_like(st_ref)

    tc = n_chunks * CHUNK
    tri = _chunk_tri(tc, reverse)
    order = range(n_chunks - 1, -1, -1) if reverse else range(n_chunks)

    def intra(h):
        kcols = pl.ds(pl.multiple_of(h * dk, dk), dk)
        vcols = pl.ds(pl.multiple_of(h * dv, dv), dv)
        qf = q_ref[:, kcols].astype(F32)
        if hgrn:
            lb = lb_ref[:, kcols]
            s = jax.nn.sigmoid(f_ref[:, kcols].astype(F32))
            qf = qf * jax.nn.sigmoid(qf)
            kf = (1.0 - lb) * (1.0 - s)
            b = _chunk_cumsum_steps(jnp.log(lb + (1.0 - lb) * s), reverse)
        else:
            kf = k_ref[:, kcols].astype(F32)
            b = g_ref[:, kcols]
        v = v_ref[:, vcols]
        qd = (qf * (qscale * jnp.exp(b))).astype(BF16)
        kdf = kf * jnp.exp(-b)
        att = jnp.where(tri, _dot_nt(qd, kdf.astype(BF16)), 0.0).astype(BF16)
        return vcols, b, qd, kdf, v, _dot(att, v)

    def finish(vcols, outs):
        o = jnp.concatenate(outs, axis=0)
        if final:
            o = o + of_ref[:, vcols]
            o = o * lax.rsqrt(jnp.mean(o * o, axis=-1, keepdims=True) + RMS_EPS) * gain_ref[...]
            og = og_ref[:, vcols].astype(F32)
            o = o * (og * jax.nn.sigmoid(og))
        o_ref[:, vcols] = o.astype(o_ref.dtype)

    def head_group(i, carry):
        hs = [i * group + j for j in range(group)]
        parts = [intra(h) for h in hs]
        states = [st_ref[h] for h in hs]
        outs = [[None] * n_chunks for _ in hs]
        for c in order:
            rows = slice(c * CHUNK, (c + 1) * CHUNK)
            end = c * CHUNK if reverse else (c + 1) * CHUNK - 1
            for j, (_, b, qd, kdf, v, o_intra) in enumerate(parts):
                st = states[j]
                outs[j][c] = o_intra[rows] + _dot_nt(qd[rows], st.astype(BF16))
                dec = jnp.exp(b[end:end + 1])
                states[j] = st * dec + _dot_tn(v[rows], (kdf[rows] * dec).astype(BF16))
        for j, h in enumerate(hs):
            st_ref[h] = states[j]
            finish(parts[j][0], outs[j])
        return carry

    lax.fori_loop(0, heads // group, head_group, 0)


def _gla_scan(rs, hgrn, reverse, qkv, col_blocks, extra, heads, dk, dv, qscale, final_args=None):
    tc = min(256, rs.n_ctx)
    ctx_blocks, lat_blocks = rs.n_ctx // tc, rs.n_lat // tc
    steps = ctx_blocks + lat_blocks
    kd_w, vd_w = heads * dk, heads * dv

    def row_block(b, s):
        if reverse:
            ctx = b * ctx_blocks + (ctx_blocks - 1 - s)
            lat = rs.ctx_rows // tc + b * lat_blocks + (steps - 1 - s)
        else:
            ctx = b * ctx_blocks + s
            lat = rs.ctx_rows // tc + b * lat_blocks + (s - ctx_blocks)
        return jnp.where(s < ctx_blocks, ctx, lat)

    def spec(width, cb):
        return pl.BlockSpec((tc, width), lambda b, s: (row_block(b, s), cb))

    in_specs = [spec(kd_w, qkv[0][1]), spec(kd_w, qkv[1][1]), spec(vd_w, qkv[2][1])]
    args = [qkv[0][0], qkv[1][0], qkv[2][0]]
    if hgrn:
        in_specs.append(pl.BlockSpec((1, kd_w), lambda b, s: (0, 0)))
    else:
        in_specs.append(spec(kd_w, col_blocks))
    args.append(extra)
    final = final_args is not None
    if final:
        z, gcb, o_fwd, gain = final_args
        in_specs += [spec(vd_w, gcb), spec(vd_w, 0), pl.BlockSpec((1, dv), lambda b, s: (0, 0))]
        args += [z, o_fwd, gain]
    return pl.pallas_call(
        functools.partial(_gla_scan_kernel, hgrn=hgrn, reverse=reverse, final=final, heads=heads,
                          group=8 if heads % 8 == 0 else 4 if heads % 4 == 0 else 2 - heads % 2,
                          dk=dk, dv=dv, n_chunks=tc // CHUNK, qscale=qscale),
        grid=(rs.batch, steps),
        in_specs=in_specs,
        out_specs=spec(vd_w, 0),
        out_shape=jax.ShapeDtypeStruct((rs.rows, vd_w), BF16 if final else F32),
        scratch_shapes=[pltpu.VMEM((heads, dv, dk), F32)],
        compiler_params=_params(("arbitrary", "arbitrary"), 40),
        name=("hgrn" if hgrn else "gla") + ("_bwd" if reverse else "_fwd"),
    )(*args)


def _lru_scan_kernel(*refs, reverse, final, n_blk, bw):
    if final:
        xc_ref, wg_ref, bg_ref, ls_ref, hf_ref, y_ref, o_ref, a_ref, u_ref, h_ref = refs
    else:
        xc_ref, wg_ref, bg_ref, ls_ref, o_ref, a_ref, u_ref, h_ref = refs
    tc, w = xc_ref.shape
    groups = tc // 8

    @pl.when(pl.program_id(1) == 0)
    def _():
        h_ref[...] = jnp.zeros_like(h_ref)

    for n in range(n_blk):
        cols = slice(n * bw, (n + 1) * bw)
        xb = xc_ref[:, cols]
        gr = jax.nn.sigmoid(_dot(xb, wg_ref[0, n]) + bg_ref[0:1, cols])
        gi = jax.nn.sigmoid(_dot(xb, wg_ref[1, n]) + bg_ref[1:2, cols])
        log_a = RG_C * gr * ls_ref[:, cols]
        a = jnp.exp(log_a)
        a_ref[:, cols] = a
        u_ref[:, cols] = jnp.sqrt(-jnp.tanh(log_a) * (a * a + 1.0)) * gi * xb.astype(F32)

    sub = lax.broadcasted_iota(jnp.int32, (8, w), 0)

    def group(i, h):
        g = (groups - 1 - i) if reverse else i
        rows = pl.ds(pl.multiple_of(g * 8, 8), 8)
        a, u = a_ref[rows, :], u_ref[rows, :]
        for k in (1, 2, 4):
            if reverse:
                ok = sub < 8 - k
                a_s, u_s = pltpu.roll(a, 8 - k, 0), pltpu.roll(u, 8 - k, 0)
            else:
                ok = sub >= k
                a_s, u_s = pltpu.roll(a, k, 0), pltpu.roll(u, k, 0)
            u = jnp.where(ok, a * u_s + u, u)
            a = jnp.where(ok, a * a_s, a)
        out = a * h + u
        last = out[0:1] if reverse else out[7:8]
        res = out
        if final:
            res = (out + hf_ref[rows, :]) * y_ref[rows, :].astype(F32)
        o_ref[rows, :] = res.astype(o_ref.dtype)
        return jnp.broadcast_to(last, (8, w))

    h_ref[...] = lax.fori_loop(0, groups, group, h_ref[...])


def _lru_scan(rs, reverse, xc, w_gate, b_gate, log_sig_lam, final_args=None):
    tc = min(256, rs.n_ctx)
    ctx_blocks, lat_blocks = rs.n_ctx // tc, rs.n_lat // tc
    steps = ctx_blocks + lat_blocks
    w = xc.shape[1]
    n_blk, bw = w_gate.shape[1], w_gate.shape[2]

    def row_block(b, s):
        if reverse:
            ctx = b * ctx_blocks + (ctx_blocks - 1 - s)
            lat = rs.ctx_rows // tc + b * lat_blocks + (steps - 1 - s)
        else:
            ctx = b * ctx_blocks + s
            lat = rs.ctx_rows // tc + b * lat_blocks + (s - ctx_blocks)
        return jnp.where(s < ctx_blocks, ctx, lat)

    stream = pl.BlockSpec((tc, w), lambda b, s: (row_block(b, s), 0))
    in_specs = [stream,
                pl.BlockSpec(w_gate.shape, lambda b, s: (0, 0, 0, 0)),
                pl.BlockSpec((2, w), lambda b, s: (0, 0)),
                pl.BlockSpec((1, w), lambda b, s: (0, 0))]
    args = [xc, w_gate, b_gate, log_sig_lam]
    final = final_args is not None
    if final:
        in_specs += [stream, stream]
        args += list(final_args)
    return pl.pallas_call(
        functools.partial(_lru_scan_kernel, reverse=reverse, final=final, n_blk=n_blk, bw=bw),
        grid=(rs.batch, steps),
        in_specs=in_specs,
        out_specs=stream,
        out_shape=jax.ShapeDtypeStruct((rs.rows, w), BF16 if final else F32),
        scratch_shapes=[pltpu.VMEM((tc, w), F32), pltpu.VMEM((tc, w), F32), pltpu.VMEM((8, w), F32)],
        compiler_params=_params(("arbitrary", "arbitrary"), 40),
        name="lru_bwd" if reverse else "lru_fwd",
    )(*args)


def _layer_norm(z, g, b):
    mu = jnp.mean(z, axis=-1, keepdims=True)
    zc = z - mu
    var = jnp.mean(zc * zc, axis=-1, keepdims=True)
    return zc * lax.rsqrt(var + LN_EPS) * g + b


def _pack_words(v):
    half = v.shape[1] // 2
    vb = v.astype(BF16).astype(F32)
    lo = lax.bitcast_convert_type(vb[:, :half], jnp.uint32) >> 16
    hi = lax.bitcast_convert_type(vb[:, half:], jnp.uint32) & jnp.uint32(0xFFFF0000)
    return hi | lo


def _store_packed(ref, row0, n, words):
    s_rows = words.shape[1] // 128
    for s in range(s_rows):
        ref[pl.ds(row0 * s_rows + s, n, stride=s_rows), :] = words[:, s * 128:(s + 1) * 128]


def _load_packed(ref, row0, n, s_rows):
    lo, hi = [], []
    for s in range(s_rows):
        w = ref[pl.ds(row0 * s_rows + s, n, stride=s_rows), :]
        lo.append(lax.bitcast_convert_type(w << 16, F32))
        hi.append(lax.bitcast_convert_type(w & jnp.uint32(0xFFFF0000), F32))
    return lo, hi


def _out_proj_kernel(a_ref, w_ref, x_ref, gate_ref, sh_ref, sc_ref, lg_ref, lb_ref, wr_ref, br_ref,
                     xo_ref, hp_ref, lt_ref, ya_ref, yb_ref, *, alpha):
    i = pl.program_id(0)

    @pl.when(i == 0)
    def _():
        yb_ref[...] = jnp.zeros_like(yb_ref)

    def step(prev_ref, new_ref):
        new_ref[...] = _dot(a_ref[...], w_ref[...])
        x_new = _layer_norm(alpha * x_ref[...] + gate_ref[0] * prev_ref[...], lg_ref[...], lb_ref[...])
        xo_ref[...] = x_new
        h = x_new * (1.0 + sc_ref[0]) + sh_ref[0]
        _store_packed(hp_ref, 0, h.shape[0], _pack_words(h))
        lt_ref[...] = _dot_nt(wr_ref[...], h.astype(BF16)) + br_ref[...]

    @pl.when(i % 2 == 0)
    def _():
        step(yb_ref, ya_ref)

    @pl.when(i % 2 == 1)
    def _():
        step(ya_ref, yb_ref)


def _out_proj(rs, a, w_out, x, mods, layer, ln_g, ln_b, w_router_t, b_router, alpha, skip_rows=0):
    rows, d = rs.rows, x.shape[1]
    k = a.shape[1]
    n_e = w_router_t.shape[0]
    rt = rs.tile(256)
    n_tiles = rows // rt
    skip = skip_rows // rt
    lag = lambda i: jnp.maximum(i - 1, 0)
    row = lambda i: (lag(i), 0)
    const = lambda i: (0, 0)
    mod = lambda which: (lambda i: rs.mod_map(layer, which, rt)(lag(i)))
    return pl.pallas_call(
        functools.partial(_out_proj_kernel, alpha=alpha),
        grid=(n_tiles + 1,),
        in_specs=[pl.BlockSpec((rt, k), lambda i: (jnp.minimum(i, n_tiles - 1) + skip, 0)),
                  pl.BlockSpec((k, d), const),
                  pl.BlockSpec((rt, d), lambda i: (lag(i) + skip, 0)),
                  pl.BlockSpec((1, 1, d), mod(2)),
                  pl.BlockSpec((1, 1, d), mod(3)),
                  pl.BlockSpec((1, 1, d), mod(4)),
                  pl.BlockSpec((1, d), const),
                  pl.BlockSpec((1, d), const),
                  pl.BlockSpec((n_e, d), const),
                  pl.BlockSpec((n_e, 1), const)],
        out_specs=[pl.BlockSpec((rt, d), row),
                   pl.BlockSpec((rt * (d // 256), 128), row),
                   pl.BlockSpec((n_e, rt), lambda i: (0, lag(i)))],
        out_shape=[jax.ShapeDtypeStruct((rows, d), F32),
                   jax.ShapeDtypeStruct((rows * (d // 256), 128), jnp.uint32),
                   jax.ShapeDtypeStruct((n_e, rows), F32)],
        scratch_shapes=[pltpu.VMEM((rt, d), F32), pltpu.VMEM((rt, d), F32)],
        compiler_params=_params(("arbitrary",), 52),
        name="out_proj_ln",
    )(a, w_out, x, mods, mods, mods, ln_g, ln_b, w_router_t, b_router)


def _route_kernel(lt_ref, e_ref, g_ref, r_ref, cnt_ref, carry_ref):
    n_e, tr = lt_ref.shape

    @pl.when(pl.program_id(0) == 0)
    def _():
        carry_ref[...] = jnp.zeros_like(carry_ref)

    lg = lt_ref[...]
    eid = lax.broadcasted_iota(jnp.int32, (n_e, tr), 0)
    vals, idxs, sels = [], [], []
    for _ in range(TOP_K):
        m = jnp.max(lg, axis=0, keepdims=True)
        idx = jnp.min(jnp.where(lg == m, eid, n_e), axis=0, keepdims=True)
        sel = eid == idx
        vals.append(m)
        idxs.append(idx)
        sels.append(sel)
        lg = jnp.where(sel, -jnp.inf, lg)
    ex = [jnp.exp(v - vals[0]) for v in vals]
    denom = ex[0] + ex[1] + ex[2] + ex[3]
    chosen = sels[0] | sels[1] | sels[2] | sels[3]
    member = jnp.where(chosen, 1.0, 0.0)
    srow = lax.broadcasted_iota(jnp.int32, (tr, tr), 0)
    scol = lax.broadcasted_iota(jnp.int32, (tr, tr), 1)
    before = jnp.where(srow < scol, 1.0, 0.0).astype(BF16)
    prefix = _dot(member.astype(BF16), before) + carry_ref[...]
    for k in range(TOP_K):
        e_ref[k:k + 1, :] = idxs[k]
        g_ref[k:k + 1, :] = ex[k] / denom
        r_ref[k:k + 1, :] = jnp.sum(jnp.where(sels[k], prefix, 0.0), axis=0, keepdims=True).astype(jnp.int32)
    carry_ref[...] = carry_ref[...] + jnp.sum(member, axis=1, keepdims=True)
    cnt_ref[...] = carry_ref[...]


def _route(logits_t):
    n_e, rows = logits_t.shape
    tr = _pick(rows, 512)
    tok = pl.BlockSpec((TOP_K, tr), lambda i: (0, i))
    return pl.pallas_call(
        _route_kernel,
        grid=(rows // tr,),
        in_specs=[pl.BlockSpec((n_e, tr), lambda i: (0, i))],
        out_specs=[tok, tok, tok, pl.BlockSpec((n_e, 1), lambda i: (0, 0))],
        out_shape=[jax.ShapeDtypeStruct((TOP_K, rows), jnp.int32),
                   jax.ShapeDtypeStruct((TOP_K, rows), F32),
                   jax.ShapeDtypeStruct((TOP_K, rows), jnp.int32),
                   jax.ShapeDtypeStruct((n_e, 1), F32)],
        scratch_shapes=[pltpu.VMEM((n_e, 1), F32)],
        compiler_params=_params(("arbitrary",), 32),
        name="route",
    )(logits_t)


def _row_slab(ref, row, s_rows):
    return ref.at[pl.ds(pl.multiple_of(row * s_rows, s_rows), s_rows), :]


def _dispatch_kernel(pe_ref, pd_ref, dest_ref, hp_ref, xg_ref, idx_ref, zero_ref, sem, idx_sem,
                     *, rt, bm, s_rows, n_e, n_blocks):
    @pl.when(pl.program_id(0) == 0)
    def _():
        zero_ref[...] = jnp.zeros_like(zero_ref)
        def fill(last_row):
            first = pl.multiple_of((last_row - bm) * s_rows, s_rows)
            return pltpu.make_async_copy(zero_ref, xg_ref.at[pl.ds(first, bm * s_rows), :], sem)

        def start(e, c):
            @pl.when(pd_ref[e] > 0)
            def _():
                fill(pe_ref[e]).start()
            return c

        def wait(e, c):
            @pl.when(pd_ref[e] > 0)
            def _():
                fill(pe_ref[e]).wait()
            return c

        lax.fori_loop(0, n_e, start, 0)
        lax.fori_loop(0, n_e, wait, 0)
        n_used = pe_ref[n_e - 1] // bm
        lax.fori_loop(n_used, n_blocks, lambda j, c: (fill((j + 1) * bm).start(), c)[1], 0)
        lax.fori_loop(n_used, n_blocks, lambda j, c: (fill((j + 1) * bm).wait(), c)[1], 0)

    load = pltpu.make_async_copy(dest_ref, idx_ref, idx_sem)
    load.start()
    load.wait()

    def token(t, c):
        src = _row_slab(hp_ref, t, s_rows)
        for k in range(TOP_K):
            pltpu.make_async_copy(src, _row_slab(xg_ref, idx_ref[k, t], s_rows), sem).start(priority=k % 2)
        return c

    lax.fori_loop(0, rt, token, 0)
    for _ in range(TOP_K):
        pltpu.make_async_copy(hp_ref, xg_ref.at[pl.ds(0, rt * s_rows), :], sem).wait()


def _dispatch(hp, dest, pad_end, padded, n_rows, bm, rt):
    s_rows = hp.shape[0] // dest.shape[1]
    rows = dest.shape[1]
    n_e = pad_end.shape[0]
    grid_spec = pltpu.PrefetchScalarGridSpec(
        num_scalar_prefetch=2,
        grid=(rows // rt,),
        in_specs=[pl.BlockSpec((TOP_K, rt), lambda i, pe, pd: (0, i)),
                  pl.BlockSpec((rt * s_rows, 128), lambda i, pe, pd: (i, 0))],
        out_specs=pl.BlockSpec(memory_space=pl.ANY),
        scratch_shapes=[pltpu.SMEM((TOP_K, rt), jnp.int32),
                        pltpu.VMEM((bm * s_rows, 128), jnp.uint32),
                        pltpu.SemaphoreType.DMA,
                        pltpu.SemaphoreType.DMA],
    )
    return pl.pallas_call(
        functools.partial(_dispatch_kernel, rt=rt, bm=bm, s_rows=s_rows, n_e=n_e, n_blocks=n_rows // bm),
        grid_spec=grid_spec,
        out_shape=jax.ShapeDtypeStruct((n_rows * s_rows, 128), jnp.uint32),
        compiler_params=_params(("arbitrary",), 32),
        name="dispatch",
    )(pad_end, padded, dest, hp)


N_STAGE = 8

def _expert_kernel(be_ref, nb_ref, nx_ref, c0_ref, c1_ref, par_ref, nv_ref,
                   x_ref, wgu_hbm, bgu_ref, wd_hbm, bd_ref, o_ref,
                   wgu_s, wd_s, xb_s, st_ref, sem,
                   *, layer, d_ff, bm, s_rows, wrows, n_gu, n_d):
    i = pl.program_id(0)
    n_chunks = n_gu + n_d

    ff2, d = wgu_s.shape[2], wd_s.shape[2]

    def gu_copy(e, c):
        rows = pl.ds(pl.multiple_of(c * wrows, wrows), wrows)
        slot = c % N_STAGE
        return pltpu.make_async_copy(wgu_hbm.at[layer, e, rows, :], st_ref.at[slot, :, pl.ds(0, ff2)], sem.at[slot])

    def down_copy(e, c):
        rows = pl.ds(pl.multiple_of((c - n_gu) * wrows, wrows), wrows)
        slot = c % N_STAGE
        return pltpu.make_async_copy(wd_hbm.at[layer, e, rows, :], st_ref.at[slot, :, pl.ds(0, d)], sem.at[slot])

    def start(e, c):
        @pl.when(c < n_gu)
        def _():
            gu_copy(e, c).start()

        @pl.when((c >= n_gu) & (c < n_chunks))
        def _():
            down_copy(e, c).start()

    def land(e, c, buf):
        @pl.when(c < n_gu)
        def _():
            gu_copy(e, c).wait()
            rows = pl.ds(pl.multiple_of(c * wrows, wrows), wrows)
            wgu_s[buf, rows, :] = st_ref[c % N_STAGE, :, 0:ff2].astype(BF16)

        @pl.when(c >= n_gu)
        def _():
            down_copy(e, c).wait()
            rows = pl.ds(pl.multiple_of((c - n_gu) * wrows, wrows), wrows)
            wd_s[buf, rows, :] = st_ref[c % N_STAGE, :, 0:d].astype(BF16)

    def prime(e):
        for c in range(N_STAGE - 1):
            start(e, jnp.int32(c))

    def stream(e, lo, hi, buf):
        def body(c, carry):
            start(e, c + (N_STAGE - 1))
            land(e, c, buf)
            return carry
        lax.fori_loop(lo, hi, body, 0)

    @pl.when(i < nb_ref[0])
    def _():
        e, nxt, cur = be_ref[i], nx_ref[i], par_ref[i]
        first = (i == 0) | (e != be_ref[jnp.maximum(i - 1, 0)])

        @pl.when(i == 0)
        def _():
            prime(e)
            stream(e, 0, n_chunks, cur)

        @pl.when(first & (nxt >= 0))
        def _():
            prime(nxt)

        @pl.when(nxt >= 0)
        def _():
            stream(nxt, c0_ref[i], c1_ref[i], 1 - cur)

        half = s_rows * 128

        def ffn(rows):
            lo, hi = _load_packed(x_ref, 0, rows, s_rows)
            for s in range(s_rows):
                xb_s[0:rows, s * 128:(s + 1) * 128] = lo[s].astype(BF16)
                xb_s[0:rows, half + s * 128:half + (s + 1) * 128] = hi[s].astype(BF16)
            gu = _dot(xb_s[0:rows, :], wgu_s[cur]) + bgu_ref[0]
            g = jnp.minimum(gu[:, :d_ff], SWIGLU_LIMIT)
            u = jnp.clip(gu[:, d_ff:], -SWIGLU_LIMIT, SWIGLU_LIMIT)
            act = (g * jax.nn.sigmoid(SWIGLU_ALPHA * g) * (u + 1.0)).astype(BF16)
            y = _dot(act, wd_s[cur]) + bd_ref[0]
            _store_packed(o_ref, 0, rows, _pack_words(y))
            if rows < bm:
                o_ref[rows * s_rows:bm * s_rows, :] = jnp.zeros(((bm - rows) * s_rows, 128), o_ref.dtype)

        @pl.when(nv_ref[i] > bm // 2)
        def _():
            ffn(bm)

        @pl.when(nv_ref[i] <= bm // 2)
        def _():
            ffn(bm // 2)

    @pl.when(i >= nb_ref[0])
    def _():
        o_ref[...] = jnp.zeros_like(o_ref)


def _experts(xg, counts, pad_end, padded, layer, w_gu, b_gu, w_down, b_down, bm):
    depth, n_e, d, ff2 = w_gu.shape
    d_ff = ff2 // 2
    s_rows = d // 256
    n_blocks = xg.shape[0] // (bm * s_rows)
    wrows = _pick(d_ff, 128)
    n_gu, n_d = d // wrows, d_ff // wrows
    n_chunks = n_gu + n_d

    experts = jnp.arange(n_e, dtype=jnp.int32)
    block_start = jnp.arange(n_blocks, dtype=jnp.int32) * bm
    block_e = jnp.minimum(jnp.sum(pad_end[None, :] <= block_start[:, None], axis=1), n_e - 1).astype(jnp.int32)
    n_used = (pad_end[-1:] // bm).astype(jnp.int32)
    onehot = block_e[:, None] == experts[None, :]
    pick = lambda table: jnp.sum(jnp.where(onehot, table[None, :], 0), axis=1)
    blocks_of = padded // bm
    local = jnp.arange(n_blocks, dtype=jnp.int32) - pick((pad_end - padded) // bm)
    nblk = pick(blocks_of)
    later_blocks = jnp.maximum(nblk - 1, 1)
    share = lambda l: jnp.where(nblk > 1, (n_chunks * jnp.maximum(l, 0)) // later_blocks, n_chunks * (l + 1))
    c0, c1 = share(local - 1), share(local)
    live = blocks_of > 0
    later = (experts[None, :] > experts[:, None]) & live[None, :]
    next_of = jnp.min(jnp.where(later, experts[None, :], n_e), axis=1)
    next_of = jnp.where(next_of == n_e, -1, next_of)
    ordinal = jnp.cumsum(live.astype(jnp.int32)) - live.astype(jnp.int32)
    valid = jnp.clip(pick(counts) - local * bm, 0, bm)
    tables = [block_e, n_used, pick(next_of), c0, c1, pick(ordinal) % 2, valid]
    tables = [t.astype(jnp.int32) for t in tables]

    used = lambda i, nb: jnp.minimum(i, nb[0] - 1)
    bias = lambda i, be, nb, *_: (layer, be[used(i, nb)], 0, 0)
    grid_spec = pltpu.PrefetchScalarGridSpec(
        num_scalar_prefetch=len(tables),
        grid=(n_blocks,),
        in_specs=[pl.BlockSpec((bm * s_rows, 128), lambda i, be, nb, *_: (used(i, nb), 0)),
                  pl.BlockSpec(memory_space=pl.ANY),
                  pl.BlockSpec((None, 1, 1, ff2), bias),
                  pl.BlockSpec(memory_space=pl.ANY),
                  pl.BlockSpec((None, 1, 1, d), bias)],
        out_specs=pl.BlockSpec((bm * s_rows, 128), lambda i, *_: (i, 0)),
        scratch_shapes=[pltpu.VMEM((2, d, ff2), BF16), pltpu.VMEM((2, d_ff, d), BF16), pltpu.VMEM((bm, d), BF16),
                        pltpu.VMEM((N_STAGE, wrows, max(ff2, d)), F32),
                        pltpu.SemaphoreType.DMA((N_STAGE,))],
    )
    return pl.pallas_call(
        functools.partial(_expert_kernel, layer=layer, d_ff=d_ff, bm=bm, s_rows=s_rows,
                          wrows=wrows, n_gu=n_gu, n_d=n_d),
        grid_spec=grid_spec,
        out_shape=jax.ShapeDtypeStruct(xg.shape, jnp.uint32),
        compiler_params=_params(("arbitrary",), 58),
        name="experts",
    )(*tables, xg, w_gu, b_gu.reshape(depth, n_e, 1, ff2), w_down, b_down.reshape(depth, n_e, 1, d))


COMBINE_SUB = 32


def _combine_kernel(dcur_ref, dnext_ref, g_ref, x_ref, gate_ref, lg_ref, lb_ref, y_ref, o_ref,
                    idx_ref, buf_ref, sem, idx_sem, *, alpha, rt, s_rows, n_tiles):
    i = pl.program_id(0)
    cur, nxt = i % 2, (i + 1) % 2
    sub = min(COMBINE_SUB, rt)

    def gather(slot, t, k):
        dst = buf_ref.at[slot, pl.ds(pl.multiple_of((k * rt + t) * s_rows, s_rows), s_rows), :]
        return pltpu.make_async_copy(_row_slab(y_ref, idx_ref[slot, k, t], s_rows), dst, sem.at[slot])

    def retire(slot):
        for k in range(TOP_K):
            part = pl.ds(k * rt * s_rows, rt * s_rows)
            pltpu.make_async_copy(y_ref.at[pl.ds(0, rt * s_rows), :], buf_ref.at[slot, part, :], sem.at[slot]).wait()

    def load_indices(src_ref, slot):
        load = pltpu.make_async_copy(src_ref, idx_ref.at[slot], idx_sem)
        load.start()
        load.wait()

    @pl.when(i == 0)
    def _():
        load_indices(dcur_ref, 0)

        def token(t, c):
            for k in range(TOP_K):
                gather(0, t, k).start(priority=k % 2)
            return c
        lax.fori_loop(0, rt, token, 0)

    load_indices(dnext_ref, nxt)
    retire(cur)

    def rows_step(j, carry):
        r0 = pl.multiple_of(j * sub, sub)
        g = g_ref[pl.ds(r0, sub), :]
        lo_parts, hi_parts = [], []
        for s in range(s_rows):
            for tt in range(s * sub // s_rows, (s + 1) * sub // s_rows):
                for k in range(TOP_K):
                    gather(nxt, r0 + tt, k).start(priority=k % 2)
            lo = hi = None
            for k in range(TOP_K):
                w = buf_ref[cur, pl.ds((k * rt + r0) * s_rows + s, sub, stride=s_rows), :]
                gk = g[:, k:k + 1]
                lo_k = gk * lax.bitcast_convert_type(w << 16, F32)
                hi_k = gk * lax.bitcast_convert_type(w & jnp.uint32(0xFFFF0000), F32)
                lo = lo_k if lo is None else lo + lo_k
                hi = hi_k if hi is None else hi + hi_k
            lo_parts.append(lo)
            hi_parts.append(hi)
        acc = jnp.concatenate(lo_parts + hi_parts, axis=1)
        z = alpha * x_ref[pl.ds(r0, sub), :] + gate_ref[0] * acc
        o_ref[pl.ds(r0, sub), :] = _layer_norm(z, lg_ref[...], lb_ref[...])
        return carry

    lax.fori_loop(0, rt // sub, rows_step, 0)

    @pl.when(i == n_tiles - 1)
    def _():
        retire(nxt)


def _combine(rs, y, dest, gates, x, mods, layer, ln_g, ln_b, alpha):
    rows, d = x.shape
    s_rows = d // 256
    rt = rs.tile(256)
    n_tiles = rows // rt
    row = lambda i: (i, 0)
    const = lambda i: (0, 0)
    return pl.pallas_call(
        functools.partial(_combine_kernel, alpha=alpha, rt=rt, s_rows=s_rows, n_tiles=n_tiles),
        grid=(n_tiles,),
        in_specs=[pl.BlockSpec((TOP_K, rt), lambda i: (0, i)),
                  pl.BlockSpec((TOP_K, rt), lambda i: (0, jnp.minimum(i + 1, n_tiles - 1))),
                  pl.BlockSpec((rt, TOP_K), row),
                  pl.BlockSpec((rt, d), row),
                  pl.BlockSpec((1, 1, d), rs.mod_map(layer, 5, rt)),
                  pl.BlockSpec((1, d), const),
                  pl.BlockSpec((1, d), const),
                  pl.BlockSpec(memory_space=pl.ANY)],
        out_specs=pl.BlockSpec((rt, d), row),
        out_shape=jax.ShapeDtypeStruct((rows, d), F32),
        scratch_shapes=[pltpu.SMEM((2, TOP_K, rt), jnp.int32),
                        pltpu.VMEM((2, TOP_K * rt * s_rows, 128), jnp.uint32),
                        pltpu.SemaphoreType.DMA((2,)),
                        pltpu.SemaphoreType.DMA],
        compiler_params=_params(("arbitrary",), 40),
        name="combine_ln",
    )(dest, dest, gates, x, mods, ln_g, ln_b, y)


def _moe(rs, hp, logits_t, x, mods, layer, ln_g, ln_b, w_gu, b_gu, w_down, b_down, alpha):
    rows = x.shape[0]
    n_e = w_gu.shape[1]
    bm = 512 if rows * TOP_K >= 512 * n_e * 4 else 128
    e_t, g_t, r_t, cnt = _route(logits_t)
    counts = cnt[:, 0].astype(jnp.int32)
    padded = (counts + bm - 1) // bm * bm
    pad_end = jnp.cumsum(padded)
    first = pad_end - padded
    experts = jnp.arange(n_e, dtype=jnp.int32)
    first_of = jnp.sum(jnp.where(e_t[None] == experts[:, None, None], first[:, None, None], 0), axis=0)
    dest = first_of + r_t
    n_blocks = -(-(rows * TOP_K + n_e * (bm - 1)) // bm)
    xg = _dispatch(hp, dest, pad_end, padded, n_blocks * bm, bm, rs.tile(256))
    y = _experts(xg, counts, pad_end, padded, layer, w_gu, b_gu, w_down, b_down, bm)
    return _combine(rs, y, dest, g_t.T, x, mods, layer, ln_g, ln_b, alpha)


def kernel(x, c, ctx, c_ctx, ada_w, ada_b, ln_g, ln_b, sc_w_in, sc_conv, sc_w_out, gla_w_in, gla_w_gate2, gla_b_gate, gla_norm, gla_w_out, lru_w_in, lru_conv, lru_conv_b, lru_w_gate, lru_b_gate, lru_lambda, lru_w_out, hg_w_in, hg_lb_raw, hg_norm, hg_w_out, moe_w_router, moe_b_router, moe_w_gu, moe_b_gu, moe_w_down, moe_b_down):
    batch, n_lat, d = x.shape
    n_ctx = ctx.shape[1]
    depth = ada_w.shape[0]
    alpha = (2 * depth) ** 0.25
    assert GRID_W & (GRID_W - 1) == 0 and n_ctx & (n_ctx - 1) == 0 and batch < MOD_ROWS
    rs = _Rows(batch, n_ctx, n_lat)

    cc = jnp.zeros((MOD_ROWS, d), F32).at[:batch].set(c).at[batch].set(c_ctx)
    mods = _adaln(cc, ada_w, ada_b).reshape(depth * MOD_ROWS * N_MOD, 1, d)
    xs = jnp.concatenate([ctx.reshape(-1, d), x.reshape(-1, d)], axis=0)

    for i in range(depth):
        kind, j = i % 4, i // 4
        with_ctx = i < depth - 1
        if kind == 0:
            w_in = sc_w_in[j].astype(BF16)
            a = _in_proj("sconv", rs, xs, mods, i, [(w_in, 0), (w_in, 1), (w_in, 2)], d, [BF16],
                         extra=[sc_conv[j]])
            w_out = sc_w_out[j]
        elif kind == 1:
            dv = gla_norm.shape[-1]
            dk = dv // 2
            heads = gla_w_gate2.shape[-1] // dk
            kd_w, vd_w = heads * dk, heads * dv
            n_main = 2 * kd_w + 2 * vd_w
            rank = gla_w_gate2.shape[2]
            w_in = gla_w_in[j]
            z = _in_proj("plain", rs, xs, mods, i, [(w_in[:, :n_main].astype(BF16), 0)], n_main, [BF16],
                         tn_pref=1024)
            w2 = jnp.zeros((2 * rank, 2 * kd_w), F32)
            w2 = w2.at[:rank, :kd_w].set(gla_w_gate2[j, 0]).at[rank:, kd_w:].set(gla_w_gate2[j, 1])
            gates = _gla_gates(rs, xs, mods, i, w_in[:, n_main:].astype(BF16), w2.astype(BF16),
                               gla_b_gate[j].reshape(1, 2 * kd_w))
            qkv = [(z, 0), (z, 1), (z, kd_w * 2 // vd_w)]
            scale = dk ** -0.5
            o_f = _gla_scan(rs, False, False, qkv, 0, gates, heads, dk, dv, scale)
            a = _gla_scan(rs, False, True, qkv, 1, gates, heads, dk, dv, scale,
                          final_args=(z, (2 * kd_w + vd_w) // vd_w, o_f, gla_norm[j].reshape(1, dv)))
            w_out = gla_w_out[j]
        elif kind == 2:
            w = lru_w_in.shape[-1] // 2
            w_in = lru_w_in[j].astype(BF16)
            y, xc = _in_proj("lru", rs, xs, mods, i, [(w_in, 0), (w_in, 1)], w, [BF16, BF16],
                             extra=[lru_conv[j], lru_conv_b[j].reshape(1, w)])
            log_sig = _log_sigmoid(lru_lambda[j])
            wg, bgate = lru_w_gate[j].astype(BF16), lru_b_gate[j]
            h_f = _lru_scan(rs, False, xc, wg[0], bgate[0], log_sig[0:1])
            a = _lru_scan(rs, True, xc, wg[1], bgate[1], log_sig[1:2], final_args=(h_f, y))
            w_out = lru_w_out[j]
        else:
            dk = hg_norm.shape[-1]
            heads = d // dk
            z = _in_proj("plain", rs, xs, mods, i, [(hg_w_in[j].astype(BF16), 0)], 5 * d, [BF16], tn_pref=1024)
            p = jax.nn.softmax(hg_lb_raw, axis=0)
            lower = (jnp.cumsum(p, axis=0)[i] - p[0]).reshape(1, d)
            scale = dk ** -0.5
            o_f = _gla_scan(rs, True, False, [(z, 0), (z, 1), (z, 3)], 0, lower, heads, dk, dk, scale)
            a = _gla_scan(rs, True, True, [(z, 0), (z, 2), (z, 3)], 0, lower, heads, dk, dk, scale,
                          final_args=(z, 4, o_f, hg_norm[j].reshape(1, dk)))
            w_out = hg_w_out[j]

        skip = 0
        if not with_ctx:
            skip = rs.ctx_rows
            assert skip % _Rows(batch, 0, n_lat).tile(256) == 0
            rs = _Rows(batch, 0, n_lat)
        xs, hp, logits_t = _out_proj(rs, a, w_out.astype(BF16), xs, mods, i,
                                     ln_g[i, 0].reshape(1, d), ln_b[i, 0].reshape(1, d),
                                     moe_w_router[i].T.astype(BF16), moe_b_router[i].reshape(-1, 1), alpha,
                                     skip_rows=skip)
        xs = _moe(rs, hp, logits_t, xs, mods, i, ln_g[i, 1].reshape(1, d), ln_b[i, 1].reshape(1, d),
                  moe_w_gu, moe_b_gu, moe_w_down, moe_b_down, alpha)
    return xs[rs.ctx_rows:].reshape(batch, n_lat, d)
```

```python
import functools

import jax
import jax.numpy as jnp
from jax import lax
from jax.experimental import pallas as pl
from jax.experimental.pallas import tpu as pltpu

F32 = jnp.float32
BF16 = jnp.bfloat16

GRID_W = 64
CHUNK = 64
TOP_K = 4
N_MOD = 6
MOD_ROWS = 8
GLA_NORMALIZER = 16.0
RG_C = 8.0
SWIGLU_LIMIT = 7.0
SWIGLU_ALPHA = 1.702
LN_EPS = 1e-5
RMS_EPS = 1e-6
MIB = 1024 * 1024


def _params(sem, vmem_mib):
    return pltpu.CompilerParams(dimension_semantics=sem, vmem_limit_bytes=vmem_mib * MIB)


def _dot(a, b):
    return jnp.dot(a, b, preferred_element_type=F32)


def _dot_nt(a, b):
    return lax.dot_general(a, b, (((1,), (1,)), ((), ())), preferred_element_type=F32)


def _dot_tn(a, b):
    return lax.dot_general(a, b, (((0,), (0,)), ((), ())), preferred_element_type=F32)


def _log_sigmoid(x):
    return jnp.minimum(x, 0.0) - jnp.log1p(jnp.exp(-jnp.abs(x)))


def _pick(n, pref):
    t = min(n, pref)
    while n % t:
        t //= 2
    return t


class _Rows:
    def __init__(self, batch, n_ctx, n_lat):
        self.batch, self.n_ctx, self.n_lat = batch, n_ctx, n_lat
        self.ctx_rows = batch * n_ctx
        self.rows = self.ctx_rows + batch * n_lat

    def tile(self, pref):
        t = min(pref, self.n_lat)
        while self.n_lat % t or self.ctx_rows % t:
            t //= 2
        return t

    def mod_map(self, layer, which, rt):
        def index(i, *_):
            r0 = i * rt
            row = jnp.where(r0 < self.ctx_rows, self.batch, (r0 - self.ctx_rows) // self.n_lat)
            return ((layer * MOD_ROWS + row) * N_MOD + which, 0, 0)
        return index


def _adaln_kernel(c_ref, w_ref, b_ref, o_ref):
    c = c_ref[...]
    a = (c * jax.nn.sigmoid(c)).astype(BF16)
    o_ref[0] = _dot(a, w_ref[0].astype(BF16)) + b_ref[0]


def _adaln(cc, ada_w, ada_b):
    depth, d, n = ada_w.shape
    tn = _pick(n, 1024)
    return pl.pallas_call(
        _adaln_kernel,
        grid=(depth, n // tn),
        in_specs=[pl.BlockSpec((MOD_ROWS, d), lambda l, j: (0, 0)),
                  pl.BlockSpec((1, d, tn), lambda l, j: (l, 0, j)),
                  pl.BlockSpec((1, 1, tn), lambda l, j: (l, 0, j))],
        out_specs=pl.BlockSpec((1, MOD_ROWS, tn), lambda l, j: (l, 0, j)),
        out_shape=jax.ShapeDtypeStruct((depth, MOD_ROWS, n), F32),
        compiler_params=_params(("arbitrary", "arbitrary"), 40),
        name="adaln",
    )(cc, ada_w, ada_b.reshape(depth, 1, n))


def _modulate(x_ref, sh_ref, sc_ref, h_ref):
    @pl.when(pl.program_id(1) == 0)
    def _():
        h_ref[...] = (x_ref[...] * (1.0 + sc_ref[0]) + sh_ref[0]).astype(BF16)


def _segment_pos(shape, ctx_tiles, seg_ctx, seg_lat):
    mask = jnp.where(pl.program_id(0) < ctx_tiles, seg_ctx - 1, seg_lat - 1)
    return lax.broadcasted_iota(jnp.int32, shape, 0) & mask, mask


def _shift_rows(p, pos, seg_last, offset):
    rt = p.shape[0]
    rolled = pltpu.roll(p, (-offset) % rt, 0)
    ok = (pos + offset >= 0) & (pos + offset <= seg_last)
    return jnp.where(ok, rolled, 0.0)


def _col_parts(width, part=256):
    part = min(part, width)
    return [slice(c, c + part) for c in range(0, width, part)]


def _proj_kernel(x_ref, sh_ref, sc_ref, w_ref, o_ref, h_ref):
    _modulate(x_ref, sh_ref, sc_ref, h_ref)
    o_ref[...] = _dot(h_ref[...], w_ref[...]).astype(o_ref.dtype)


def _sconv_in_kernel(x_ref, sh_ref, sc_ref, wb_ref, wc_ref, wv_ref, cw_ref, o_ref, h_ref,
                     *, ctx_tiles, seg_ctx, seg_lat):
    _modulate(x_ref, sh_ref, sc_ref, h_ref)
    h = h_ref[...]
    for cols in _col_parts(o_ref.shape[1]):
        bg = _dot(h, wb_ref[:, cols])
        p = _dot(h, wc_ref[:, cols]) * _dot(h, wv_ref[:, cols])
        pos, last = _segment_pos(p.shape, ctx_tiles, seg_ctx, seg_lat)
        cw = cw_ref[:, cols]
        conv = (cw[0:1] * _shift_rows(p, pos, last, -1) + cw[1:2] * p
                + cw[2:3] * _shift_rows(p, pos, last, 1))
        o_ref[:, cols] = (bg * conv).astype(o_ref.dtype)


def _lru_in_kernel(x_ref, sh_ref, sc_ref, wy_ref, wx_ref, cw_ref, cb_ref, y_ref, xc_ref, h_ref,
                   *, ctx_tiles, seg_ctx, seg_lat):
    _modulate(x_ref, sh_ref, sc_ref, h_ref)
    h = h_ref[...]
    for cols in _col_parts(y_ref.shape[1]):
        y_ref[:, cols] = jax.nn.gelu(_dot(h, wy_ref[:, cols])).astype(y_ref.dtype)
        xb = _dot(h, wx_ref[:, cols])
        pos, last = _segment_pos(xb.shape, ctx_tiles, seg_ctx, seg_lat)
        cw = cw_ref[:, cols]
        conv = (cw[0:1] * _shift_rows(xb, pos, last, -1) + cw[1:2] * xb
                + cw[2:3] * _shift_rows(xb, pos, last, 1) + cw[3:4] * _shift_rows(xb, pos, last, 2))
        xc_ref[:, cols] = (conv + cb_ref[:, cols]).astype(xc_ref.dtype)


def _in_proj(kind, rs, x, mods, layer, weights, n_out, out_dtypes, extra=(), tn_pref=512, rt_pref=1024):
    rows, d = x.shape
    rt = rs.tile(rt_pref)
    tn = _pick(n_out, tn_pref)
    nj = n_out // tn
    in_specs = [pl.BlockSpec((rt, d), lambda i, j: (i, 0)),
                pl.BlockSpec((1, 1, d), rs.mod_map(layer, 0, rt)),
                pl.BlockSpec((1, 1, d), rs.mod_map(layer, 1, rt))]
    args = [x, mods, mods]
    for w, off in weights:
        in_specs.append(pl.BlockSpec((d, tn), functools.partial(lambda i, j, o: (0, o * nj + j), o=off)))
        args.append(w)
    for e in extra:
        in_specs.append(pl.BlockSpec((e.shape[0], tn), lambda i, j: (0, j)))
        args.append(e)
    out_specs = [pl.BlockSpec((rt, tn), lambda i, j: (i, j)) for _ in out_dtypes]
    out_shape = [jax.ShapeDtypeStruct((rows, n_out), dt) for dt in out_dtypes]
    ctx_tiles = rs.ctx_rows // rt
    if kind == "plain":
        body = _proj_kernel
    elif kind == "sconv":
        body = functools.partial(_sconv_in_kernel, ctx_tiles=ctx_tiles, seg_ctx=rs.n_ctx, seg_lat=GRID_W)
    else:
        body = functools.partial(_lru_in_kernel, ctx_tiles=ctx_tiles, seg_ctx=rs.n_ctx, seg_lat=GRID_W)
    single = len(out_dtypes) == 1
    return pl.pallas_call(
        body,
        grid=(rows // rt, nj),
        in_specs=in_specs,
        out_specs=out_specs[0] if single else out_specs,
        out_shape=out_shape[0] if single else out_shape,
        scratch_shapes=[pltpu.VMEM((rt, d), BF16)],
        compiler_params=_params(("arbitrary", "arbitrary"), 52),
        name=kind + "_in_proj",
    )(*args)


def _chunk_tri(n, reverse):
    row = lax.broadcasted_iota(jnp.int32, (n, n), 0)
    col = lax.broadcasted_iota(jnp.int32, (n, n), 1)
    shift = CHUNK.bit_length() - 1
    return ((row >> shift) == (col >> shift)) & ((row <= col) if reverse else (row >= col))


def _chunk_cumsum(tri, g):
    tri_b = jnp.where(tri, 1.0, 0.0).astype(BF16)
    g_hi = g.astype(BF16)
    g_lo = (g - g_hi.astype(F32)).astype(BF16)
    return _dot(tri_b, g_hi) + _dot(tri_b, g_lo)


def _chunk_cumsum_steps(g, reverse):
    n = g.shape[0]
    pos = lax.broadcasted_iota(jnp.int32, g.shape, 0) & (CHUNK - 1)
    k = 1
    while k < CHUNK:
        if reverse:
            g = g + jnp.where(pos < CHUNK - k, pltpu.roll(g, n - k, 0), 0.0)
        else:
            g = g + jnp.where(pos >= k, pltpu.roll(g, k, 0), 0.0)
        k *= 2
    return g


def _gla_gate_kernel(x_ref, sh_ref, sc_ref, wr_ref, w2_ref, b2_ref, o_ref):
    h = (x_ref[...] * (1.0 + sc_ref[0]) + sh_ref[0]).astype(BF16)
    r = _dot(h, wr_ref[...]).astype(BF16)
    pre = _dot(r, w2_ref[...]) + b2_ref[...]
    g = _log_sigmoid(pre) * (1.0 / GLA_NORMALIZER)
    n, half = g.shape[0], g.shape[1] // 2
    o_ref[:, :half] = _chunk_cumsum(_chunk_tri(n, False), g[:, :half])
    o_ref[:, half:] = _chunk_cumsum(_chunk_tri(n, True), g[:, half:])


def _gla_gates(rs, x, mods, layer, w_r, w2, b2):
    rows, d = x.shape
    rt = rs.tile(256)
    n = w2.shape[1]
    return pl.pallas_call(
        _gla_gate_kernel,
        grid=(rows // rt,),
        in_specs=[pl.BlockSpec((rt, d), lambda i: (i, 0)),
                  pl.BlockSpec((1, 1, d), rs.mod_map(layer, 0, rt)),
                  pl.BlockSpec((1, 1, d), rs.mod_map(layer, 1, rt)),
                  pl.BlockSpec(w_r.shape, lambda i: (0, 0)),
                  pl.BlockSpec(w2.shape, lambda i: (0, 0)),
                  pl.BlockSpec((1, n), lambda i: (0, 0))],
        out_specs=pl.BlockSpec((rt, n), lambda i: (i, 0)),
        out_shape=jax.ShapeDtypeStruct((rows, n), F32),
        compiler_params=_params(("arbitrary",), 40),
        name="gla_gates",
    )(x, mods, mods, w_r, w2, b2)


def _gla_scan_kernel(*refs, hgrn, reverse, final, heads, group, dk, dv, n_chunks, qscale):
    if hgrn:
        q_ref, f_ref, v_ref, lb_ref = refs[:4]
        rest = refs[4:]
    else:
        q_ref, k_ref, v_ref, g_ref = refs[:4]
        rest = refs[4:]
    if final:
        og_ref, of_ref, gain_ref, o_ref, st_ref = rest
    else:
        o_ref, st_ref = rest

    @pl.when(pl.program_id(1) == 0)
    def _():
        st_ref[...] = jnp.zeros_like(st_ref)

    tc = n_chunks * CHUNK
    tri = _chunk_tri(tc, reverse)
    order = range(n_chunks - 1, -1, -1) if reverse else range(n_chunks)

    def intra(h):
        kcols = pl.ds(pl.multiple_of(h * dk, dk), dk)
        vcols = pl.ds(pl.multiple_of(h * dv, dv), dv)
        qf = q_ref[:, kcols].astype(F32)
        if hgrn:
            lb = lb_ref[:, kcols]
            s = jax.nn.sigmoid(f_ref[:, kcols].astype(F32))
            qf = qf * jax.nn.sigmoid(qf)
            kf = (1.0 - lb) * (1.0 - s)
            b = _chunk_cumsum_steps(jnp.log(lb + (1.0 - lb) * s), reverse)
        else:
            kf = k_ref[:, kcols].astype(F32)
            b = g_ref[:, kcols]
        v = v_ref[:, vcols]
        qd = (qf * (qscale * jnp.exp(b))).astype(BF16)
        kdf = kf * jnp.exp(-b)
        att = jnp.where(tri, _dot_nt(qd, kdf.astype(BF16)), 0.0).astype(BF16)
        return vcols, b, qd, kdf, v, _dot(att, v)

    def finish(vcols, outs):
        o = jnp.concatenate(outs, axis=0)
        if final:
            o = o + of_ref[:, vcols]
            o = o * lax.rsqrt(jnp.mean(o * o, axis=-1, keepdims=True) + RMS_EPS) * gain_ref[...]
            og = og_ref[:, vcols].astype(F32)
            o = o * (og * jax.nn.sigmoid(og))
        o_ref[:, vcols] = o.astype(o_ref.dtype)

    def head_group(i, carry):
        hs = [i * group + j for j in range(group)]
        parts = [intra(h) for h in hs]
        states = [st_ref[h] for h in hs]
        outs = [[None] * n_chunks for _ in hs]
        for c in order:
            rows = slice(c * CHUNK, (c + 1) * CHUNK)
            end = c * CHUNK if reverse else (c + 1) * CHUNK - 1
            for j, (_, b, qd, kdf, v, o_intra) in enumerate(parts):
                st = states[j]
                outs[j][c] = o_intra[rows] + _dot_nt(qd[rows], st.astype(BF16))
                dec = jnp.exp(b[end:end + 1])
                states[j] = st * dec + _dot_tn(v[rows], (kdf[rows] * dec).astype(BF16))
        for j, h in enumerate(hs):
            st_ref[h] = states[j]
            finish(parts[j][0], outs[j])
        return carry

    lax.fori_loop(0, heads // group, head_group, 0)


def _gla_scan(rs, hgrn, reverse, qkv, col_blocks, extra, heads, dk, dv, qscale, final_args=None):
    tc = min(256, rs.n_ctx)
    ctx_blocks, lat_blocks = rs.n_ctx // tc, rs.n_lat // tc
    steps = ctx_blocks + lat_blocks
    kd_w, vd_w = heads * dk, heads * dv

    def row_block(b, s):
        if reverse:
            ctx = b * ctx_blocks + (ctx_blocks - 1 - s)
            lat = rs.ctx_rows // tc + b * lat_blocks + (steps - 1 - s)
        else:
            ctx = b * ctx_blocks + s
            lat = rs.ctx_rows // tc + b * lat_blocks + (s - ctx_blocks)
        return jnp.where(s < ctx_blocks, ctx, lat)

    def spec(width, cb):
        return pl.BlockSpec((tc, width), lambda b, s: (row_block(b, s), cb))

    in_specs = [spec(kd_w, qkv[0][1]), spec(kd_w, qkv[1][1]), spec(vd_w, qkv[2][1])]
    args = [qkv[0][0], qkv[1][0], qkv[2][0]]
    if hgrn:
        in_specs.append(pl.BlockSpec((1, kd_w), lambda b, s: (0, 0)))
    else:
        in_specs.append(spec(kd_w, col_blocks))
    args.append(extra)
    final = final_args is not None
    if final:
        z, gcb, o_fwd, gain = final_args
        in_specs += [spec(vd_w, gcb), spec(vd_w, 0), pl.BlockSpec((1, dv), lambda b, s: (0, 0))]
        args += [z, o_fwd, gain]
    return pl.pallas_call(
        functools.partial(_gla_scan_kernel, hgrn=hgrn, reverse=reverse, final=final, heads=heads,
                          group=8 if heads % 8 == 0 else 4 if heads % 4 == 0 else 2 - heads % 2,
                          dk=dk, dv=dv, n_chunks=tc // CHUNK, qscale=qscale),
        grid=(rs.batch, steps),
        in_specs=in_specs,
        out_specs=spec(vd_w, 0),
        out_shape=jax.ShapeDtypeStruct((rs.rows, vd_w), BF16 if final else F32),
        scratch_shapes=[pltpu.VMEM((heads, dv, dk), F32)],
        compiler_params=_params(("arbitrary", "arbitrary"), 40),
        name=("hgrn" if hgrn else "gla") + ("_bwd" if reverse else "_fwd"),
    )(*args)


def _lru_scan_kernel(*refs, reverse, final, n_blk, bw):
    if final:
        xc_ref, wg_ref, bg_ref, ls_ref, hf_ref, y_ref, o_ref, a_ref, u_ref, h_ref = refs
    else:
        xc_ref, wg_ref, bg_ref, ls_ref, o_ref, a_ref, u_ref, h_ref = refs
    tc, w = xc_ref.shape
    groups = tc // 8

    @pl.when(pl.program_id(1) == 0)
    def _():
        h_ref[...] = jnp.zeros_like(h_ref)

    for n in range(n_blk):
        cols = slice(n * bw, (n + 1) * bw)
        xb = xc_ref[:, cols]
        gr = jax.nn.sigmoid(_dot(xb, wg_ref[0, n]) + bg_ref[0:1, cols])
        gi = jax.nn.sigmoid(_dot(xb, wg_ref[1, n]) + bg_ref[1:2, cols])
        log_a = RG_C * gr * ls_ref[:, cols]
        a = jnp.exp(log_a)
        a_ref[:, cols] = a
        u_ref[:, cols] = jnp.sqrt(-jnp.tanh(log_a) * (a * a + 1.0)) * gi * xb.astype(F32)

    sub = lax.broadcasted_iota(jnp.int32, (8, w), 0)

    def group(i, h):
        g = (groups - 1 - i) if reverse else i
        rows = pl.ds(pl.multiple_of(g * 8, 8), 8)
        a, u = a_ref[rows, :], u_ref[rows, :]
        for k in (1, 2, 4):
            if reverse:
                ok = sub < 8 - k
                a_s, u_s = pltpu.roll(a, 8 - k, 0), pltpu.roll(u, 8 - k, 0)
            else:
                ok = sub >= k
                a_s, u_s = pltpu.roll(a, k, 0), pltpu.roll(u, k, 0)
            u = jnp.where(ok, a * u_s + u, u)
            a = jnp.where(ok, a * a_s, a)
        out = a * h + u
        last = out[0:1] if reverse else out[7:8]
        res = out
        if final:
            res = (out + hf_ref[rows, :]) * y_ref[rows, :].astype(F32)
        o_ref[rows, :] = res.astype(o_ref.dtype)
        return jnp.broadcast_to(last, (8, w))

    h_ref[...] = lax.fori_loop(0, groups, group, h_ref[...])


def _lru_scan(rs, reverse, xc, w_gate, b_gate, log_sig_lam, final_args=None):
    tc = min(256, rs.n_ctx)
    ctx_blocks, lat_blocks = rs.n_ctx // tc, rs.n_lat // tc
    steps = ctx_blocks + lat_blocks
    w = xc.shape[1]
    n_blk, bw = w_gate.shape[1], w_gate.shape[2]

    def row_block(b, s):
        if reverse:
            ctx = b * ctx_blocks + (ctx_blocks - 1 - s)
            lat = rs.ctx_rows // tc + b * lat_blocks + (steps - 1 - s)
        else:
            ctx = b * ctx_blocks + s
            lat = rs.ctx_rows // tc + b * lat_blocks + (s - ctx_blocks)
        return jnp.where(s < ctx_blocks, ctx, lat)

    stream = pl.BlockSpec((tc, w), lambda b, s: (row_block(b, s), 0))
    in_specs = [stream,
                pl.BlockSpec(w_gate.shape, lambda b, s: (0, 0, 0, 0)),
                pl.BlockSpec((2, w), lambda b, s: (0, 0)),
                pl.BlockSpec((1, w), lambda b, s: (0, 0))]
    args = [xc, w_gate, b_gate, log_sig_lam]
    final = final_args is not None
    if final:
        in_specs += [stream, stream]
        args += list(final_args)
    return pl.pallas_call(
        functools.partial(_lru_scan_kernel, reverse=reverse, final=final, n_blk=n_blk, bw=bw),
        grid=(rs.batch, steps),
        in_specs=in_specs,
        out_specs=stream,
        out_shape=jax.ShapeDtypeStruct((rs.rows, w), BF16 if final else F32),
        scratch_shapes=[pltpu.VMEM((tc, w), F32), pltpu.VMEM((tc, w), F32), pltpu.VMEM((8, w), F32)],
        compiler_params=_params(("arbitrary", "arbitrary"), 40),
        name="lru_bwd" if reverse else "lru_fwd",
    )(*args)


def _layer_norm(z, g, b):
    mu = jnp.mean(z, axis=-1, keepdims=True)
    zc = z - mu
    var = jnp.mean(zc * zc, axis=-1, keepdims=True)
    return zc * lax.rsqrt(var + LN_EPS) * g + b


def _pack_words(v):
    half = v.shape[1] // 2
    vb = v.astype(BF16).astype(F32)
    lo = lax.bitcast_convert_type(vb[:, :half], jnp.uint32) >> 16
    hi = lax.bitcast_convert_type(vb[:, half:], jnp.uint32) & jnp.uint32(0xFFFF0000)
    return hi | lo


def _store_packed(ref, row0, n, words):
    s_rows = words.shape[1] // 128
    for s in range(s_rows):
        ref[pl.ds(row0 * s_rows + s, n, stride=s_rows), :] = words[:, s * 128:(s + 1) * 128]


def _load_packed(ref, row0, n, s_rows):
    lo, hi = [], []
    for s in range(s_rows):
        w = ref[pl.ds(row0 * s_rows + s, n, stride=s_rows), :]
        lo.append(lax.bitcast_convert_type(w << 16, F32))
        hi.append(lax.bitcast_convert_type(w & jnp.uint32(0xFFFF0000), F32))
    return lo, hi


def _out_proj_kernel(a_ref, w_ref, x_ref, gate_ref, sh_ref, sc_ref, lg_ref, lb_ref, wr_ref, br_ref,
                     xo_ref, hp_ref, lt_ref, ya_ref, yb_ref, *, alpha):
    i = pl.program_id(0)

    @pl.when(i == 0)
    def _():
        yb_ref[...] = jnp.zeros_like(yb_ref)

    def step(prev_ref, new_ref):
        new_ref[...] = _dot(a_ref[...], w_ref[...])
        x_new = _layer_norm(alpha * x_ref[...] + gate_ref[0] * prev_ref[...], lg_ref[...], lb_ref[...])
        xo_ref[...] = x_new
        h = x_new * (1.0 + sc_ref[0]) + sh_ref[0]
        _store_packed(hp_ref, 0, h.shape[0], _pack_words(h))
        lt_ref[...] = _dot_nt(wr_ref[...], h.astype(BF16)) + br_ref[...]

    @pl.when(i % 2 == 0)
    def _():
        step(yb_ref, ya_ref)

    @pl.when(i % 2 == 1)
    def _():
        step(ya_ref, yb_ref)


def _out_proj(rs, a, w_out, x, mods, layer, ln_g, ln_b, w_router_t, b_router, alpha, skip_rows=0):
    rows, d = rs.rows, x.shape[1]
    k = a.shape[1]
    n_e = w_router_t.shape[0]
    rt = rs.tile(256)
    n_tiles = rows // rt
    skip = skip_rows // rt
    lag = lambda i: jnp.maximum(i - 1, 0)
    row = lambda i: (lag(i), 0)
    const = lambda i: (0, 0)
    mod = lambda which: (lambda i: rs.mod_map(layer, which, rt)(lag(i)))
    return pl.pallas_call(
        functools.partial(_out_proj_kernel, alpha=alpha),
        grid=(n_tiles + 1,),
        in_specs=[pl.BlockSpec((rt, k), lambda i: (jnp.minimum(i, n_tiles - 1) + skip, 0)),
                  pl.BlockSpec((k, d), const),
                  pl.BlockSpec((rt, d), lambda i: (lag(i) + skip, 0)),
                  pl.BlockSpec((1, 1, d), mod(2)),
                  pl.BlockSpec((1, 1, d), mod(3)),
                  pl.BlockSpec((1, 1, d), mod(4)),
                  pl.BlockSpec((1, d), const),
                  pl.BlockSpec((1, d), const),
                  pl.BlockSpec((n_e, d), const),
                  pl.BlockSpec((n_e, 1), const)],
        out_specs=[pl.BlockSpec((rt, d), row),
                   pl.BlockSpec((rt * (d // 256), 128), row),
                   pl.BlockSpec((n_e, rt), lambda i: (0, lag(i)))],
        out_shape=[jax.ShapeDtypeStruct((rows, d), F32),
                   jax.ShapeDtypeStruct((rows * (d // 256), 128), jnp.uint32),
                   jax.ShapeDtypeStruct((n_e, rows), F32)],
        scratch_shapes=[pltpu.VMEM((rt, d), F32), pltpu.VMEM((rt, d), F32)],
        compiler_params=_params(("arbitrary",), 52),
        name="out_proj_ln",
    )(a, w_out, x, mods, mods, mods, ln_g, ln_b, w_router_t, b_router)


def _route_kernel(lt_ref, e_ref, g_ref, r_ref, cnt_ref, carry_ref):
    n_e, tr = lt_ref.shape

    @pl.when(pl.program_id(0) == 0)
    def _():
        carry_ref[...] = jnp.zeros_like(carry_ref)

    lg = lt_ref[...]
    eid = lax.broadcasted_iota(jnp.int32, (n_e, tr), 0)
    vals, idxs, sels = [], [], []
    for _ in range(TOP_K):
        m = jnp.max(lg, axis=0, keepdims=True)
        idx = jnp.min(jnp.where(lg == m, eid, n_e), axis=0, keepdims=True)
        sel = eid == idx
        vals.append(m)
        idxs.append(idx)
        sels.append(sel)
        lg = jnp.where(sel, -jnp.inf, lg)
    ex = [jnp.exp(v - vals[0]) for v in vals]
    denom = ex[0] + ex[1] + ex[2] + ex[3]
    chosen = sels[0] | sels[1] | sels[2] | sels[3]
    member = jnp.where(chosen, 1.0, 0.0)
    srow = lax.broadcasted_iota(jnp.int32, (tr, tr), 0)
    scol = lax.broadcasted_iota(jnp.int32, (tr, tr), 1)
    before = jnp.where(srow < scol, 1.0, 0.0).astype(BF16)
    prefix = _dot(member.astype(BF16), before) + carry_ref[...]
    for k in range(TOP_K):
        e_ref[k:k + 1, :] = idxs[k]
        g_ref[k:k + 1, :] = ex[k] / denom
        r_ref[k:k + 1, :] = jnp.sum(jnp.where(sels[k], prefix, 0.0), axis=0, keepdims=True).astype(jnp.int32)
    carry_ref[...] = carry_ref[...] + jnp.sum(member, axis=1, keepdims=True)
    cnt_ref[...] = carry_ref[...]


def _route(logits_t):
    n_e, rows = logits_t.shape
    tr = _pick(rows, 512)
    tok = pl.BlockSpec((TOP_K, tr), lambda i: (0, i))
    return pl.pallas_call(
        _route_kernel,
        grid=(rows // tr,),
        in_specs=[pl.BlockSpec((n_e, tr), lambda i: (0, i))],
        out_specs=[tok, tok, tok, pl.BlockSpec((n_e, 1), lambda i: (0, 0))],
        out_shape=[jax.ShapeDtypeStruct((TOP_K, rows), jnp.int32),
                   jax.ShapeDtypeStruct((TOP_K, rows), F32),
                   jax.ShapeDtypeStruct((TOP_K, rows), jnp.int32),
                   jax.ShapeDtypeStruct((n_e, 1), F32)],
        scratch_shapes=[pltpu.VMEM((n_e, 1), F32)],
        compiler_params=_params(("arbitrary",), 32),
        name="route",
    )(logits_t)


def _row_slab(ref, row, s_rows):
    return ref.at[pl.ds(pl.multiple_of(row * s_rows, s_rows), s_rows), :]


def _dispatch_kernel(pe_ref, pd_ref, dest_ref, hp_ref, xg_ref, idx_ref, zero_ref, sem, idx_sem,
                     *, rt, bm, s_rows, n_e, n_blocks):
    @pl.when(pl.program_id(0) == 0)
    def _():
        zero_ref[...] = jnp.zeros_like(zero_ref)
        def fill(last_row):
            first = pl.multiple_of((last_row - bm) * s_rows, s_rows)
            return pltpu.make_async_copy(zero_ref, xg_ref.at[pl.ds(first, bm * s_rows), :], sem)

        def start(e, c):
            @pl.when(pd_ref[e] > 0)
            def _():
                fill(pe_ref[e]).start()
            return c

        def wait(e, c):
            @pl.when(pd_ref[e] > 0)
            def _():
                fill(pe_ref[e]).wait()
            return c

        lax.fori_loop(0, n_e, start, 0)
        lax.fori_loop(0, n_e, wait, 0)
        n_used = pe_ref[n_e - 1] // bm
        lax.fori_loop(n_used, n_blocks, lambda j, c: (fill((j + 1) * bm).start(), c)[1], 0)
        lax.fori_loop(n_used, n_blocks, lambda j, c: (fill((j + 1) * bm).wait(), c)[1], 0)

    load = pltpu.make_async_copy(dest_ref, idx_ref, idx_sem)
    load.start()
    load.wait()

    def token(t, c):
        src = _row_slab(hp_ref, t, s_rows)
        for k in range(TOP_K):
            pltpu.make_async_copy(src, _row_slab(xg_ref, idx_ref[k, t], s_rows), sem).start(priority=k % 2)
        return c

    lax.fori_loop(0, rt, token, 0)
    for _ in range(TOP_K):
        pltpu.make_async_copy(hp_ref, xg_ref.at[pl.ds(0, rt * s_rows), :], sem).wait()


def _dispatch(hp, dest, pad_end, padded, n_rows, bm, rt):
    s_rows = hp.shape[0] // dest.shape[1]
    rows = dest.shape[1]
    n_e = pad_end.shape[0]
    grid_spec = pltpu.PrefetchScalarGridSpec(
        num_scalar_prefetch=2,
        grid=(rows // rt,),
        in_specs=[pl.BlockSpec((TOP_K, rt), lambda i, pe, pd: (0, i)),
                  pl.BlockSpec((rt * s_rows, 128), lambda i, pe, pd: (i, 0))],
        out_specs=pl.BlockSpec(memory_space=pl.ANY),
        scratch_shapes=[pltpu.SMEM((TOP_K, rt), jnp.int32),
                        pltpu.VMEM((bm * s_rows, 128), jnp.uint32),
                        pltpu.SemaphoreType.DMA,
                        pltpu.SemaphoreType.DMA],
    )
    return pl.pallas_call(
        functools.partial(_dispatch_kernel, rt=rt, bm=bm, s_rows=s_rows, n_e=n_e, n_blocks=n_rows // bm),
        grid_spec=grid_spec,
        out_shape=jax.ShapeDtypeStruct((n_rows * s_rows, 128), jnp.uint32),
        compiler_params=_params(("arbitrary",), 32),
        name="dispatch",
    )(pad_end, padded, dest, hp)


N_STAGE = 8

def _expert_kernel(be_ref, nb_ref, nx_ref, c0_ref, c1_ref, par_ref, nv_ref,
                   x_ref, wgu_hbm, bgu_ref, wd_hbm, bd_ref, o_ref,
                   wgu_s, wd_s, xb_s, st_ref, sem,
                   *, layer, d_ff, bm, s_rows, wrows, n_gu, n_d):
    i = pl.program_id(0)
    n_chunks = n_gu + n_d

    ff2, d = wgu_s.shape[2], wd_s.shape[2]

    def gu_copy(e, c):
        rows = pl.ds(pl.multiple_of(c * wrows, wrows), wrows)
        slot = c % N_STAGE
        return pltpu.make_async_copy(wgu_hbm.at[layer, e, rows, :], st_ref.at[slot, :, pl.ds(0, ff2)], sem.at[slot])

    def down_copy(e, c):
        rows = pl.ds(pl.multiple_of((c - n_gu) * wrows, wrows), wrows)
        slot = c % N_STAGE
        return pltpu.make_async_copy(wd_hbm.at[layer, e, rows, :], st_ref.at[slot, :, pl.ds(0, d)], sem.at[slot])

    def start(e, c):
        @pl.when(c < n_gu)
        def _():
            gu_copy(e, c).start()

        @pl.when((c >= n_gu) & (c < n_chunks))
        def _():
            down_copy(e, c).start()

    def land(e, c, buf):
        @pl.when(c < n_gu)
        def _():
            gu_copy(e, c).wait()
            rows = pl.ds(pl.multiple_of(c * wrows, wrows), wrows)
            wgu_s[buf, rows, :] = st_ref[c % N_STAGE, :, 0:ff2].astype(BF16)

        @pl.when(c >= n_gu)
        def _():
            down_copy(e, c).wait()
            rows = pl.ds(pl.multiple_of((c - n_gu) * wrows, wrows), wrows)
            wd_s[buf, rows, :] = st_ref[c % N_STAGE, :, 0:d].astype(BF16)

    def prime(e):
        for c in range(N_STAGE - 1):
            start(e, jnp.int32(c))

    def stream(e, lo, hi, buf):
        def body(c, carry):
            start(e, c + (N_STAGE - 1))
            land(e, c, buf)
            return carry
        lax.fori_loop(lo, hi, body, 0)

    @pl.when(i < nb_ref[0])
    def _():
        e, nxt, cur = be_ref[i], nx_ref[i], par_ref[i]
        first = (i == 0) | (e != be_ref[jnp.maximum(i - 1, 0)])

        @pl.when(i == 0)
        def _():
            prime(e)
            stream(e, 0, n_chunks, cur)

        @pl.when(first & (nxt >= 0))
        def _():
            prime(nxt)

        @pl.when(nxt >= 0)
        def _():
            stream(nxt, c0_ref[i], c1_ref[i], 1 - cur)

        half = s_rows * 128

        def ffn(rows):
            lo, hi = _load_packed(x_ref, 0, rows, s_rows)
            for s in range(s_rows):
                xb_s[0:rows, s * 128:(s + 1) * 128] = lo[s].astype(BF16)
                xb_s[0:rows, half + s * 128:half + (s + 1) * 128] = hi[s].astype(BF16)
            gu = _dot(xb_s[0:rows, :], wgu_s[cur]) + bgu_ref[0]
            g = jnp.minimum(gu[:, :d_ff], SWIGLU_LIMIT)
            u = jnp.clip(gu[:, d_ff:], -SWIGLU_LIMIT, SWIGLU_LIMIT)
            act = (g * jax.nn.sigmoid(SWIGLU_ALPHA * g) * (u + 1.0)).astype(BF16)
            y = _dot(act, wd_s[cur]) + bd_ref[0]
            _store_packed(o_ref, 0, rows, _pack_words(y))
            if rows < bm:
                o_ref[rows * s_rows:bm * s_rows, :] = jnp.zeros(((bm - rows) * s_rows, 128), o_ref.dtype)

        @pl.when(nv_ref[i] > bm // 2)
        def _():
            ffn(bm)

        @pl.when(nv_ref[i] <= bm // 2)
        def _():
            ffn(bm // 2)

    @pl.when(i >= nb_ref[0])
    def _():
        o_ref[...] = jnp.zeros_like(o_ref)


def _experts(xg, counts, pad_end, padded, layer, w_gu, b_gu, w_down, b_down, bm):
    depth, n_e, d, ff2 = w_gu.shape
    d_ff = ff2 // 2
    s_rows = d // 256
    n_blocks = xg.shape[0] // (bm * s_rows)
    wrows = _pick(d_ff, 128)
    n_gu, n_d = d // wrows, d_ff // wrows
    n_chunks = n_gu + n_d

    experts = jnp.arange(n_e, dtype=jnp.int32)
    block_start = jnp.arange(n_blocks, dtype=jnp.int32) * bm
    block_e = jnp.minimum(jnp.sum(pad_end[None, :] <= block_start[:, None], axis=1), n_e - 1).astype(jnp.int32)
    n_used = (pad_end[-1:] // bm).astype(jnp.int32)
    onehot = block_e[:, None] == experts[None, :]
    pick = lambda table: jnp.sum(jnp.where(onehot, table[None, :], 0), axis=1)
    blocks_of = padded // bm
    local = jnp.arange(n_blocks, dtype=jnp.int32) - pick((pad_end - padded) // bm)
    nblk = pick(blocks_of)
    later_blocks = jnp.maximum(nblk - 1, 1)
    share = lambda l: jnp.where(nblk > 1, (n_chunks * jnp.maximum(l, 0)) // later_blocks, n_chunks * (l + 1))
    c0, c1 = share(local - 1), share(local)
    live = blocks_of > 0
    later = (experts[None, :] > experts[:, None]) & live[None, :]
    next_of = jnp.min(jnp.where(later, experts[None, :], n_e), axis=1)
    next_of = jnp.where(next_of == n_e, -1, next_of)
    ordinal = jnp.cumsum(live.astype(jnp.int32)) - live.astype(jnp.int32)
    valid = jnp.clip(pick(counts) - local * bm, 0, bm)
    tables = [block_e, n_used, pick(next_of), c0, c1, pick(ordinal) % 2, valid]
    tables = [t.astype(jnp.int32) for t in tables]

    used = lambda i, nb: jnp.minimum(i, nb[0] - 1)
    bias = lambda i, be, nb, *_: (layer, be[used(i, nb)], 0, 0)
    grid_spec = pltpu.PrefetchScalarGridSpec(
        num_scalar_prefetch=len(tables),
        grid=(n_blocks,),
        in_specs=[pl.BlockSpec((bm * s_rows, 128), lambda i, be, nb, *_: (used(i, nb), 0)),
                  pl.BlockSpec(memory_space=pl.ANY),
                  pl.BlockSpec((None, 1, 1, ff2), bias),
                  pl.BlockSpec(memory_space=pl.ANY),
                  pl.BlockSpec((None, 1, 1, d), bias)],
        out_specs=pl.BlockSpec((bm * s_rows, 128), lambda i, *_: (i, 0)),
        scratch_shapes=[pltpu.VMEM((2, d, ff2), BF16), pltpu.VMEM((2, d_ff, d), BF16), pltpu.VMEM((bm, d), BF16),
                        pltpu.VMEM((N_STAGE, wrows, max(ff2, d)), F32),
                        pltpu.SemaphoreType.DMA((N_STAGE,))],
    )
    return pl.pallas_call(
        functools.partial(_expert_kernel, layer=layer, d_ff=d_ff, bm=bm, s_rows=s_rows,
                          wrows=wrows, n_gu=n_gu, n_d=n_d),
        grid_spec=grid_spec,
        out_shape=jax.ShapeDtypeStruct(xg.shape, jnp.uint32),
        compiler_params=_params(("arbitrary",), 58),
        name="experts",
    )(*tables, xg, w_gu, b_gu.reshape(depth, n_e, 1, ff2), w_down, b_down.reshape(depth, n_e, 1, d))


COMBINE_SUB = 64


def _combine_kernel(dcur_ref, dnext_ref, g_ref, x_ref, gate_ref, lg_ref, lb_ref, y_ref, o_ref,
                    idx_ref, buf_ref, sem, idx_sem, *, alpha, rt, s_rows, n_tiles):
    i = pl.program_id(0)
    cur, nxt = i % 2, (i + 1) % 2
    sub = min(COMBINE_SUB, rt)

    def gather(slot, t, k):
        dst = buf_ref.at[slot, pl.ds(pl.multiple_of((k * rt + t) * s_rows, s_rows), s_rows), :]
        return pltpu.make_async_copy(_row_slab(y_ref, idx_ref[slot, k, t], s_rows), dst, sem.at[slot])

    def retire(slot):
        for k in range(TOP_K):
            part = pl.ds(k * rt * s_rows, rt * s_rows)
            pltpu.make_async_copy(y_ref.at[pl.ds(0, rt * s_rows), :], buf_ref.at[slot, part, :], sem.at[slot]).wait()

    def load_indices(src_ref, slot):
        load = pltpu.make_async_copy(src_ref, idx_ref.at[slot], idx_sem)
        load.start()
        load.wait()

    @pl.when(i == 0)
    def _():
        load_indices(dcur_ref, 0)

        def token(t, c):
            for k in range(TOP_K):
                gather(0, t, k).start(priority=k % 2)
            return c
        lax.fori_loop(0, rt, token, 0)

    load_indices(dnext_ref, nxt)
    retire(cur)

    def rows_step(j, carry):
        r0 = pl.multiple_of(j * sub, sub)
        g = g_ref[pl.ds(r0, sub), :]
        lo_parts, hi_parts = [], []
        for s in range(s_rows):
            for tt in range(s * sub // s_rows, (s + 1) * sub // s_rows):
                for k in range(TOP_K):
                    gather(nxt, r0 + tt, k).start(priority=k % 2)
            lo = hi = None
            for k in range(TOP_K):
                w = buf_ref[cur, pl.ds((k * rt + r0) * s_rows + s, sub, stride=s_rows), :]
                gk = g[:, k:k + 1]
                lo_k = gk * lax.bitcast_convert_type(w << 16, F32)
                hi_k = gk * lax.bitcast_convert_type(w & jnp.uint32(0xFFFF0000), F32)
                lo = lo_k if lo is None else lo + lo_k
                hi = hi_k if hi is None else hi + hi_k
            lo_parts.append(lo)
            hi_parts.append(hi)
        acc = jnp.concatenate(lo_parts + hi_parts, axis=1)
        z = alpha * x_ref[pl.ds(r0, sub), :] + gate_ref[0] * acc
        o_ref[pl.ds(r0, sub), :] = _layer_norm(z, lg_ref[...], lb_ref[...])
        return carry

    lax.fori_loop(0, rt // sub, rows_step, 0)

    @pl.when(i == n_tiles - 1)
    def _():
        retire(nxt)


def _combine(rs, y, dest, gates, x, mods, layer, ln_g, ln_b, alpha):
    rows, d = x.shape
    s_rows = d // 256
    rt = rs.tile(256)
    n_tiles = rows // rt
    row = lambda i: (i, 0)
    const = lambda i: (0, 0)
    return pl.pallas_call(
        functools.partial(_combine_kernel, alpha=alpha, rt=rt, s_rows=s_rows, n_tiles=n_tiles),
        grid=(n_tiles,),
        in_specs=[pl.BlockSpec((TOP_K, rt), lambda i: (0, i)),
                  pl.BlockSpec((TOP_K, rt), lambda i: (0, jnp.minimum(i + 1, n_tiles - 1))),
                  pl.BlockSpec((rt, TOP_K), row),
                  pl.BlockSpec((rt, d), row),
                  pl.BlockSpec((1, 1, d), rs.mod_map(layer, 5, rt)),
                  pl.BlockSpec((1, d), const),
                  pl.BlockSpec((1, d), const),
                  pl.BlockSpec(memory_space=pl.ANY)],
        out_specs=pl.BlockSpec((rt, d), row),
        out_shape=jax.ShapeDtypeStruct((rows, d), F32),
        scratch_shapes=[pltpu.SMEM((2, TOP_K, rt), jnp.int32),
                        pltpu.VMEM((2, TOP_K * rt * s_rows, 128), jnp.uint32),
                        pltpu.SemaphoreType.DMA((2,)),
                        pltpu.SemaphoreType.DMA],
        compiler_params=_params(("arbitrary",), 40),
        name="combine_ln",
    )(dest, dest, gates, x, mods, ln_g, ln_b, y)


def _moe(rs, hp, logits_t, x, mods, layer, ln_g, ln_b, w_gu, b_gu, w_down, b_down, alpha):
    rows = x.shape[0]
    n_e = w_gu.shape[1]
    bm = 512 if rows * TOP_K >= 512 * n_e * 4 else 128
    e_t, g_t, r_t, cnt = _route(logits_t)
    counts = cnt[:, 0].astype(jnp.int32)
    padded = (counts + bm - 1) // bm * bm
    pad_end = jnp.cumsum(padded)
    first = pad_end - padded
    experts = jnp.arange(n_e, dtype=jnp.int32)
    first_of = jnp.sum(jnp.where(e_t[None] == experts[:, None, None], first[:, None, None], 0), axis=0)
    dest = first_of + r_t
    n_blocks = -(-(rows * TOP_K + n_e * (bm - 1)) // bm)
    xg = _dispatch(hp, dest, pad_end, padded, n_blocks * bm, bm, rs.tile(256))
    y = _experts(xg, counts, pad_end, padded, layer, w_gu, b_gu, w_down, b_down, bm)
    return _combine(rs, y, dest, g_t.T, x, mods, layer, ln_g, ln_b, alpha)


def kernel(x, c, ctx, c_ctx, ada_w, ada_b, ln_g, ln_b, sc_w_in, sc_conv, sc_w_out, gla_w_in, gla_w_gate2, gla_b_gate, gla_norm, gla_w_out, lru_w_in, lru_conv, lru_conv_b, lru_w_gate, lru_b_gate, lru_lambda, lru_w_out, hg_w_in, hg_lb_raw, hg_norm, hg_w_out, moe_w_router, moe_b_router, moe_w_gu, moe_b_gu, moe_w_down, moe_b_down):
    batch, n_lat, d = x.shape
    n_ctx = ctx.shape[1]
    depth = ada_w.shape[0]
    alpha = (2 * depth) ** 0.25
    assert GRID_W & (GRID_W - 1) == 0 and n_ctx & (n_ctx - 1) == 0 and batch < MOD_ROWS
    rs = _Rows(batch, n_ctx, n_lat)

    cc = jnp.zeros((MOD_ROWS, d), F32).at[:batch].set(c).at[batch].set(c_ctx)
    mods = _adaln(cc, ada_w, ada_b).reshape(depth * MOD_ROWS * N_MOD, 1, d)
    xs = jnp.concatenate([ctx.reshape(-1, d), x.reshape(-1, d)], axis=0)

    for i in range(depth):
        kind, j = i % 4, i // 4
        with_ctx = i < depth - 1
        if kind == 0:
            w_in = sc_w_in[j].astype(BF16)
            a = _in_proj("sconv", rs, xs, mods, i, [(w_in, 0), (w_in, 1), (w_in, 2)], d, [BF16],
                         extra=[sc_conv[j]])
            w_out = sc_w_out[j]
        elif kind == 1:
            dv = gla_norm.shape[-1]
            dk = dv // 2
            heads = gla_w_gate2.shape[-1] // dk
            kd_w, vd_w = heads * dk, heads * dv
            n_main = 2 * kd_w + 2 * vd_w
            rank = gla_w_gate2.shape[2]
            w_in = gla_w_in[j]
            z = _in_proj("plain", rs, xs, mods, i, [(w_in[:, :n_main].astype(BF16), 0)], n_main, [BF16],
                         tn_pref=1024)
            w2 = jnp.zeros((2 * rank, 2 * kd_w), F32)
            w2 = w2.at[:rank, :kd_w].set(gla_w_gate2[j, 0]).at[rank:, kd_w:].set(gla_w_gate2[j, 1])
            gates = _gla_gates(rs, xs, mods, i, w_in[:, n_main:].astype(BF16), w2.astype(BF16),
                               gla_b_gate[j].reshape(1, 2 * kd_w))
            qkv = [(z, 0), (z, 1), (z, kd_w * 2 // vd_w)]
            scale = dk ** -0.5
            o_f = _gla_scan(rs, False, False, qkv, 0, gates, heads, dk, dv, scale)
            a = _gla_scan(rs, False, True, qkv, 1, gates, heads, dk, dv, scale,
                          final_args=(z, (2 * kd_w + vd_w) // vd_w, o_f, gla_norm[j].reshape(1, dv)))
            w_out = gla_w_out[j]
        elif kind == 2:
            w = lru_w_in.shape[-1] // 2
            w_in = lru_w_in[j].astype(BF16)
            y, xc = _in_proj("lru", rs, xs, mods, i, [(w_in, 0), (w_in, 1)], w, [BF16, BF16],
                             extra=[lru_conv[j], lru_conv_b[j].reshape(1, w)])
            log_sig = _log_sigmoid(lru_lambda[j])
            wg, bgate = lru_w_gate[j].astype(BF16), lru_b_gate[j]
            h_f = _lru_scan(rs, False, xc, wg[0], bgate[0], log_sig[0:1])
            a = _lru_scan(rs, True, xc, wg[1], bgate[1], log_sig[1:2], final_args=(h_f, y))
            w_out = lru_w_out[j]
        else:
            dk = hg_norm.shape[-1]
            heads = d // dk
            z = _in_proj("plain", rs, xs, mods, i, [(hg_w_in[j].astype(BF16), 0)], 5 * d, [BF16], tn_pref=1024)
            p = jax.nn.softmax(hg_lb_raw, axis=0)
            lower = (jnp.cumsum(p, axis=0)[i] - p[0]).reshape(1, d)
            scale = dk ** -0.5
            o_f = _gla_scan(rs, True, False, [(z, 0), (z, 1), (z, 3)], 0, lower, heads, dk, dk, scale)
            a = _gla_scan(rs, True, True, [(z, 0), (z, 2), (z, 3)], 0, lower, heads, dk, dk, scale,
                          final_args=(z, 4, o_f, hg_norm[j].reshape(1, dk)))
            w_out = hg_w_out[j]

        skip = 0
        if not with_ctx:
            skip = rs.ctx_rows
            assert skip % _Rows(batch, 0, n_lat).tile(256) == 0
            rs = _Rows(batch, 0, n_lat)
        xs, hp, logits_t = _out_proj(rs, a, w_out.astype(BF16), xs, mods, i,
                                     ln_g[i, 0].reshape(1, d), ln_b[i, 0].reshape(1, d),
                                     moe_w_router[i].T.astype(BF16), moe_b_router[i].reshape(-1, 1), alpha,
                                     skip_rows=skip)
        xs = _moe(rs, hp, logits_t, xs, mods, i, ln_g[i, 1].reshape(1, d), ln_b[i, 1].reshape(1, d),
                  moe_w_gu, moe_b_gu, moe_w_down, moe_b_down, alpha)
    return xs[rs.ctx_rows:].reshape(batch, n_lat, d)
```
